```python
import math
import jax
import jax.numpy as jnp
from jax import lax
import numpy as np

D_MODEL = 1024
BATCH = 8
SEQ = 2048
DEPTH = 1

S5_WIDTH = D_MODEL // 2
S5_GROUP = 16
S5_GROUPS = S5_WIDTH // S5_GROUP
S5_STATE = 64
DT_MIN = 1e-3
DT_MAX = 1e-1

GLA_HEADS = 4
GLA_DK = D_MODEL // 16
GLA_DV = D_MODEL // 8
GLA_KEY = GLA_HEADS * GLA_DK
GLA_VAL = GLA_HEADS * GLA_DV
GLA_GATE_RANK = 16
GLA_GATE_TAU = 16.0
GLA_CHUNK = 64

N_EXPERTS = 256
TOP_K = 8
N_GROUPS = 8
TOPK_GROUPS = 4
EXPERT_FF = D_MODEL // 4
SHARED_FF = D_MODEL // 4
ROUTE_SCALE = 2.5
MOE_BLOCK = 128

EPS = 1e-6
IN_SPLITS = (S5_WIDTH, GLA_KEY, GLA_KEY, GLA_VAL, GLA_GATE_RANK, GLA_VAL, D_MODEL, D_MODEL)
IN_WIDTH = sum(IN_SPLITS)
IN_OFFSETS = [int(o) for o in np.cumsum(IN_SPLITS)[:-1]]

kernel_name = 'hybrid_s5_gla_moe_block'


def rmsnorm(x, g):
    xf = x.astype(jnp.float32)
    r = lax.rsqrt(jnp.mean(xf * xf, axis=-1, keepdims=True) + EPS)
    return (xf * r).astype(x.dtype) * g


def _cmul(ar, ai, br, bi):
    return ar * br - ai * bi, ar * bi + ai * br


def _affine_combine(left, right):
    a1r, a1i, b1r, b1i = left
    a2r, a2i, b2r, b2i = right
    ar, ai = _cmul(a2r, a2i, a1r, a1i)
    br, bi = _cmul(a2r, a2i, b1r, b1i)
    return ar, ai, br + b2r, bi + b2i


def s5_branch(u, lam_re, lam_im, log_dt, b_re, b_im, c_re, c_im, d_skip, w_glu, b_glu):
    bsz, seq, _ = u.shape
    f32 = jnp.float32
    ug = u.reshape(bsz, seq, S5_GROUPS, S5_GROUP).astype(f32)
    lr = lam_re.astype(f32)
    li = lam_im.astype(f32)
    dt = jnp.exp(log_dt.astype(f32))[:, None]
    mag = jnp.exp(lr * dt)
    abar_re = mag * jnp.cos(li * dt)
    abar_im = mag * jnp.sin(li * dt)
    den = lr * lr + li * li
    num_re = abar_re - 1.0
    coef_re = (num_re * lr + abar_im * li) / den
    coef_im = (abar_im * lr - num_re * li) / den
    bbar_re, bbar_im = _cmul(coef_re[..., None], coef_im[..., None], b_re.astype(f32), b_im.astype(f32))
    bu_re = jnp.einsum('blgh,gph->blgp', ug, bbar_re)
    bu_im = jnp.einsum('blgh,gph->blgp', ug, bbar_im)
    a_re = jnp.broadcast_to(abar_re, (1, seq) + abar_re.shape)
    a_im = jnp.broadcast_to(abar_im, (1, seq) + abar_im.shape)
    _, _, s_re, s_im = lax.associative_scan(_affine_combine, (a_re, a_im, bu_re, bu_im), axis=1)
    y = (jnp.einsum('blgp,ghp->blgh', s_re, c_re.astype(f32))
         - jnp.einsum('blgp,ghp->blgh', s_im, c_im.astype(f32))
         + d_skip.astype(f32) * ug)
    y = jax.nn.gelu(y.reshape(bsz, seq, S5_WIDTH)).astype(u.dtype)
    return y * jax.nn.sigmoid(y @ w_glu + b_glu)


def gla_chunk_scan(q, k, v, log_a):
    f32 = jnp.float32
    q, k, v, log_a = (t.astype(f32) for t in (q, k, v, log_a))
    bsz, nh, seq, dk = q.shape
    dv = v.shape[-1]
    nck = seq // GLA_CHUNK

    def to_chunks(t):
        return jnp.moveaxis(t.reshape(bsz, nh, nck, GLA_CHUNK, t.shape[-1]), 2, 0)

    qc, kc, vc, gc = (to_chunks(t) for t in (q, k, v, log_a))
    bc = jnp.cumsum(gc, axis=3)
    causal = jnp.tril(jnp.ones((GLA_CHUNK, GLA_CHUNK), dtype=bool))[:, :, None]

    def step(state, inp):
        qi, ki, vi, bi = inp
        b_last = bi[:, :, -1:, :]
        o_inter = jnp.einsum('bhcd,bhde->bhce', qi * jnp.exp(bi), state)
        rel = bi[:, :, :, None, :] - bi[:, :, None, :, :]
        decay = jnp.exp(jnp.where(causal, rel, -jnp.inf))
        scores = jnp.einsum('bhid,bhjd,bhijd->bhij', qi, ki, decay)
        out = o_inter + jnp.einsum('bhij,bhje->bhie', scores, vi)
        state = (jnp.exp(b_last[:, :, 0, :])[..., None] * state
                 + jnp.einsum('bhjd,bhje->bhde', ki * jnp.exp(b_last - bi), vi))
        return state, out

    s0 = jnp.zeros((bsz, nh, dk, dv), f32)
    _, oc = lax.scan(step, s0, (qc, kc, vc, bc))
    return jnp.moveaxis(oc, 0, 2).reshape(bsz, nh, seq, dv)


def gla_branch(q, k, v, gk_low, r, w_gk2, b_gk2, norm_g):
    bsz, seq, _ = q.shape

    def heads(t, dh):
        return t.reshape(bsz, seq, GLA_HEADS, dh).transpose(0, 2, 1, 3)

    log_a = jax.nn.log_sigmoid((gk_low @ w_gk2 + b_gk2).astype(jnp.float32)) / GLA_GATE_TAU
    o = gla_chunk_scan(heads(q, GLA_DK) * (GLA_DK ** -0.5), heads(k, GLA_DK),
                       heads(v, GLA_DV), heads(log_a, GLA_DK))
    o = rmsnorm(o.astype(q.dtype), norm_g)
    o = o.transpose(0, 2, 1, 3).reshape(bsz, seq, GLA_VAL)
    return o * jax.nn.silu(r)


def moe_ffn(h, w_router, router_bias, w_gate, w_up, w_down, ws_gate, ws_up, ws_down):
    bsz, seq, d = h.shape
    t = bsz * seq
    hf = h.reshape(t, d)
    scores = jax.nn.sigmoid((hf @ w_router).astype(jnp.float32))
    biased = scores + router_bias.astype(jnp.float32)
    grp = biased.reshape(t, N_GROUPS, N_EXPERTS // N_GROUPS)
    grp_score = lax.top_k(grp, 2)[0].sum(-1)
    _, grp_idx = lax.top_k(grp_score, TOPK_GROUPS)
    grp_mask = jnp.any(grp_idx[:, :, None] == jnp.arange(N_GROUPS)[None, None, :], axis=1)
    exp_mask = jnp.repeat(grp_mask, N_EXPERTS // N_GROUPS, axis=1)
    _, idx = lax.top_k(jnp.where(exp_mask, biased, -jnp.inf), TOP_K)
    wts = jnp.take_along_axis(scores, idx, axis=1)
    wts = wts / jnp.sum(wts, axis=-1, keepdims=True) * ROUTE_SCALE

    n_assign = t * TOP_K
    e_flat = idx.reshape(-1)
    tok_flat = jnp.repeat(jnp.arange(t, dtype=jnp.int32), TOP_K)
    w_flat = wts.reshape(-1)
    order = jnp.argsort(e_flat)
    e_s, tok_s, w_s = e_flat[order], tok_flat[order], w_flat[order]
    counts = jnp.bincount(e_flat, length=N_EXPERTS)
    padded = (counts + MOE_BLOCK - 1) // MOE_BLOCK * MOE_BLOCK
    pad_end = jnp.cumsum(padded)
    pad_start = pad_end - padded
    start = jnp.cumsum(counts) - counts
    dest = pad_start[e_s] + (jnp.arange(n_assign, dtype=jnp.int32) - start[e_s])
    n_slots = -(-(n_assign + N_EXPERTS * (MOE_BLOCK - 1)) // MOE_BLOCK) * MOE_BLOCK
    n_blocks = n_slots // MOE_BLOCK
    slot_tok = jnp.full((n_slots,), t, jnp.int32).at[dest].set(tok_s)
    slot_w = jnp.zeros((n_slots,), jnp.float32).at[dest].set(w_s)
    block_exp = jnp.minimum(
        jnp.searchsorted(pad_end, jnp.arange(n_blocks, dtype=jnp.int32) * MOE_BLOCK, side='right'),
        N_EXPERTS - 1)
    h_pad = jnp.concatenate([hf, jnp.zeros((1, d), hf.dtype)], axis=0)

    def block_step(acc, blk):
        toks, bw, e = blk
        xb = h_pad[toks]
        yb = (jax.nn.silu(xb @ w_gate[e]) * (xb @ w_up[e])) @ w_down[e]
        return acc.at[toks].add(yb * bw[:, None].astype(yb.dtype)), None

    acc0 = jnp.zeros((t + 1, d), hf.dtype)
    routed, _ = lax.scan(block_step, acc0, (slot_tok.reshape(n_blocks, MOE_BLOCK),
                                            slot_w.reshape(n_blocks, MOE_BLOCK), block_exp))
    shared = (jax.nn.silu(hf @ ws_gate) * (hf @ ws_up)) @ ws_down
    return (routed[:t] + shared).reshape(bsz, seq, d)


def setup_inputs(seed: int = 0) -> dict:
    key = jax.random.key(seed)
    ks = iter(jax.random.split(key, 40))
    f32 = jnp.float32

    def nrm(shape, scale):
        return scale * jax.random.normal(next(ks), shape, f32)

    L = DEPTH
    n_idx = jnp.arange(S5_STATE, dtype=f32)
    inp = {}
    inp['x'] = nrm((BATCH, SEQ, D_MODEL), 1.0)
    inp['c'] = nrm((BATCH, D_MODEL), 1.0)
    inp['ada_w'] = nrm((L, D_MODEL, 6 * D_MODEL), 0.5 * D_MODEL ** -0.5)
    inp['ada_b'] = nrm((L, 6 * D_MODEL), 0.02)
    inp['norm1_g'] = 1.0 + nrm((L, D_MODEL), 0.05)
    inp['w_in'] = nrm((L, D_MODEL, IN_WIDTH), D_MODEL ** -0.5)
    inp['s5_lam_re'] = -0.5 + nrm((L, S5_GROUPS, S5_STATE), 0.01)
    inp['s5_lam_im'] = jnp.pi * n_idx + nrm((L, S5_GROUPS, S5_STATE), 0.01)
    inp['s5_log_dt'] = jax.random.uniform(next(ks), (L, S5_GROUPS), f32,
                                          math.log(DT_MIN), math.log(DT_MAX))
    inp['s5_b_re'] = nrm((L, S5_GROUPS, S5_STATE, S5_GROUP), (2 * S5_GROUP) ** -0.5)
    inp['s5_b_im'] = nrm((L, S5_GROUPS, S5_STATE, S5_GROUP), (2 * S5_GROUP) ** -0.5)
    inp['s5_c_re'] = nrm((L, S5_GROUPS, S5_GROUP, S5_STATE), (2 * S5_STATE) ** -0.5)
    inp['s5_c_im'] = nrm((L, S5_GROUPS, S5_GROUP, S5_STATE), (2 * S5_STATE) ** -0.5)
    inp['s5_d'] = nrm((L, S5_GROUPS, S5_GROUP), 1.0)
    inp['s5_w_glu'] = nrm((L, S5_WIDTH, S5_WIDTH), S5_WIDTH ** -0.5)
    inp['s5_b_glu'] = nrm((L, S5_WIDTH), 0.02)
    inp['w_proj_a'] = nrm((L, S5_WIDTH, D_MODEL), S5_WIDTH ** -0.5)
    inp['gla_w_gk2'] = nrm((L, GLA_GATE_RANK, GLA_KEY), GLA_GATE_RANK ** -0.5)
    inp['gla_b_gk2'] = nrm((L, GLA_KEY), 0.1)
    inp['gla_norm_g'] = 1.0 + nrm((L, GLA_DV), 0.05)
    inp['w_proj_b'] = nrm((L, GLA_VAL, D_MODEL), GLA_VAL ** -0.5)
    inp['w_out'] = nrm((L, D_MODEL, D_MODEL), D_MODEL ** -0.5)
    inp['norm2_g'] = 1.0 + nrm((L, D_MODEL), 0.05)
    inp['router_w'] = nrm((L, D_MODEL, N_EXPERTS), D_MODEL ** -0.5)
    inp['router_bias'] = nrm((L, N_EXPERTS), 0.01)
    inp['exp_w_gate'] = nrm((L, N_EXPERTS, D_MODEL, EXPERT_FF), D_MODEL ** -0.5)
    inp['exp_w_up'] = nrm((L, N_EXPERTS, D_MODEL, EXPERT_FF), D_MODEL ** -0.5)
    inp['exp_w_down'] = nrm((L, N_EXPERTS, EXPERT_FF, D_MODEL), EXPERT_FF ** -0.5)
    inp['sh_w_gate'] = nrm((L, D_MODEL, SHARED_FF), D_MODEL ** -0.5)
    inp['sh_w_up'] = nrm((L, D_MODEL, SHARED_FF), D_MODEL ** -0.5)
    inp['sh_w_down'] = nrm((L, SHARED_FF, D_MODEL), SHARED_FF ** -0.5)
    inp['final_g'] = 1.0 + nrm((D_MODEL,), 0.05)
    return inp


def reference(x, c, ada_w, ada_b, norm1_g, w_in, s5_lam_re, s5_lam_im, s5_log_dt, s5_b_re, s5_b_im,
              s5_c_re, s5_c_im, s5_d, s5_w_glu, s5_b_glu, w_proj_a, gla_w_gk2, gla_b_gk2, gla_norm_g,
              w_proj_b, w_out, norm2_g, router_w, router_bias, exp_w_gate, exp_w_up, exp_w_down,
              sh_w_gate, sh_w_up, sh_w_down, final_g):
    for l in range(DEPTH):
        mod = jax.nn.silu(c) @ ada_w[l] + ada_b[l]
        sh1, sc1, g1, sh2, sc2, g2 = jnp.split(mod[:, None, :], 6, axis=-1)
        h = rmsnorm(x, norm1_g[l]) * (1.0 + sc1) + sh1
        u_s5, q, k, v, gk_low, r, gate_a, gate_b = jnp.split(h @ w_in[l], IN_OFFSETS, axis=-1)
        y_a = s5_branch(u_s5, s5_lam_re[l], s5_lam_im[l], s5_log_dt[l], s5_b_re[l], s5_b_im[l],
                        s5_c_re[l], s5_c_im[l], s5_d[l], s5_w_glu[l], s5_b_glu[l])
        y_b = gla_branch(q, k, v, gk_low, r, gla_w_gk2[l], gla_b_gk2[l], gla_norm_g[l])
        mixed = (jax.nn.sigmoid(gate_a) * (y_a @ w_proj_a[l])
                 + jax.nn.sigmoid(gate_b) * (y_b @ w_proj_b[l]))
        x = x + g1 * (mixed @ w_out[l])
        h = rmsnorm(x, norm2_g[l]) * (1.0 + sc2) + sh2
        x = x + g2 * moe_ffn(h, router_w[l], router_bias[l], exp_w_gate[l], exp_w_up[l],
                             exp_w_down[l], sh_w_gate[l], sh_w_up[l], sh_w_down[l])
    return rmsnorm(x, final_g)
```

```python
import functools

import jax
import jax.numpy as jnp
from jax import lax
from jax.experimental import pallas as pl
from jax.experimental.pallas import tpu as pltpu

F32 = jnp.float32
BF16 = jnp.bfloat16
I32 = jnp.int32

D_MODEL = 1024
BATCH = 8
SEQ = 2048
TOKENS = BATCH * SEQ
S5_WIDTH = 512
S5_GROUP = 16
S5_GROUPS = 32
S5_STATE = 64
S5_COLS = S5_GROUPS * S5_STATE
GLA_HEADS = 4
GLA_DK = 64
GLA_DV = 128
GLA_KEY = GLA_HEADS * GLA_DK
GLA_VAL = GLA_HEADS * GLA_DV
GLA_GATE_RANK = 16
GLA_GATE_TAU = 16.0
GLA_CHUNK = 64
GLA_SUB = 16
N_EXPERTS = 256
TOP_K = 8
N_GROUPS = 8
GROUP_SIZE = N_EXPERTS // N_GROUPS
TOPK_GROUPS = 4
EXPERT_FF = 256
ROUTE_SCALE = 2.5
MOE_BLOCK = 128
EPS = 1e-6
N_SLOTS = -(-(TOKENS * TOP_K + N_EXPERTS * (MOE_BLOCK - 1)) // MOE_BLOCK) * MOE_BLOCK
N_BLOCKS = N_SLOTS // MOE_BLOCK

LANES = 128
VMEM_LIMIT = 56 * 1024 * 1024

ADA_TN = 1536
INPROJ_TM = 512
S5_TC = 64
S5_COL_CHUNK = 512
GLA_TG = 256
MERGE_TM = 256
ROUTE_TR = 512
DISPATCH_TQ = 256
COMBINE_TQ = 128
EXP_CLAMP = 60.0


def _params(sem, vmem=VMEM_LIMIT):
    return pltpu.CompilerParams(dimension_semantics=sem, vmem_limit_bytes=vmem)


def _dot(a, b):
    return jnp.dot(a, b, preferred_element_type=F32)


def _dot_nt(a, b):
    return lax.dot_general(a, b, (((1,), (1,)), ((), ())), preferred_element_type=F32)


def _sigmoid(x):
    return jax.nn.sigmoid(x)


def _ada_kernel(c_ref, w_ref, b_ref, o_ref):
    c = c_ref[...]
    s = (c * _sigmoid(c)).astype(BF16)
    o_ref[...] = _dot(s, w_ref[...].astype(BF16)) + b_ref[...]


def _ada(c, w, b):
    n = w.shape[1]
    return pl.pallas_call(
        _ada_kernel,
        grid=(n // ADA_TN,),
        in_specs=[
            pl.BlockSpec((BATCH, D_MODEL), lambda j: (0, 0)),
            pl.BlockSpec((D_MODEL, ADA_TN), lambda j: (0, j)),
            pl.BlockSpec((1, ADA_TN), lambda j: (0, j)),
        ],
        out_specs=pl.BlockSpec((BATCH, ADA_TN), lambda j: (0, j)),
        out_shape=jax.ShapeDtypeStruct((BATCH, n), F32),
        compiler_params=_params(("arbitrary",)),
        name="ada",
    )(c, w, b.reshape(1, n))


_U0, _Q0, _K0, _V0, _R0, _GA0, _GB0, _END = 0, 512, 768, 1024, 1536, 2048, 3072, 4096


def _inproj_kernel(x_ref, g_ref, sc_ref, sh_ref, wm_ref, wgk_ref, wgk2_ref, bgk_ref,
                   u_ref, q_ref, k_ref, v_ref, la_ref, rs_ref, ga_ref, gb_ref):
    x = x_ref[0]
    r = lax.rsqrt(jnp.mean(x * x, axis=-1, keepdims=True) + EPS)
    h = (x * r) * g_ref[...] * (1.0 + sc_ref[0]) + sh_ref[0]
    hb = h.astype(BF16)

    def seg(lo, hi):
        return _dot(hb, wm_ref[:, lo:hi])

    u_ref[...] = seg(_U0, _Q0)
    q_ref[0] = seg(_Q0, _K0) * (GLA_DK ** -0.5)
    k_ref[0] = seg(_K0, _V0)
    v_ref[0] = seg(_V0, _R0)
    rr = seg(_R0, _GA0)
    rs_ref[0] = rr * _sigmoid(rr)
    ga_ref[0] = _sigmoid(seg(_GA0, _GB0))
    gb_ref[0] = _sigmoid(seg(_GB0, _END))
    gk = _dot(hb, wgk_ref[...])
    z = _dot(gk.astype(BF16), wgk2_ref[...]) + bgk_ref[...]
    la_ref[0] = -(jnp.maximum(-z, 0.0) + jnp.log1p(jnp.exp(-jnp.abs(z)))) * (1.0 / GLA_GATE_TAU)


def _inproj(x, g, sc, sh, w_main, w_gk, w_gk2, b_gk2):
    tm = INPROJ_TM
    nt = SEQ // tm
    row3 = lambda w: pl.BlockSpec((1, tm, w), lambda b, i: (b, i, 0))
    const = lambda shape: pl.BlockSpec(shape, lambda b, i: tuple(0 for _ in shape))
    mod = pl.BlockSpec((1, 1, D_MODEL), lambda b, i: (b, 0, 0))
    bld = lambda w: jax.ShapeDtypeStruct((BATCH, SEQ, w), F32)
    return pl.pallas_call(
        _inproj_kernel,
        grid=(BATCH, nt),
        in_specs=[row3(D_MODEL), const((1, D_MODEL)), mod, mod,
                  const((D_MODEL, _END)), const((D_MODEL, LANES)), const((LANES, GLA_KEY)),
                  const((1, GLA_KEY))],
        out_specs=[pl.BlockSpec((tm, S5_WIDTH), lambda b, i: (i, b)),
                   row3(GLA_KEY), row3(GLA_KEY), row3(GLA_VAL), row3(GLA_KEY), row3(GLA_VAL),
                   row3(D_MODEL), row3(D_MODEL)],
        out_shape=[jax.ShapeDtypeStruct((SEQ, BATCH * S5_WIDTH), F32),
                   bld(GLA_KEY), bld(GLA_KEY), bld(GLA_VAL), bld(GLA_KEY), bld(GLA_VAL),
                   bld(D_MODEL), bld(D_MODEL)],
        compiler_params=_params(("parallel", "parallel")),
        name="inproj",
    )(x, g, sc, sh, w_main, w_gk, w_gk2, b_gk2)


def _s5_kernel(u_ref, bre_ref, bim_ref, cre_ref, cim_ref, are_ref, aim_ref, d_ref,
               y_ref, s_ref, st_ref):
    half = S5_COLS // 2

    @pl.when(pl.program_id(0) == 0)
    def _():
        st_ref[...] = jnp.zeros_like(st_ref)

    u = u_ref[...]
    ub = u.astype(BF16)
    for j in range(2):
        uj = ub[:, j * 256:(j + 1) * 256]
        s_ref[:, j * half:(j + 1) * half] = _dot(uj, bre_ref[j])
        s_ref[:, S5_COLS + j * half:S5_COLS + (j + 1) * half] = _dot(uj, bim_ref[j])

    for cc in range(S5_COLS // S5_COL_CHUNK):
        re_sl = slice(cc * S5_COL_CHUNK, (cc + 1) * S5_COL_CHUNK)
        im_sl = slice(S5_COLS + cc * S5_COL_CHUNK, S5_COLS + (cc + 1) * S5_COL_CHUNK)
        ar = are_ref[:, re_sl]
        ai = aim_ref[:, re_sl]

        def body(t, carry, re_sl=re_sl, im_sl=im_sl, ar=ar, ai=ai):
            sr, si = carry
            rows = pl.ds(pl.multiple_of(t * BATCH, BATCH), BATCH)
            nr = ar * sr - ai * si + s_ref[rows, re_sl]
            ni = ar * si + ai * sr + s_ref[rows, im_sl]
            s_ref[rows, re_sl] = nr
            s_ref[rows, im_sl] = ni
            return nr, ni

        sr, si = lax.fori_loop(0, S5_TC, body, (st_ref[:, re_sl], st_ref[:, im_sl]), unroll=4)
        st_ref[:, re_sl] = sr
        st_ref[:, im_sl] = si

    ys = []
    for j in range(2):
        sre = s_ref[:, j * half:(j + 1) * half].astype(BF16)
        sim = s_ref[:, S5_COLS + j * half:S5_COLS + (j + 1) * half].astype(BF16)
        ys.append(_dot(sre, cre_ref[j]) + _dot(sim, cim_ref[j]))
    y = jnp.concatenate(ys, axis=1) + d_ref[...] * u
    y_ref[...] = jax.nn.gelu(y)


def _s5(u_tb, bre, bim, cre, cimn, are, aim, dflat):
    rows = S5_TC * BATCH
    const = lambda shape: pl.BlockSpec(shape, lambda i: tuple(0 for _ in shape))
    return pl.pallas_call(
        _s5_kernel,
        grid=(SEQ // S5_TC,),
        in_specs=[pl.BlockSpec((rows, S5_WIDTH), lambda i: (i, 0)),
                  const(bre.shape), const(bim.shape), const(cre.shape), const(cimn.shape),
                  const(are.shape), const(aim.shape), const(dflat.shape)],
        out_specs=pl.BlockSpec((rows, S5_WIDTH), lambda i: (i, 0)),
        out_shape=jax.ShapeDtypeStruct((TOKENS, S5_WIDTH), F32),
        scratch_shapes=[pltpu.VMEM((rows, 2 * S5_COLS), F32),
                        pltpu.VMEM((BATCH, 2 * S5_COLS), F32)],
        compiler_params=_params(("arbitrary",)),
        name="s5",
    )(u_tb, bre, bim, cre, cimn, are, aim, dflat)


def _s5_tables(lam_re, lam_im, log_dt, b_re, b_im, c_re, c_im, d_skip):
    lr = lam_re.astype(F32)
    li = lam_im.astype(F32)
    dt = jnp.exp(log_dt.astype(F32))[:, None]
    mag = jnp.exp(lr * dt)
    abar_re = mag * jnp.cos(li * dt)
    abar_im = mag * jnp.sin(li * dt)
    den = lr * lr + li * li
    num_re = abar_re - 1.0
    coef_re = (num_re * lr + abar_im * li) / den
    coef_im = (abar_im * lr - num_re * li) / den
    cr, ci = coef_re[..., None], coef_im[..., None]
    br, bi = b_re.astype(F32), b_im.astype(F32)
    bbar_re = cr * br - ci * bi
    bbar_im = cr * bi + ci * br
    eye = jnp.eye(S5_GROUPS // 2, dtype=F32)

    def in_map(bb):
        bb = bb.reshape(2, S5_GROUPS // 2, S5_STATE, S5_GROUP)
        return jnp.einsum('jgph,gk->jghkp', bb, eye).reshape(2, 256, S5_COLS // 2).astype(BF16)

    def out_map(cc):
        cc = cc.reshape(2, S5_GROUPS // 2, S5_GROUP, S5_STATE)
        return jnp.einsum('jghp,gk->jgpkh', cc, eye).reshape(2, S5_COLS // 2, 256).astype(BF16)

    are = jnp.broadcast_to(abar_re.reshape(1, S5_COLS), (BATCH, S5_COLS))
    aim = jnp.broadcast_to(abar_im.reshape(1, S5_COLS), (BATCH, S5_COLS))
    return (in_map(bbar_re), in_map(bbar_im), out_map(c_re.astype(F32)), out_map(-c_im.astype(F32)),
            are, aim, d_skip.astype(F32).reshape(1, S5_WIDTH))


def _gla_kernel(q_ref, k_ref, la_ref, v_ref, rs_ref, ng_ref, tt_ref, o_ref, st_ref):
    @pl.when(pl.program_id(1) == 0)
    def _():
        st_ref[...] = jnp.zeros_like(st_ref)

    c = GLA_CHUNK
    tt = tt_ref[...]
    row = lax.broadcasted_iota(I32, (c, GLA_KEY), 0)
    ri = lax.broadcasted_iota(I32, (c, c), 0)
    ci = lax.broadcasted_iota(I32, (c, c), 1)
    causal = ci <= ri
    nsub = c // GLA_SUB

    for ch in range(GLA_TG // c):
        sl = slice(ch * c, (ch + 1) * c)
        g = la_ref[0, sl, :]
        g1 = g.astype(BF16)
        r1 = g - g1.astype(F32)
        g2 = r1.astype(BF16)
        g3 = (r1 - g2.astype(F32)).astype(BF16)
        cs = _dot(tt, g1) + _dot(tt, g2) + _dot(tt, g3)
        b = cs[0:c]
        cl = cs[c:2 * c]
        ref_pt = b - cl
        q = q_ref[0, sl, :]
        k = k_ref[0, sl, :]
        qt = q * jnp.exp(cl)
        qe = q * jnp.exp(b)
        blast = b[c - 1:c, :]
        ks = k * jnp.exp(blast - b)
        k_sub = []
        q_sub = []
        for s in range(nsub):
            rs_ = ref_pt[s * GLA_SUB:s * GLA_SUB + 1, :]
            k_sub.append(k * jnp.exp(jnp.minimum(rs_ - b, EXP_CLAMP)))
            q_sub.append(jnp.where((row >= s * GLA_SUB) & (row < (s + 1) * GLA_SUB), qt, 0.0))
        eblast = jnp.exp(blast)
        for h in range(GLA_HEADS):
            hs = slice(h * GLA_DK, (h + 1) * GLA_DK)
            vs = slice(h * GLA_DV, (h + 1) * GLA_DV)
            qbig = jnp.concatenate([x[:, hs] for x in q_sub], axis=1).astype(BF16)
            kbig = jnp.concatenate([x[:, hs] for x in k_sub], axis=1).astype(BF16)
            sc = jnp.where(causal, _dot_nt(qbig, kbig), 0.0)
            vh = v_ref[0, sl, vs]
            vb = vh.astype(BF16)
            st = st_ref[h]
            o = _dot_nt(qe[:, hs].astype(BF16), st.astype(BF16)) + _dot(sc.astype(BF16), vb)
            st_ref[h] = st * eblast[:, hs] + _dot(vh.T.astype(BF16), ks[:, hs].astype(BF16))
            on = o * lax.rsqrt(jnp.mean(o * o, axis=-1, keepdims=True) + EPS) * ng_ref[...]
            o_ref[0, sl, vs] = on * rs_ref[0, sl, vs]


def _gla(q, k, la, v, rs, norm_g):
    tg = GLA_TG
    c = GLA_CHUNK
    r = jnp.arange(c)
    tri = (r[None, :] <= r[:, None])
    blk = tri & ((r[None, :] // GLA_SUB) == (r[:, None] // GLA_SUB))
    tt = jnp.concatenate([tri, blk], axis=0).astype(BF16)
    row3 = lambda w: pl.BlockSpec((1, tg, w), lambda b, i: (b, i, 0))
    const = lambda shape: pl.BlockSpec(shape, lambda b, i: tuple(0 for _ in shape))
    return pl.pallas_call(
        _gla_kernel,
        grid=(BATCH, SEQ // tg),
        in_specs=[row3(GLA_KEY), row3(GLA_KEY), row3(GLA_KEY), row3(GLA_VAL), row3(GLA_VAL),
                  const((1, GLA_DV)), const((2 * c, c))],
        out_specs=row3(GLA_VAL),
        out_shape=jax.ShapeDtypeStruct((BATCH, SEQ, GLA_VAL), F32),
        scratch_shapes=[pltpu.VMEM((GLA_HEADS, GLA_DV, GLA_DK), F32)],
        compiler_params=_params(("parallel", "arbitrary")),
        name="gla",
    )(q, k, la, v, rs, norm_g.reshape(1, GLA_DV), tt)


def _merge_kernel(ya_ref, yb_ref, ga_ref, gb_ref, x_ref, g1_ref, sc2_ref, sh2_ref, g2_ref, n2_ref,
                  wglu_ref, bglu_ref, wpa_ref, wpb_ref, wout_ref, wr_ref, wsgu_ref, wsd_ref,
                  base_ref, hp_ref, sc_ref):
    ya = ya_ref[...]
    z = _dot(ya.astype(BF16), wglu_ref[...]) + bglu_ref[...]
    ya2 = ya * _sigmoid(z)
    mixed = (ga_ref[0] * _dot(ya2.astype(BF16), wpa_ref[...])
             + gb_ref[0] * _dot(yb_ref[0].astype(BF16), wpb_ref[...]))
    x1 = x_ref[0] + g1_ref[0] * _dot(mixed.astype(BF16), wout_ref[...])
    r = lax.rsqrt(jnp.mean(x1 * x1, axis=-1, keepdims=True) + EPS)
    h = (x1 * r) * n2_ref[...] * (1.0 + sc2_ref[0]) + sh2_ref[0]
    hb = h.astype(BF16)
    sc_ref[...] = _sigmoid(_dot(hb, wr_ref[...]))
    gu = _dot(hb, wsgu_ref[...])
    gate = gu[:, :EXPERT_FF]
    mid = (gate * _sigmoid(gate) * gu[:, EXPERT_FF:]).astype(BF16)
    base_ref[0] = x1 + g2_ref[0] * _dot(mid, wsd_ref[...])
    bits = lax.bitcast_convert_type(hb.astype(F32), jnp.uint32)
    hp_ref[...] = bits[:, :D_MODEL // 2] | (bits[:, D_MODEL // 2:] >> 16)


def _merge(ya_tb, yb, ga, gb, x, g1, sc2, sh2, g2, n2, wglu, bglu, wpa, wpb, wout, wr, wsgu, wsd):
    tm = MERGE_TM
    row3 = lambda w: pl.BlockSpec((1, tm, w), lambda b, i: (b, i, 0))
    const = lambda a: pl.BlockSpec(a.shape, lambda b, i: tuple(0 for _ in a.shape))
    mod = pl.BlockSpec((1, 1, D_MODEL), lambda b, i: (b, 0, 0))
    nt = SEQ // tm
    flat = lambda w: pl.BlockSpec((tm, w), lambda b, i: (b * nt + i, 0))
    return pl.pallas_call(
        _merge_kernel,
        grid=(BATCH, nt),
        in_specs=[pl.BlockSpec((tm, S5_WIDTH), lambda b, i: (i, b)),
                  row3(GLA_VAL), row3(D_MODEL), row3(D_MODEL), row3(D_MODEL),
                  mod, mod, mod, mod, const(n2),
                  const(wglu), const(bglu), const(wpa), const(wpb), const(wout), const(wr),
                  const(wsgu), const(wsd)],
        out_specs=[row3(D_MODEL), flat(D_MODEL // 2), flat(N_EXPERTS)],
        out_shape=[jax.ShapeDtypeStruct((BATCH, SEQ, D_MODEL), F32),
                   jax.ShapeDtypeStruct((TOKENS, D_MODEL // 2), jnp.uint32),
                   jax.ShapeDtypeStruct((TOKENS, N_EXPERTS), F32)],
        compiler_params=_params(("parallel", "parallel")),
        name="merge",
    )(ya_tb, yb, ga, gb, x, g1, sc2, sh2, g2, n2, wglu, bglu, wpa, wpb, wout, wr, wsgu, wsd)


def _first_argmax(x, iota, size):
    m = jnp.max(x, axis=0, keepdims=True)
    idx = jnp.min(jnp.where(x == m, iota, size), axis=0, keepdims=True)
    return m, idx


def _route_kernel(s_ref, bias_ref, tri_ref, e_ref, r_ref, w_ref, cnt_ref, carry_ref):
    @pl.when(pl.program_id(0) == 0)
    def _():
        carry_ref[...] = jnp.zeros_like(carry_ref)

    tr = ROUTE_TR
    neg = -jnp.inf
    s_t = s_ref[...].T
    biased = s_t + bias_ref[...]
    io_g = lax.broadcasted_iota(I32, (GROUP_SIZE, tr), 0)
    rows = []
    for g in range(N_GROUPS):
        xg = biased[g * GROUP_SIZE:(g + 1) * GROUP_SIZE, :]
        m1, i1 = _first_argmax(xg, io_g, GROUP_SIZE)
        m2 = jnp.max(jnp.where(io_g == i1, neg, xg), axis=0, keepdims=True)
        rows.append(m1 + m2)
    gs = jnp.concatenate(rows, axis=0)
    io_n = lax.broadcasted_iota(I32, (N_GROUPS, tr), 0)
    gsel = jnp.zeros((N_GROUPS, tr), F32)
    for _ in range(TOPK_GROUPS):
        _, gi = _first_argmax(gs, io_n, N_GROUPS)
        hit = io_n == gi
        gsel = jnp.where(hit, 1.0, gsel)
        gs = jnp.where(hit, neg, gs)
    masked = jnp.concatenate(
        [jnp.where(gsel[g:g + 1, :] > 0.0, biased[g * GROUP_SIZE:(g + 1) * GROUP_SIZE, :], neg)
         for g in range(N_GROUPS)], axis=0)
    io_e = lax.broadcasted_iota(I32, (N_EXPERTS, tr), 0)
    sel = jnp.zeros((N_EXPERTS, tr), F32)
    idxs = []
    for _ in range(TOP_K):
        _, ei = _first_argmax(masked, io_e, N_EXPERTS)
        hit = io_e == ei
        sel = jnp.where(hit, 1.0, sel)
        masked = jnp.where(hit, neg, masked)
        idxs.append(ei)
    den = jnp.sum(sel * s_t, axis=0, keepdims=True)
    rank = _dot(sel.astype(BF16), tri_ref[...]) + carry_ref[...]
    rks, sks = [], []
    for kk in range(TOP_K):
        hit = io_e == idxs[kk]
        rks.append(jnp.sum(jnp.where(hit, rank, 0.0), axis=0, keepdims=True))
        sks.append(jnp.sum(jnp.where(hit, s_t, 0.0), axis=0, keepdims=True))
    e_ref[...] = jnp.concatenate(idxs, axis=0)
    r_ref[...] = jnp.concatenate(rks, axis=0).astype(I32)
    w_ref[...] = jnp.concatenate(sks, axis=0) / den * ROUTE_SCALE
    carry_ref[...] += jnp.sum(sel, axis=1, keepdims=True)
    cnt_ref[...] = carry_ref[...]


def _route(scores, bias):
    tr = ROUTE_TR
    r = jnp.arange(tr)
    tri = (r[:, None] < r[None, :]).astype(BF16)
    kt = lambda dt: jax.ShapeDtypeStruct((TOP_K, TOKENS), dt)
    blk = pl.BlockSpec((TOP_K, tr), lambda i: (0, i))
    return pl.pallas_call(
        _route_kernel,
        grid=(TOKENS // tr,),
        in_specs=[pl.BlockSpec((tr, N_EXPERTS), lambda i: (i, 0)),
                  pl.BlockSpec((N_EXPERTS, 1), lambda i: (0, 0)),
                  pl.BlockSpec((tr, tr), lambda i: (0, 0))],
        out_specs=[blk, blk, blk, pl.BlockSpec((N_EXPERTS, 1), lambda i: (0, 0))],
        out_shape=[kt(I32), kt(I32), kt(F32), jax.ShapeDtypeStruct((N_EXPERTS, 1), F32)],
        scratch_shapes=[pltpu.VMEM((N_EXPERTS, 1), F32)],
        compiler_params=_params(("arbitrary",)),
        name="route",
    )(scores, bias.reshape(N_EXPERTS, 1), tri)


def _dispatch_kernel(dest_ref, h_ref, init_ref, xs_ref, sem):
    del init_ref
    tq = DISPATCH_TQ

    def row_copy(i, d):
        return pltpu.make_async_copy(h_ref.at[pl.ds(i, 1), :], xs_ref.at[pl.ds(d, 1), :], sem)

    def start(i, carry):
        for kk in range(TOP_K):
            row_copy(i, dest_ref[kk, i]).start()
        return carry

    def wait(i, carry):
        for kk in range(TOP_K):
            row_copy(i, dest_ref[kk, i]).wait()
        return carry

    lax.fori_loop(0, tq, start, 0)
    lax.fori_loop(0, tq, wait, 0)


def _dispatch(dest, hp, xs_init):
    tq = DISPATCH_TQ
    return pl.pallas_call(
        _dispatch_kernel,
        grid=(TOKENS // tq,),
        in_specs=[pl.BlockSpec((TOP_K, tq), lambda i: (0, i), memory_space=pltpu.SMEM),
                  pl.BlockSpec((tq, D_MODEL // 2), lambda i: (i, 0)),
                  pl.BlockSpec(memory_space=pl.ANY)],
        out_specs=pl.BlockSpec(memory_space=pl.ANY),
        out_shape=jax.ShapeDtypeStruct((N_SLOTS, D_MODEL // 2), jnp.uint32),
        scratch_shapes=[pltpu.SemaphoreType.DMA],
        input_output_aliases={2: 0},
        compiler_params=_params(("arbitrary",)),
        name="dispatch",
    )(dest, hp, xs_init)


def _expert_kernel(be_ref, nu_ref, x_ref, wg_ref, wu_ref, wd_ref, y_ref, wgs, wus, wds):
    i = pl.program_id(0)
    e = be_ref[i]
    prev = be_ref[jnp.maximum(i - 1, 0)]
    live = i < nu_ref[0]

    @pl.when(live & ((i == 0) | (e != prev)))
    def _():
        wgs[...] = wg_ref[0].astype(BF16)
        wus[...] = wu_ref[0].astype(BF16)
        wds[...] = wd_ref[0].astype(BF16)

    @pl.when(live)
    def _():
        xp = x_ref[...]
        half = D_MODEL // 2
        hi = lax.bitcast_convert_type(xp & jnp.uint32(0xFFFF0000), F32).astype(BF16)
        lo = lax.bitcast_convert_type(xp << 16, F32).astype(BF16)
        gate = _dot(hi, wgs[:half, :]) + _dot(lo, wgs[half:, :])
        up = _dot(hi, wus[:half, :]) + _dot(lo, wus[half:, :])
        mid = (gate * _sigmoid(gate) * up).astype(BF16)
        y_ref[...] = _dot(mid, wds[...])

    @pl.when(jnp.logical_not(live))
    def _():
        y_ref[...] = jnp.zeros_like(y_ref)


def _experts(block_exp, n_used, xs, wg, wu, wd):
    wspec = lambda shape: pl.BlockSpec((1,) + shape, lambda i, be, nu: (be[i], 0, 0))
    grid_spec = pltpu.PrefetchScalarGridSpec(
        num_scalar_prefetch=2,
        grid=(N_BLOCKS,),
        in_specs=[pl.BlockSpec((MOE_BLOCK, D_MODEL // 2), lambda i, be, nu: (i, 0)),
                  wspec((D_MODEL, EXPERT_FF)), wspec((D_MODEL, EXPERT_FF)),
                  wspec((EXPERT_FF, D_MODEL))],
        out_specs=pl.BlockSpec((MOE_BLOCK, D_MODEL), lambda i, be, nu: (i, 0)),
        scratch_shapes=[pltpu.VMEM((D_MODEL, EXPERT_FF), BF16),
                        pltpu.VMEM((D_MODEL, EXPERT_FF), BF16),
                        pltpu.VMEM((EXPERT_FF, D_MODEL), BF16)],
    )
    return pl.pallas_call(
        _expert_kernel,
        grid_spec=grid_spec,
        out_shape=jax.ShapeDtypeStruct((N_SLOTS, D_MODEL), F32),
        compiler_params=_params(("arbitrary",)),
        name="experts",
    )(block_exp, n_used, xs, wg, wu, wd)


def _combine_kernel(dest_ref, w_ref, base_ref, g2_ref, fg_ref, ys_ref, o_ref, buf, sem):
    tq = COMBINE_TQ

    def row_copy(i, kk, d):
        return pltpu.make_async_copy(ys_ref.at[pl.ds(d, 1), :], buf.at[kk, pl.ds(i, 1), :], sem)

    def start(i, carry):
        for kk in range(TOP_K):
            row_copy(i, kk, dest_ref[kk, i]).start()
        return carry

    def wait(i, carry):
        for kk in range(TOP_K):
            row_copy(i, kk, dest_ref[kk, i]).wait()
        return carry

    lax.fori_loop(0, tq, start, 0)
    lax.fori_loop(0, tq, wait, 0)
    w = w_ref[...]
    acc = buf[0] * w[:, 0:1]
    for kk in range(1, TOP_K):
        acc = acc + buf[kk] * w[:, kk:kk + 1]
    xo = base_ref[0] + g2_ref[0] * acc
    r = lax.rsqrt(jnp.mean(xo * xo, axis=-1, keepdims=True) + EPS)
    o_ref[0] = (xo * r) * fg_ref[...]


def _combine(dest, w_tk, base, g2, final_g, ys):
    tq = COMBINE_TQ
    nt = SEQ // tq
    return pl.pallas_call(
        _combine_kernel,
        grid=(BATCH, nt),
        in_specs=[pl.BlockSpec((TOP_K, tq), lambda b, i: (0, b * nt + i), memory_space=pltpu.SMEM),
                  pl.BlockSpec((tq, TOP_K), lambda b, i: (b * nt + i, 0)),
                  pl.BlockSpec((1, tq, D_MODEL), lambda b, i: (b, i, 0)),
                  pl.BlockSpec((1, 1, D_MODEL), lambda b, i: (b, 0, 0)),
                  pl.BlockSpec((1, D_MODEL), lambda b, i: (0, 0)),
                  pl.BlockSpec(memory_space=pl.ANY)],
        out_specs=pl.BlockSpec((1, tq, D_MODEL), lambda b, i: (b, i, 0)),
        out_shape=jax.ShapeDtypeStruct((BATCH, SEQ, D_MODEL), F32),
        scratch_shapes=[pltpu.VMEM((TOP_K, tq, D_MODEL), F32), pltpu.SemaphoreType.DMA],
        compiler_params=_params(("arbitrary", "arbitrary")),
        name="combine",
    )(dest, w_tk, base, g2, final_g.reshape(1, D_MODEL), ys)


def kernel(x, c, ada_w, ada_b, norm1_g, w_in, s5_lam_re, s5_lam_im, s5_log_dt, s5_b_re, s5_b_im,
           s5_c_re, s5_c_im, s5_d, s5_w_glu, s5_b_glu, w_proj_a, gla_w_gk2, gla_b_gk2, gla_norm_g,
           w_proj_b, w_out, norm2_g, router_w, router_bias, exp_w_gate, exp_w_up, exp_w_down,
           sh_w_gate, sh_w_up, sh_w_down, final_g):
    l = 0
    mod = _ada(c, ada_w[l], ada_b[l])
    sh1, sc1, g1, sh2, sc2, g2 = [m.reshape(BATCH, 1, D_MODEL) for m in jnp.split(mod, 6, axis=-1)]

    w = w_in[l]
    gk0 = _V0 + GLA_VAL
    w_main = jnp.concatenate([w[:, :gk0], w[:, gk0 + GLA_GATE_RANK:]], axis=1).astype(BF16)
    w_gk = jnp.pad(w[:, gk0:gk0 + GLA_GATE_RANK], ((0, 0), (0, LANES - GLA_GATE_RANK))).astype(BF16)
    w_gk2 = jnp.pad(gla_w_gk2[l], ((0, LANES - GLA_GATE_RANK), (0, 0))).astype(BF16)
    u_tb, q, k, v, la, rs, ga, gb = _inproj(
        x, norm1_g[l].reshape(1, D_MODEL), sc1, sh1, w_main, w_gk, w_gk2,
        gla_b_gk2[l].reshape(1, GLA_KEY))
    tables = _s5_tables(s5_lam_re[l], s5_lam_im[l], s5_log_dt[l], s5_b_re[l], s5_b_im[l],
                        s5_c_re[l], s5_c_im[l], s5_d[l])
    ya_tb = _s5(u_tb.reshape(TOKENS, S5_WIDTH), *tables)
    yb = _gla(q, k, la, v, rs, gla_norm_g[l])

    wsgu = jnp.concatenate([sh_w_gate[l], sh_w_up[l]], axis=1).astype(BF16)
    base, hp, scores = _merge(
        ya_tb.reshape(SEQ, BATCH * S5_WIDTH), yb, ga, gb, x, g1, sc2, sh2, g2,
        norm2_g[l].reshape(1, D_MODEL), s5_w_glu[l].astype(BF16), s5_b_glu[l].reshape(1, S5_WIDTH),
        w_proj_a[l].astype(BF16), w_proj_b[l].astype(BF16), w_out[l].astype(BF16),
        router_w[l].astype(BF16), wsgu, sh_w_down[l].astype(BF16))

    e_idx, rank, w_k, cnt = _route(scores, router_bias[l])
    counts = cnt[:, 0].astype(I32)
    padded = (counts + MOE_BLOCK - 1) // MOE_BLOCK * MOE_BLOCK
    pad_end = jnp.cumsum(padded)
    pad_start = pad_end - padded
    dest = pad_start[e_idx] + rank
    block_pos = jnp.arange(N_BLOCKS, dtype=I32) * MOE_BLOCK
    block_exp = jnp.minimum(
        jnp.sum((pad_end[None, :] <= block_pos[:, None]).astype(I32), axis=1), N_EXPERTS - 1)
    n_used = (pad_end[-1:] // MOE_BLOCK).astype(I32)
    xs = _dispatch(dest, hp, jnp.zeros((N_SLOTS, D_MODEL // 2), jnp.uint32))
    ys = _experts(block_exp, n_used, xs, exp_w_gate[l], exp_w_up[l], exp_w_down[l])
    return _combine(dest, w_k.T, base, g2, final_g, ys)
```

```python
import functools

import jax
import jax.numpy as jnp
from jax import lax
from jax.experimental import pallas as pl
from jax.experimental.pallas import tpu as pltpu

F32 = jnp.float32
BF16 = jnp.bfloat16
I32 = jnp.int32

D_MODEL = 1024
BATCH = 8
SEQ = 2048
TOKENS = BATCH * SEQ
S5_WIDTH = 512
S5_GROUP = 16
S5_GROUPS = 32
S5_STATE = 64
S5_COLS = S5_GROUPS * S5_STATE
GLA_HEADS = 4
GLA_DK = 64
GLA_DV = 128
GLA_KEY = GLA_HEADS * GLA_DK
GLA_VAL = GLA_HEADS * GLA_DV
GLA_GATE_RANK = 16
GLA_GATE_TAU = 16.0
GLA_CHUNK = 64
GLA_SUB = 16
N_EXPERTS = 256
TOP_K = 8
N_GROUPS = 8
GROUP_SIZE = N_EXPERTS // N_GROUPS
TOPK_GROUPS = 4
EXPERT_FF = 256
ROUTE_SCALE = 2.5
MOE_BLOCK = 128
EPS = 1e-6
N_SLOTS = -(-(TOKENS * TOP_K + N_EXPERTS * (MOE_BLOCK - 1)) // MOE_BLOCK) * MOE_BLOCK
N_BLOCKS = N_SLOTS // MOE_BLOCK

LANES = 128
VMEM_LIMIT = 56 * 1024 * 1024

ADA_TN = 1536
INPROJ_TM = 512
S5_TC = 64
S5_COL_CHUNK = 512
GLA_TG = 256
MERGE_TM = 256
ROUTE_TR = 512
DISPATCH_TQ = 1024
COMBINE_TQ = 128
EXPERT_NBUF = 4
EXP_CLAMP = 60.0


def _params(sem, vmem=VMEM_LIMIT):
    return pltpu.CompilerParams(dimension_semantics=sem, vmem_limit_bytes=vmem)


def _dot(a, b):
    return jnp.dot(a, b, preferred_element_type=F32)


def _dot_nt(a, b):
    return lax.dot_general(a, b, (((1,), (1,)), ((), ())), preferred_element_type=F32)


def _sigmoid(x):
    return jax.nn.sigmoid(x)


def _ada_kernel(c_ref, w_ref, b_ref, o_ref):
    c = c_ref[...]
    s = (c * _sigmoid(c)).astype(BF16)
    o_ref[...] = _dot(s, w_ref[...].astype(BF16)) + b_ref[...]


def _ada(c, w, b):
    n = w.shape[1]
    return pl.pallas_call(
        _ada_kernel,
        grid=(n // ADA_TN,),
        in_specs=[
            pl.BlockSpec((BATCH, D_MODEL), lambda j: (0, 0)),
            pl.BlockSpec((D_MODEL, ADA_TN), lambda j: (0, j)),
            pl.BlockSpec((1, ADA_TN), lambda j: (0, j)),
        ],
        out_specs=pl.BlockSpec((BATCH, ADA_TN), lambda j: (0, j)),
        out_shape=jax.ShapeDtypeStruct((BATCH, n), F32),
        compiler_params=_params(("arbitrary",)),
        name="ada",
    )(c, w, b.reshape(1, n))


_U0, _Q0, _K0, _V0, _R0, _GA0, _GB0, _END = 0, 512, 768, 1024, 1536, 2048, 3072, 4096


def _inproj_kernel(x_ref, g_ref, sc_ref, sh_ref, wm_ref, wgk_ref, wgk2_ref, bgk_ref,
                   u_ref, q_ref, k_ref, v_ref, la_ref, rs_ref, ga_ref, gb_ref):
    x = x_ref[0]
    r = lax.rsqrt(jnp.mean(x * x, axis=-1, keepdims=True) + EPS)
    h = (x * r) * g_ref[...] * (1.0 + sc_ref[0]) + sh_ref[0]
    hb = h.astype(BF16)

    def seg(lo, hi):
        return _dot(hb, wm_ref[:, lo:hi])

    u_ref[...] = seg(_U0, _Q0)
    q_ref[0] = seg(_Q0, _K0) * (GLA_DK ** -0.5)
    k_ref[0] = seg(_K0, _V0)
    v_ref[0] = seg(_V0, _R0)
    rr = seg(_R0, _GA0)
    rs_ref[0] = rr * _sigmoid(rr)
    ga_ref[0] = _sigmoid(seg(_GA0, _GB0))
    gb_ref[0] = _sigmoid(seg(_GB0, _END))
    gk = _dot(hb, wgk_ref[...])
    z = _dot(gk.astype(BF16), wgk2_ref[...]) + bgk_ref[...]
    la_ref[0] = -(jnp.maximum(-z, 0.0) + jnp.log1p(jnp.exp(-jnp.abs(z)))) * (1.0 / GLA_GATE_TAU)


def _inproj(x, g, sc, sh, w_main, w_gk, w_gk2, b_gk2):
    tm = INPROJ_TM
    nt = SEQ // tm
    row3 = lambda w: pl.BlockSpec((1, tm, w), lambda b, i: (b, i, 0))
    const = lambda shape: pl.BlockSpec(shape, lambda b, i: tuple(0 for _ in shape))
    mod = pl.BlockSpec((1, 1, D_MODEL), lambda b, i: (b, 0, 0))
    bld = lambda w: jax.ShapeDtypeStruct((BATCH, SEQ, w), F32)
    return pl.pallas_call(
        _inproj_kernel,
        grid=(BATCH, nt),
        in_specs=[row3(D_MODEL), const((1, D_MODEL)), mod, mod,
                  const((D_MODEL, _END)), const((D_MODEL, LANES)), const((LANES, GLA_KEY)),
                  const((1, GLA_KEY))],
        out_specs=[pl.BlockSpec((tm, S5_WIDTH), lambda b, i: (i, b)),
                   row3(GLA_KEY), row3(GLA_KEY), row3(GLA_VAL), row3(GLA_KEY), row3(GLA_VAL),
                   row3(D_MODEL), row3(D_MODEL)],
        out_shape=[jax.ShapeDtypeStruct((SEQ, BATCH * S5_WIDTH), F32),
                   bld(GLA_KEY), bld(GLA_KEY), bld(GLA_VAL), bld(GLA_KEY), bld(GLA_VAL),
                   bld(D_MODEL), bld(D_MODEL)],
        compiler_params=_params(("parallel", "parallel")),
        name="inproj",
    )(x, g, sc, sh, w_main, w_gk, w_gk2, b_gk2)


def _s5_kernel(u_ref, bre_ref, bim_ref, cre_ref, cim_ref, are_ref, aim_ref, d_ref,
               y_ref, s_ref, st_ref):
    half = S5_COLS // 2

    @pl.when(pl.program_id(0) == 0)
    def _():
        st_ref[...] = jnp.zeros_like(st_ref)

    u = u_ref[...]
    ub = u.astype(BF16)
    for j in range(2):
        uj = ub[:, j * 256:(j + 1) * 256]
        s_ref[:, j * half:(j + 1) * half] = _dot(uj, bre_ref[j])
        s_ref[:, S5_COLS + j * half:S5_COLS + (j + 1) * half] = _dot(uj, bim_ref[j])

    for cc in range(S5_COLS // S5_COL_CHUNK):
        re_sl = slice(cc * S5_COL_CHUNK, (cc + 1) * S5_COL_CHUNK)
        im_sl = slice(S5_COLS + cc * S5_COL_CHUNK, S5_COLS + (cc + 1) * S5_COL_CHUNK)
        ar = are_ref[:, re_sl]
        ai = aim_ref[:, re_sl]

        def body(t, carry, re_sl=re_sl, im_sl=im_sl, ar=ar, ai=ai):
            sr, si = carry
            rows = pl.ds(pl.multiple_of(t * BATCH, BATCH), BATCH)
            nr = ar * sr - ai * si + s_ref[rows, re_sl]
            ni = ar * si + ai * sr + s_ref[rows, im_sl]
            s_ref[rows, re_sl] = nr
            s_ref[rows, im_sl] = ni
            return nr, ni

        sr, si = lax.fori_loop(0, S5_TC, body, (st_ref[:, re_sl], st_ref[:, im_sl]), unroll=4)
        st_ref[:, re_sl] = sr
        st_ref[:, im_sl] = si

    ys = []
    for j in range(2):
        sre = s_ref[:, j * half:(j + 1) * half].astype(BF16)
        sim = s_ref[:, S5_COLS + j * half:S5_COLS + (j + 1) * half].astype(BF16)
        ys.append(_dot(sre, cre_ref[j]) + _dot(sim, cim_ref[j]))
    y = jnp.concatenate(ys, axis=1) + d_ref[...] * u
    y_ref[...] = jax.nn.gelu(y)


def _s5(u_tb, bre, bim, cre, cimn, are, aim, dflat):
    rows = S5_TC * BATCH
    const = lambda shape: pl.BlockSpec(shape, lambda i: tuple(0 for _ in shape))
    return pl.pallas_call(
        _s5_kernel,
        grid=(SEQ // S5_TC,),
        in_specs=[pl.BlockSpec((rows, S5_WIDTH), lambda i: (i, 0)),
                  const(bre.shape), const(bim.shape), const(cre.shape), const(cimn.shape),
                  const(are.shape), const(aim.shape), const(dflat.shape)],
        out_specs=pl.BlockSpec((rows, S5_WIDTH), lambda i: (i, 0)),
        out_shape=jax.ShapeDtypeStruct((TOKENS, S5_WIDTH), F32),
        scratch_shapes=[pltpu.VMEM((rows, 2 * S5_COLS), F32),
                        pltpu.VMEM((BATCH, 2 * S5_COLS), F32)],
        compiler_params=_params(("arbitrary",)),
        name="s5",
    )(u_tb, bre, bim, cre, cimn, are, aim, dflat)


def _s5_tables(lam_re, lam_im, log_dt, b_re, b_im, c_re, c_im, d_skip):
    lr = lam_re.astype(F32)
    li = lam_im.astype(F32)
    dt = jnp.exp(log_dt.astype(F32))[:, None]
    mag = jnp.exp(lr * dt)
    abar_re = mag * jnp.cos(li * dt)
    abar_im = mag * jnp.sin(li * dt)
    den = lr * lr + li * li
    num_re = abar_re - 1.0
    coef_re = (num_re * lr + abar_im * li) / den
    coef_im = (abar_im * lr - num_re * li) / den
    cr, ci = coef_re[..., None], coef_im[..., None]
    br, bi = b_re.astype(F32), b_im.astype(F32)
    bbar_re = cr * br - ci * bi
    bbar_im = cr * bi + ci * br
    eye = jnp.eye(S5_GROUPS // 2, dtype=F32)

    def in_map(bb):
        bb = bb.reshape(2, S5_GROUPS // 2, S5_STATE, S5_GROUP)
        return jnp.einsum('jgph,gk->jghkp', bb, eye).reshape(2, 256, S5_COLS // 2).astype(BF16)

    def out_map(cc):
        cc = cc.reshape(2, S5_GROUPS // 2, S5_GROUP, S5_STATE)
        return jnp.einsum('jghp,gk->jgpkh', cc, eye).reshape(2, S5_COLS // 2, 256).astype(BF16)

    are = jnp.broadcast_to(abar_re.reshape(1, S5_COLS), (BATCH, S5_COLS))
    aim = jnp.broadcast_to(abar_im.reshape(1, S5_COLS), (BATCH, S5_COLS))
    return (in_map(bbar_re), in_map(bbar_im), out_map(c_re.astype(F32)), out_map(-c_im.astype(F32)),
            are, aim, d_skip.astype(F32).reshape(1, S5_WIDTH))


def _gla_kernel(q_ref, k_ref, la_ref, v_ref, rs_ref, ng_ref, tt_ref, o_ref, st_ref):
    @pl.when(pl.program_id(1) == 0)
    def _():
        st_ref[...] = jnp.zeros_like(st_ref)

    c = GLA_CHUNK
    tt = tt_ref[...]
    row = lax.broadcasted_iota(I32, (c, GLA_KEY), 0)
    ri = lax.broadcasted_iota(I32, (c, c), 0)
    ci = lax.broadcasted_iota(I32, (c, c), 1)
    causal = ci <= ri
    nsub = c // GLA_SUB

    for ch in range(GLA_TG // c):
        sl = slice(ch * c, (ch + 1) * c)
        g = la_ref[0, sl, :]
        g1 = g.astype(BF16)
        r1 = g - g1.astype(F32)
        g2 = r1.astype(BF16)
        g3 = (r1 - g2.astype(F32)).astype(BF16)
        cs = _dot(tt, g1) + _dot(tt, g2) + _dot(tt, g3)
        b = cs[0:c]
        cl = cs[c:2 * c]
        ref_pt = b - cl
        q = q_ref[0, sl, :]
        k = k_ref[0, sl, :]
        qt = q * jnp.exp(cl)
        qe = q * jnp.exp(b)
        blast = b[c - 1:c, :]
        ks = k * jnp.exp(blast - b)
        k_sub = []
        q_sub = []
        for s in range(nsub):
            rs_ = ref_pt[s * GLA_SUB:s * GLA_SUB + 1, :]
            k_sub.append(k * jnp.exp(jnp.minimum(rs_ - b, EXP_CLAMP)))
            q_sub.append(jnp.where((row >= s * GLA_SUB) & (row < (s + 1) * GLA_SUB), qt, 0.0))
        eblast = jnp.exp(blast)
        for h in range(GLA_HEADS):
            hs = slice(h * GLA_DK, (h + 1) * GLA_DK)
            vs = slice(h * GLA_DV, (h + 1) * GLA_DV)
            qbig = jnp.concatenate([x[:, hs] for x in q_sub], axis=1).astype(BF16)
            kbig = jnp.concatenate([x[:, hs] for x in k_sub], axis=1).astype(BF16)
            sc = jnp.where(causal, _dot_nt(qbig, kbig), 0.0)
            vh = v_ref[0, sl, vs]
            vb = vh.astype(BF16)
            st = st_ref[h]
            o = _dot_nt(qe[:, hs].astype(BF16), st.astype(BF16)) + _dot(sc.astype(BF16), vb)
            st_ref[h] = st * eblast[:, hs] + _dot(vh.T.astype(BF16), ks[:, hs].astype(BF16))
            on = o * lax.rsqrt(jnp.mean(o * o, axis=-1, keepdims=True) + EPS) * ng_ref[...]
            o_ref[0, sl, vs] = on * rs_ref[0, sl, vs]


def _gla(q, k, la, v, rs, norm_g):
    tg = GLA_TG
    c = GLA_CHUNK
    r = jnp.arange(c)
    tri = (r[None, :] <= r[:, None])
    blk = tri & ((r[None, :] // GLA_SUB) == (r[:, None] // GLA_SUB))
    tt = jnp.concatenate([tri, blk], axis=0).astype(BF16)
    row3 = lambda w: pl.BlockSpec((1, tg, w), lambda b, i: (b, i, 0))
    const = lambda shape: pl.BlockSpec(shape, lambda b, i: tuple(0 for _ in shape))
    return pl.pallas_call(
        _gla_kernel,
        grid=(BATCH, SEQ // tg),
        in_specs=[row3(GLA_KEY), row3(GLA_KEY), row3(GLA_KEY), row3(GLA_VAL), row3(GLA_VAL),
                  const((1, GLA_DV)), const((2 * c, c))],
        out_specs=row3(GLA_VAL),
        out_shape=jax.ShapeDtypeStruct((BATCH, SEQ, GLA_VAL), F32),
        scratch_shapes=[pltpu.VMEM((GLA_HEADS, GLA_DV, GLA_DK), F32)],
        compiler_params=_params(("parallel", "arbitrary")),
        name="gla",
    )(q, k, la, v, rs, norm_g.reshape(1, GLA_DV), tt)


def _merge_kernel(ya_ref, yb_ref, ga_ref, gb_ref, x_ref, g1_ref, sc2_ref, sh2_ref, g2_ref, n2_ref,
                  wglu_ref, bglu_ref, wpa_ref, wpb_ref, wout_ref, wr_ref, wsgu_ref, wsd_ref,
                  base_ref, hp_ref, sc_ref):
    ya = ya_ref[...]
    z = _dot(ya.astype(BF16), wglu_ref[...]) + bglu_ref[...]
    ya2 = ya * _sigmoid(z)
    mixed = (ga_ref[0] * _dot(ya2.astype(BF16), wpa_ref[...])
             + gb_ref[0] * _dot(yb_ref[0].astype(BF16), wpb_ref[...]))
    x1 = x_ref[0] + g1_ref[0] * _dot(mixed.astype(BF16), wout_ref[...])
    r = lax.rsqrt(jnp.mean(x1 * x1, axis=-1, keepdims=True) + EPS)
    h = (x1 * r) * n2_ref[...] * (1.0 + sc2_ref[0]) + sh2_ref[0]
    hb = h.astype(BF16)
    sc_ref[...] = _sigmoid(_dot(hb, wr_ref[...]))
    gu = _dot(hb, wsgu_ref[...])
    gate = gu[:, :EXPERT_FF]
    mid = (gate * _sigmoid(gate) * gu[:, EXPERT_FF:]).astype(BF16)
    base_ref[0] = x1 + g2_ref[0] * _dot(mid, wsd_ref[...])
    hp_ref[...] = h


def _merge(ya_tb, yb, ga, gb, x, g1, sc2, sh2, g2, n2, wglu, bglu, wpa, wpb, wout, wr, wsgu, wsd):
    tm = MERGE_TM
    row3 = lambda w: pl.BlockSpec((1, tm, w), lambda b, i: (b, i, 0))
    const = lambda a: pl.BlockSpec(a.shape, lambda b, i: tuple(0 for _ in a.shape))
    mod = pl.BlockSpec((1, 1, D_MODEL), lambda b, i: (b, 0, 0))
    nt = SEQ // tm
    flat = lambda w: pl.BlockSpec((tm, w), lambda b, i: (b * nt + i, 0))
    return pl.pallas_call(
        _merge_kernel,
        grid=(BATCH, nt),
        in_specs=[pl.BlockSpec((tm, S5_WIDTH), lambda b, i: (i, b)),
                  row3(GLA_VAL), row3(D_MODEL), row3(D_MODEL), row3(D_MODEL),
                  mod, mod, mod, mod, const(n2),
                  const(wglu), const(bglu), const(wpa), const(wpb), const(wout), const(wr),
                  const(wsgu), const(wsd)],
        out_specs=[row3(D_MODEL), flat(D_MODEL), flat(N_EXPERTS)],
        out_shape=[jax.ShapeDtypeStruct((BATCH, SEQ, D_MODEL), F32),
                   jax.ShapeDtypeStruct((TOKENS, D_MODEL), F32),
                   jax.ShapeDtypeStruct((TOKENS, N_EXPERTS), F32)],
        compiler_params=_params(("parallel", "parallel")),
        name="merge",
    )(ya_tb, yb, ga, gb, x, g1, sc2, sh2, g2, n2, wglu, bglu, wpa, wpb, wout, wr, wsgu, wsd)


def _first_argmax(x, iota, size):
    m = jnp.max(x, axis=0, keepdims=True)
    idx = jnp.min(jnp.where(x == m, iota, size), axis=0, keepdims=True)
    return m, idx


def _route_kernel(s_ref, bias_ref, tri_ref, e_ref, r_ref, w_ref, cnt_ref, carry_ref):
    @pl.when(pl.program_id(0) == 0)
    def _():
        carry_ref[...] = jnp.zeros_like(carry_ref)

    tr = ROUTE_TR
    neg = -jnp.inf
    s_t = s_ref[...].T
    biased = s_t + bias_ref[...]
    io_g = lax.broadcasted_iota(I32, (GROUP_SIZE, tr), 0)
    rows = []
    for g in range(N_GROUPS):
        xg = biased[g * GROUP_SIZE:(g + 1) * GROUP_SIZE, :]
        m1, i1 = _first_argmax(xg, io_g, GROUP_SIZE)
        m2 = jnp.max(jnp.where(io_g == i1, neg, xg), axis=0, keepdims=True)
        rows.append(m1 + m2)
    gs = jnp.concatenate(rows, axis=0)
    io_n = lax.broadcasted_iota(I32, (N_GROUPS, tr), 0)
    gsel = jnp.zeros((N_GROUPS, tr), F32)
    for _ in range(TOPK_GROUPS):
        _, gi = _first_argmax(gs, io_n, N_GROUPS)
        hit = io_n == gi
        gsel = jnp.where(hit, 1.0, gsel)
        gs = jnp.where(hit, neg, gs)
    masked = jnp.concatenate(
        [jnp.where(gsel[g:g + 1, :] > 0.0, biased[g * GROUP_SIZE:(g + 1) * GROUP_SIZE, :], neg)
         for g in range(N_GROUPS)], axis=0)
    io_e = lax.broadcasted_iota(I32, (N_EXPERTS, tr), 0)
    sel = jnp.zeros((N_EXPERTS, tr), F32)
    idxs = []
    for _ in range(TOP_K):
        _, ei = _first_argmax(masked, io_e, N_EXPERTS)
        hit = io_e == ei
        sel = jnp.where(hit, 1.0, sel)
        masked = jnp.where(hit, neg, masked)
        idxs.append(ei)
    den = jnp.sum(sel * s_t, axis=0, keepdims=True)
    rank = _dot(sel.astype(BF16), tri_ref[...]) + carry_ref[...]
    rks, sks = [], []
    for kk in range(TOP_K):
        hit = io_e == idxs[kk]
        rks.append(jnp.sum(jnp.where(hit, rank, 0.0), axis=0, keepdims=True))
        sks.append(jnp.sum(jnp.where(hit, s_t, 0.0), axis=0, keepdims=True))
    e_ref[...] = jnp.concatenate(idxs, axis=0)
    r_ref[...] = jnp.concatenate(rks, axis=0).astype(I32)
    w_ref[...] = jnp.concatenate(sks, axis=0) / den * ROUTE_SCALE
    carry_ref[...] += jnp.sum(sel, axis=1, keepdims=True)
    cnt_ref[...] = carry_ref[...]


def _route(scores, bias):
    tr = ROUTE_TR
    r = jnp.arange(tr)
    tri = (r[:, None] < r[None, :]).astype(BF16)
    kt = lambda dt: jax.ShapeDtypeStruct((TOP_K, TOKENS), dt)
    blk = pl.BlockSpec((TOP_K, tr), lambda i: (0, i))
    return pl.pallas_call(
        _route_kernel,
        grid=(TOKENS // tr,),
        in_specs=[pl.BlockSpec((tr, N_EXPERTS), lambda i: (i, 0)),
                  pl.BlockSpec((N_EXPERTS, 1), lambda i: (0, 0)),
                  pl.BlockSpec((tr, tr), lambda i: (0, 0))],
        out_specs=[blk, blk, blk, pl.BlockSpec((N_EXPERTS, 1), lambda i: (0, 0))],
        out_shape=[kt(I32), kt(I32), kt(F32), jax.ShapeDtypeStruct((N_EXPERTS, 1), F32)],
        scratch_shapes=[pltpu.VMEM((N_EXPERTS, 1), F32)],
        compiler_params=_params(("arbitrary",)),
        name="route",
    )(scores, bias.reshape(N_EXPERTS, 1), tri)


def _slots_kernel(e_ref, r_ref, ps_ref, d_ref):
    tr = ROUTE_TR
    io_e = lax.broadcasted_iota(I32, (N_EXPERTS, tr), 0)
    ps = ps_ref[...]
    rows = []
    for kk in range(TOP_K):
        hit = io_e == e_ref[kk:kk + 1, :]
        rows.append(jnp.sum(jnp.where(hit, ps, 0.0), axis=0, keepdims=True))
    d_ref[...] = jnp.concatenate(rows, axis=0).astype(I32) + r_ref[...]


def _slots(e_idx, rank, pad_start):
    tr = ROUTE_TR
    blk = pl.BlockSpec((TOP_K, tr), lambda i: (0, i))
    return pl.pallas_call(
        _slots_kernel,
        grid=(TOKENS // tr,),
        in_specs=[blk, blk, pl.BlockSpec((N_EXPERTS, 1), lambda i: (0, 0))],
        out_specs=blk,
        out_shape=jax.ShapeDtypeStruct((TOP_K, TOKENS), I32),
        compiler_params=_params(("parallel",)),
        name="slots",
    )(e_idx, rank, pad_start.astype(F32).reshape(N_EXPERTS, 1))


def _dispatch_kernel(ps_ref, cnt_ref, nb_ref, dest_ref, h_ref, xs_ref, zbuf, sem, zsem):
    tq = DISPATCH_TQ
    step = pl.program_id(0)
    n_used = ps_ref[N_EXPERTS - 1] // MOE_BLOCK + nb_ref[N_EXPERTS - 1]

    def zero_fill(act):
        def block(j, carry):
            r = pl.ds(pl.multiple_of(j * MOE_BLOCK, MOE_BLOCK), MOE_BLOCK)
            act(pltpu.make_async_copy(zbuf, xs_ref.at[r, :], zsem))
            return carry

        def expert(e, carry):
            @pl.when(cnt_ref[e] < nb_ref[e] * MOE_BLOCK)
            def _():
                block(ps_ref[e] // MOE_BLOCK + nb_ref[e] - 1, 0)
            return carry

        lax.fori_loop(0, N_EXPERTS, expert, 0)
        lax.fori_loop(n_used, N_BLOCKS, block, 0)

    @pl.when(step == 0)
    def _():
        zbuf[...] = jnp.zeros_like(zbuf)
        zero_fill(lambda cp: cp.start())
        zero_fill(lambda cp: cp.wait())

    def start(i, carry):
        for kk in range(TOP_K):
            d = dest_ref[kk, i]
            pltpu.make_async_copy(h_ref.at[pl.ds(i, 1), :], xs_ref.at[pl.ds(d, 1), :],
                                  sem).start(priority=kk % 2)
        return carry

    lax.fori_loop(0, tq, start, 0, unroll=2)
    for _ in range(TOP_K):
        pltpu.make_async_copy(h_ref, xs_ref.at[pl.ds(0, tq), :], sem).wait()


def _dispatch(pad_start, counts, n_blk, dest, hp):
    tq = DISPATCH_TQ
    grid_spec = pltpu.PrefetchScalarGridSpec(
        num_scalar_prefetch=3,
        grid=(TOKENS // tq,),
        in_specs=[pl.BlockSpec((TOP_K, tq), lambda i, *_: (0, i), memory_space=pltpu.SMEM),
                  pl.BlockSpec((tq, D_MODEL), lambda i, *_: (i, 0))],
        out_specs=pl.BlockSpec(memory_space=pl.ANY),
        scratch_shapes=[pltpu.VMEM((MOE_BLOCK, D_MODEL), F32),
                        pltpu.SemaphoreType.DMA, pltpu.SemaphoreType.DMA],
    )
    return pl.pallas_call(
        _dispatch_kernel,
        grid_spec=grid_spec,
        out_shape=jax.ShapeDtypeStruct((N_SLOTS, D_MODEL), F32),
        compiler_params=_params(("arbitrary",)),
        name="dispatch",
    )(pad_start, counts, n_blk, dest, hp)


def _expert_kernel(ps_ref, nb_ref, wg_ref, wu_ref, wd_ref, xs_ref, ys_ref,
                   xbuf, ybuf, wgus, wds, xsem, ysem):
    e = pl.program_id(0)
    nb = nb_ref[e]
    first = ps_ref[e] // MOE_BLOCK
    n_used = ps_ref[N_EXPERTS - 1] // MOE_BLOCK + nb_ref[N_EXPERTS - 1]
    nbuf = EXPERT_NBUF

    def rows(g):
        return pl.ds(pl.multiple_of(g * MOE_BLOCK, MOE_BLOCK), MOE_BLOCK)

    def x_copy(g):
        slot = g % nbuf
        return pltpu.make_async_copy(xs_ref.at[rows(g), :], xbuf.at[slot], xsem.at[slot])

    def y_copy(g):
        slot = g % nbuf
        return pltpu.make_async_copy(ybuf.at[slot], ys_ref.at[rows(g), :], ysem.at[slot])

    @pl.when(e == 0)
    def _():
        for g in range(nbuf - 1):
            @pl.when(g < n_used)
            def _():
                x_copy(g).start()

    @pl.when(nb > 0)
    def _():
        wgus[:, :EXPERT_FF] = wg_ref[0].astype(BF16)
        wgus[:, EXPERT_FF:] = wu_ref[0].astype(BF16)
        wds[...] = wd_ref[0].astype(BF16)

        def body(j, carry):
            g = first + j
            slot = g % nbuf
            x_copy(g).wait()

            @pl.when(g + nbuf - 1 < n_used)
            def _():
                x_copy(g + nbuf - 1).start()

            @pl.when(g >= nbuf)
            def _():
                y_copy(g - nbuf).wait()

            gu = _dot(xbuf[slot].astype(BF16), wgus[...])
            gate = gu[:, :EXPERT_FF]
            mid = (gate * _sigmoid(gate) * gu[:, EXPERT_FF:]).astype(BF16)
            ybuf[slot] = _dot(mid, wds[...])
            y_copy(g).start()
            return carry

        lax.fori_loop(0, nb, body, 0)

    @pl.when(e == N_EXPERTS - 1)
    def _():
        for d in range(nbuf):
            g = n_used - nbuf + d

            @pl.when(g >= 0)
            def _():
                y_copy(g).wait()

        ybuf[0] = jnp.zeros((MOE_BLOCK, D_MODEL), F32)

        def z_copy(j):
            return pltpu.make_async_copy(ybuf.at[0], ys_ref.at[rows(j), :], ysem.at[0])

        def z_start(j, carry):
            z_copy(j).start()
            return carry

        def z_wait(j, carry):
            z_copy(j).wait()
            return carry

        lax.fori_loop(n_used, N_BLOCKS, z_start, 0)
        lax.fori_loop(n_used, N_BLOCKS, z_wait, 0)


def _experts(pad_start, n_blk, xs, wg, wu, wd):
    wspec = lambda shape: pl.BlockSpec((1,) + shape, lambda e, ps, nb: (e, 0, 0))
    grid_spec = pltpu.PrefetchScalarGridSpec(
        num_scalar_prefetch=2,
        grid=(N_EXPERTS,),
        in_specs=[wspec((D_MODEL, EXPERT_FF)), wspec((D_MODEL, EXPERT_FF)),
                  wspec((EXPERT_FF, D_MODEL)), pl.BlockSpec(memory_space=pl.ANY)],
        out_specs=pl.BlockSpec(memory_space=pl.ANY),
        scratch_shapes=[pltpu.VMEM((EXPERT_NBUF, MOE_BLOCK, D_MODEL), F32),
                        pltpu.VMEM((EXPERT_NBUF, MOE_BLOCK, D_MODEL), F32),
                        pltpu.VMEM((D_MODEL, 2 * EXPERT_FF), BF16),
                        pltpu.VMEM((EXPERT_FF, D_MODEL), BF16),
                        pltpu.SemaphoreType.DMA((EXPERT_NBUF,)),
                        pltpu.SemaphoreType.DMA((EXPERT_NBUF,))],
    )
    return pl.pallas_call(
        _expert_kernel,
        grid_spec=grid_spec,
        out_shape=jax.ShapeDtypeStruct((N_SLOTS, D_MODEL), F32),
        compiler_params=_params(("arbitrary",)),
        name="experts",
    )(pad_start, n_blk, wg, wu, wd, xs)


def _combine_kernel(dcur_ref, dnext_ref, w_ref, base_ref, g2_ref, fg_ref, ys_ref, o_ref, buf, sem):
    tq = COMBINE_TQ
    i = pl.program_id(0)
    n = pl.num_programs(0)

    def issue(dest_ref, slot):
        def start(t, carry):
            for kk in range(TOP_K):
                d = dest_ref[kk, t]
                pltpu.make_async_copy(ys_ref.at[pl.ds(d, 1), :], buf.at[slot, kk, pl.ds(t, 1), :],
                                      sem.at[slot]).start(priority=kk % 2)
            return carry
        lax.fori_loop(0, tq, start, 0, unroll=2)

    @pl.when(i == 0)
    def _():
        issue(dcur_ref, 0)

    @pl.when(i + 1 < n)
    def _():
        issue(dnext_ref, (i + 1) % 2)

    slot = i % 2
    for kk in range(TOP_K):
        pltpu.make_async_copy(ys_ref.at[pl.ds(0, tq), :], buf.at[slot, kk], sem.at[slot]).wait()
    w = w_ref[...]
    acc = buf[slot, 0] * w[:, 0:1]
    for kk in range(1, TOP_K):
        acc = acc + buf[slot, kk] * w[:, kk:kk + 1]
    xo = base_ref[0] + g2_ref[0] * acc
    r = lax.rsqrt(jnp.mean(xo * xo, axis=-1, keepdims=True) + EPS)
    o_ref[0] = (xo * r) * fg_ref[...]


def _combine(dest, w_tk, base, g2, final_g, ys):
    tq = COMBINE_TQ
    nt = SEQ // tq
    n = TOKENS // tq
    return pl.pallas_call(
        _combine_kernel,
        grid=(n,),
        in_specs=[pl.BlockSpec((TOP_K, tq), lambda i: (0, i), memory_space=pltpu.SMEM),
                  pl.BlockSpec((TOP_K, tq), lambda i: (0, jnp.minimum(i + 1, n - 1)),
                               memory_space=pltpu.SMEM),
                  pl.BlockSpec((tq, TOP_K), lambda i: (i, 0)),
                  pl.BlockSpec((1, tq, D_MODEL), lambda i: (i // nt, i % nt, 0)),
                  pl.BlockSpec((1, 1, D_MODEL), lambda i: (i // nt, 0, 0)),
                  pl.BlockSpec((1, D_MODEL), lambda i: (0, 0)),
                  pl.BlockSpec(memory_space=pl.ANY)],
        out_specs=pl.BlockSpec((1, tq, D_MODEL), lambda i: (i // nt, i % nt, 0)),
        out_shape=jax.ShapeDtypeStruct((BATCH, SEQ, D_MODEL), F32),
        scratch_shapes=[pltpu.VMEM((2, TOP_K, tq, D_MODEL), F32), pltpu.SemaphoreType.DMA((2,))],
        compiler_params=_params(("arbitrary",)),
        name="combine",
    )(dest, dest, w_tk, base, g2, final_g.reshape(1, D_MODEL), ys)


def kernel(x, c, ada_w, ada_b, norm1_g, w_in, s5_lam_re, s5_lam_im, s5_log_dt, s5_b_re, s5_b_im,
           s5_c_re, s5_c_im, s5_d, s5_w_glu, s5_b_glu, w_proj_a, gla_w_gk2, gla_b_gk2, gla_norm_g,
           w_proj_b, w_out, norm2_g, router_w, router_bias, exp_w_gate, exp_w_up, exp_w_down,
           sh_w_gate, sh_w_up, sh_w_down, final_g):
    l = 0
    mod = _ada(c, ada_w[l], ada_b[l])
    sh1, sc1, g1, sh2, sc2, g2 = [m.reshape(BATCH, 1, D_MODEL) for m in jnp.split(mod, 6, axis=-1)]

    w = w_in[l]
    gk0 = _V0 + GLA_VAL
    w_main = jnp.concatenate([w[:, :gk0], w[:, gk0 + GLA_GATE_RANK:]], axis=1).astype(BF16)
    w_gk = jnp.pad(w[:, gk0:gk0 + GLA_GATE_RANK], ((0, 0), (0, LANES - GLA_GATE_RANK))).astype(BF16)
    w_gk2 = jnp.pad(gla_w_gk2[l], ((0, LANES - GLA_GATE_RANK), (0, 0))).astype(BF16)
    u_tb, q, k, v, la, rs, ga, gb = _inproj(
        x, norm1_g[l].reshape(1, D_MODEL), sc1, sh1, w_main, w_gk, w_gk2,
        gla_b_gk2[l].reshape(1, GLA_KEY))
    tables = _s5_tables(s5_lam_re[l], s5_lam_im[l], s5_log_dt[l], s5_b_re[l], s5_b_im[l],
                        s5_c_re[l], s5_c_im[l], s5_d[l])
    ya_tb = _s5(u_tb.reshape(TOKENS, S5_WIDTH), *tables)
    yb = _gla(q, k, la, v, rs, gla_norm_g[l])

    wsgu = jnp.concatenate([sh_w_gate[l], sh_w_up[l]], axis=1).astype(BF16)
    base, hp, scores = _merge(
        ya_tb.reshape(SEQ, BATCH * S5_WIDTH), yb, ga, gb, x, g1, sc2, sh2, g2,
        norm2_g[l].reshape(1, D_MODEL), s5_w_glu[l].astype(BF16), s5_b_glu[l].reshape(1, S5_WIDTH),
        w_proj_a[l].astype(BF16), w_proj_b[l].astype(BF16), w_out[l].astype(BF16),
        router_w[l].astype(BF16), wsgu, sh_w_down[l].astype(BF16))

    e_idx, rank, w_k, cnt = _route(scores, router_bias[l])
    counts = cnt[:, 0].astype(I32)
    padded = (counts + MOE_BLOCK - 1) // MOE_BLOCK * MOE_BLOCK
    pad_end = jnp.cumsum(padded)
    pad_start = pad_end - padded
    dest = _slots(e_idx, rank, pad_start)
    pad_start = pad_start.astype(I32)
    n_blk = (padded // MOE_BLOCK).astype(I32)
    xs = _dispatch(pad_start, counts, n_blk, dest, hp)
    ys = _experts(pad_start, n_blk, xs, exp_w_gate[l], exp_w_up[l], exp_w_down[l])
    return _combine(dest, w_k.T, base, g2, final_g, ys)
```

```python
import functools

import jax
import jax.numpy as jnp
from jax import lax
from jax.experimental import pallas as pl
from jax.experimental.pallas import tpu as pltpu

F32 = jnp.float32
BF16 = jnp.bfloat16
I32 = jnp.int32

D_MODEL = 1024
BATCH = 8
SEQ = 2048
TOKENS = BATCH * SEQ
S5_WIDTH = 512
S5_GROUP = 16
S5_GROUPS = 32
S5_STATE = 64
S5_COLS = S5_GROUPS * S5_STATE
GLA_HEADS = 4
GLA_DK = 64
GLA_DV = 128
GLA_KEY = GLA_HEADS * GLA_DK
GLA_VAL = GLA_HEADS * GLA_DV
GLA_GATE_RANK = 16
GLA_GATE_TAU = 16.0
GLA_CHUNK = 64
GLA_SUB = 16
N_EXPERTS = 256
TOP_K = 8
N_GROUPS = 8
GROUP_SIZE = N_EXPERTS // N_GROUPS
TOPK_GROUPS = 4
EXPERT_FF = 256
ROUTE_SCALE = 2.5
MOE_BLOCK = 128
EPS = 1e-6
N_SLOTS = -(-(TOKENS * TOP_K + N_EXPERTS * (MOE_BLOCK - 1)) // MOE_BLOCK) * MOE_BLOCK
N_BLOCKS = N_SLOTS // MOE_BLOCK

LANES = 128
SUBLANES = 8
TOKEN_TILE = (SUBLANES, LANES)
assert D_MODEL == SUBLANES * LANES
VMEM_LIMIT = 56 * 1024 * 1024

ADA_TN = 1536
INPROJ_TM = 512
S5_TC = 64
S5_COL_CHUNK = 512
GLA_TG = 256
MERGE_TM = 256
ROUTE_TR = 512
DISPATCH_TQ = 1024
COMBINE_TQ = 128
EXPERT_NBUF = 4
EXP_CLAMP = 60.0


def _params(sem, vmem=VMEM_LIMIT):
    return pltpu.CompilerParams(dimension_semantics=sem, vmem_limit_bytes=vmem)


def _dot(a, b):
    return jnp.dot(a, b, preferred_element_type=F32)


def _dot_nt(a, b):
    return lax.dot_general(a, b, (((1,), (1,)), ((), ())), preferred_element_type=F32)


def _sigmoid(x):
    return jax.nn.sigmoid(x)


def _ada_kernel(c_ref, w_ref, b_ref, o_ref):
    c = c_ref[...]
    s = (c * _sigmoid(c)).astype(BF16)
    o_ref[...] = _dot(s, w_ref[...].astype(BF16)) + b_ref[...]


def _ada(c, w, b):
    n = w.shape[1]
    return pl.pallas_call(
        _ada_kernel,
        grid=(n // ADA_TN,),
        in_specs=[
            pl.BlockSpec((BATCH, D_MODEL), lambda j: (0, 0)),
            pl.BlockSpec((D_MODEL, ADA_TN), lambda j: (0, j)),
            pl.BlockSpec((1, ADA_TN), lambda j: (0, j)),
        ],
        out_specs=pl.BlockSpec((BATCH, ADA_TN), lambda j: (0, j)),
        out_shape=jax.ShapeDtypeStruct((BATCH, n), F32),
        compiler_params=_params(("arbitrary",)),
        name="ada",
    )(c, w, b.reshape(1, n))


_U0, _Q0, _K0, _V0, _R0, _GA0, _GB0, _END = 0, 512, 768, 1024, 1536, 2048, 3072, 4096


def _inproj_kernel(x_ref, g_ref, sc_ref, sh_ref, wm_ref, wgk_ref, wgk2_ref, bgk_ref,
                   u_ref, q_ref, k_ref, v_ref, la_ref, rs_ref, ga_ref, gb_ref):
    x = x_ref[0]
    r = lax.rsqrt(jnp.mean(x * x, axis=-1, keepdims=True) + EPS)
    h = (x * r) * g_ref[...] * (1.0 + sc_ref[0]) + sh_ref[0]
    hb = h.astype(BF16)

    def seg(lo, hi):
        return _dot(hb, wm_ref[:, lo:hi])

    u_ref[...] = seg(_U0, _Q0)
    q_ref[0] = seg(_Q0, _K0) * (GLA_DK ** -0.5)
    k_ref[0] = seg(_K0, _V0)
    v_ref[0] = seg(_V0, _R0)
    rr = seg(_R0, _GA0)
    rs_ref[0] = rr * _sigmoid(rr)
    ga_ref[0] = _sigmoid(seg(_GA0, _GB0))
    gb_ref[0] = _sigmoid(seg(_GB0, _END))
    gk = _dot(hb, wgk_ref[...])
    z = _dot(gk.astype(BF16), wgk2_ref[...]) + bgk_ref[...]
    la_ref[0] = -(jnp.maximum(-z, 0.0) + jnp.log1p(jnp.exp(-jnp.abs(z)))) * (1.0 / GLA_GATE_TAU)


def _inproj(x, g, sc, sh, w_main, w_gk, w_gk2, b_gk2):
    tm = INPROJ_TM
    nt = SEQ // tm
    row3 = lambda w: pl.BlockSpec((1, tm, w), lambda b, i: (b, i, 0))
    const = lambda shape: pl.BlockSpec(shape, lambda b, i: tuple(0 for _ in shape))
    mod = pl.BlockSpec((1, 1, D_MODEL), lambda b, i: (b, 0, 0))
    bld = lambda w: jax.ShapeDtypeStruct((BATCH, SEQ, w), F32)
    return pl.pallas_call(
        _inproj_kernel,
        grid=(BATCH, nt),
        in_specs=[row3(D_MODEL), const((1, D_MODEL)), mod, mod,
                  const((D_MODEL, _END)), const((D_MODEL, LANES)), const((LANES, GLA_KEY)),
                  const((1, GLA_KEY))],
        out_specs=[pl.BlockSpec((tm, S5_WIDTH), lambda b, i: (i, b)),
                   row3(GLA_KEY), row3(GLA_KEY), row3(GLA_VAL), row3(GLA_KEY), row3(GLA_VAL),
                   row3(D_MODEL), row3(D_MODEL)],
        out_shape=[jax.ShapeDtypeStruct((SEQ, BATCH * S5_WIDTH), F32),
                   bld(GLA_KEY), bld(GLA_KEY), bld(GLA_VAL), bld(GLA_KEY), bld(GLA_VAL),
                   bld(D_MODEL), bld(D_MODEL)],
        compiler_params=_params(("parallel", "parallel")),
        name="inproj",
    )(x, g, sc, sh, w_main, w_gk, w_gk2, b_gk2)


def _s5_kernel(u_ref, bre_ref, bim_ref, cre_ref, cim_ref, are_ref, aim_ref, d_ref,
               y_ref, s_ref, st_ref):
    half = S5_COLS // 2

    @pl.when(pl.program_id(0) == 0)
    def _():
        st_ref[...] = jnp.zeros_like(st_ref)

    u = u_ref[...]
    ub = u.astype(BF16)
    for j in range(2):
        uj = ub[:, j * 256:(j + 1) * 256]
        s_ref[:, j * half:(j + 1) * half] = _dot(uj, bre_ref[j])
        s_ref[:, S5_COLS + j * half:S5_COLS + (j + 1) * half] = _dot(uj, bim_ref[j])

    for cc in range(S5_COLS // S5_COL_CHUNK):
        re_sl = slice(cc * S5_COL_CHUNK, (cc + 1) * S5_COL_CHUNK)
        im_sl = slice(S5_COLS + cc * S5_COL_CHUNK, S5_COLS + (cc + 1) * S5_COL_CHUNK)
        ar = are_ref[:, re_sl]
        ai = aim_ref[:, re_sl]

        def body(t, carry, re_sl=re_sl, im_sl=im_sl, ar=ar, ai=ai):
            sr, si = carry
            rows = pl.ds(pl.multiple_of(t * BATCH, BATCH), BATCH)
            nr = ar * sr - ai * si + s_ref[rows, re_sl]
            ni = ar * si + ai * sr + s_ref[rows, im_sl]
            s_ref[rows, re_sl] = nr
            s_ref[rows, im_sl] = ni
            return nr, ni

        sr, si = lax.fori_loop(0, S5_TC, body, (st_ref[:, re_sl], st_ref[:, im_sl]), unroll=4)
        st_ref[:, re_sl] = sr
        st_ref[:, im_sl] = si

    ys = []
    for j in range(2):
        sre = s_ref[:, j * half:(j + 1) * half].astype(BF16)
        sim = s_ref[:, S5_COLS + j * half:S5_COLS + (j + 1) * half].astype(BF16)
        ys.append(_dot(sre, cre_ref[j]) + _dot(sim, cim_ref[j]))
    y = jnp.concatenate(ys, axis=1) + d_ref[...] * u
    y_ref[...] = jax.nn.gelu(y)


def _s5(u_tb, bre, bim, cre, cimn, are, aim, dflat):
    rows = S5_TC * BATCH
    const = lambda shape: pl.BlockSpec(shape, lambda i: tuple(0 for _ in shape))
    return pl.pallas_call(
        _s5_kernel,
        grid=(SEQ // S5_TC,),
        in_specs=[pl.BlockSpec((rows, S5_WIDTH), lambda i: (i, 0)),
                  const(bre.shape), const(bim.shape), const(cre.shape), const(cimn.shape),
                  const(are.shape), const(aim.shape), const(dflat.shape)],
        out_specs=pl.BlockSpec((rows, S5_WIDTH), lambda i: (i, 0)),
        out_shape=jax.ShapeDtypeStruct((TOKENS, S5_WIDTH), F32),
        scratch_shapes=[pltpu.VMEM((rows, 2 * S5_COLS), F32),
                        pltpu.VMEM((BATCH, 2 * S5_COLS), F32)],
        compiler_params=_params(("arbitrary",)),
        name="s5",
    )(u_tb, bre, bim, cre, cimn, are, aim, dflat)


def _s5_tables(lam_re, lam_im, log_dt, b_re, b_im, c_re, c_im, d_skip):
    lr = lam_re.astype(F32)
    li = lam_im.astype(F32)
    dt = jnp.exp(log_dt.astype(F32))[:, None]
    mag = jnp.exp(lr * dt)
    abar_re = mag * jnp.cos(li * dt)
    abar_im = mag * jnp.sin(li * dt)
    den = lr * lr + li * li
    num_re = abar_re - 1.0
    coef_re = (num_re * lr + abar_im * li) / den
    coef_im = (abar_im * lr - num_re * li) / den
    cr, ci = coef_re[..., None], coef_im[..., None]
    br, bi = b_re.astype(F32), b_im.astype(F32)
    bbar_re = cr * br - ci * bi
    bbar_im = cr * bi + ci * br
    eye = jnp.eye(S5_GROUPS // 2, dtype=F32)

    def in_map(bb):
        bb = bb.reshape(2, S5_GROUPS // 2, S5_STATE, S5_GROUP)
        return jnp.einsum('jgph,gk->jghkp', bb, eye).reshape(2, 256, S5_COLS // 2).astype(BF16)

    def out_map(cc):
        cc = cc.reshape(2, S5_GROUPS // 2, S5_GROUP, S5_STATE)
        return jnp.einsum('jghp,gk->jgpkh', cc, eye).reshape(2, S5_COLS // 2, 256).astype(BF16)

    are = jnp.broadcast_to(abar_re.reshape(1, S5_COLS), (BATCH, S5_COLS))
    aim = jnp.broadcast_to(abar_im.reshape(1, S5_COLS), (BATCH, S5_COLS))
    return (in_map(bbar_re), in_map(bbar_im), out_map(c_re.astype(F32)), out_map(-c_im.astype(F32)),
            are, aim, d_skip.astype(F32).reshape(1, S5_WIDTH))


def _gla_kernel(q_ref, k_ref, la_ref, v_ref, rs_ref, ng_ref, tt_ref, o_ref, st_ref):
    @pl.when(pl.program_id(1) == 0)
    def _():
        st_ref[...] = jnp.zeros_like(st_ref)

    c = GLA_CHUNK
    tt = tt_ref[...]
    row = lax.broadcasted_iota(I32, (c, GLA_KEY), 0)
    ri = lax.broadcasted_iota(I32, (c, c), 0)
    ci = lax.broadcasted_iota(I32, (c, c), 1)
    causal = ci <= ri
    nsub = c // GLA_SUB

    for ch in range(GLA_TG // c):
        sl = slice(ch * c, (ch + 1) * c)
        g = la_ref[0, sl, :]
        g1 = g.astype(BF16)
        r1 = g - g1.astype(F32)
        g2 = r1.astype(BF16)
        g3 = (r1 - g2.astype(F32)).astype(BF16)
        cs = _dot(tt, g1) + _dot(tt, g2) + _dot(tt, g3)
        b = cs[0:c]
        cl = cs[c:2 * c]
        ref_pt = b - cl
        q = q_ref[0, sl, :]
        k = k_ref[0, sl, :]
        qt = q * jnp.exp(cl)
        qe = q * jnp.exp(b)
        blast = b[c - 1:c, :]
        ks = k * jnp.exp(blast - b)
        k_sub = []
        q_sub = []
        for s in range(nsub):
            rs_ = ref_pt[s * GLA_SUB:s * GLA_SUB + 1, :]
            k_sub.append(k * jnp.exp(jnp.minimum(rs_ - b, EXP_CLAMP)))
            q_sub.append(jnp.where((row >= s * GLA_SUB) & (row < (s + 1) * GLA_SUB), qt, 0.0))
        eblast = jnp.exp(blast)
        for h in range(GLA_HEADS):
            hs = slice(h * GLA_DK, (h + 1) * GLA_DK)
            vs = slice(h * GLA_DV, (h + 1) * GLA_DV)
            qbig = jnp.concatenate([x[:, hs] for x in q_sub], axis=1).astype(BF16)
            kbig = jnp.concatenate([x[:, hs] for x in k_sub], axis=1).astype(BF16)
            sc = jnp.where(causal, _dot_nt(qbig, kbig), 0.0)
            vh = v_ref[0, sl, vs]
            vb = vh.astype(BF16)
            st = st_ref[h]
            o = _dot_nt(qe[:, hs].astype(BF16), st.astype(BF16)) + _dot(sc.astype(BF16), vb)
            st_ref[h] = st * eblast[:, hs] + _dot(vh.T.astype(BF16), ks[:, hs].astype(BF16))
            on = o * lax.rsqrt(jnp.mean(o * o, axis=-1, keepdims=True) + EPS) * ng_ref[...]
            o_ref[0, sl, vs] = on * rs_ref[0, sl, vs]


def _gla(q, k, la, v, rs, norm_g):
    tg = GLA_TG
    c = GLA_CHUNK
    r = jnp.arange(c)
    tri = (r[None, :] <= r[:, None])
    blk = tri & ((r[None, :] // GLA_SUB) == (r[:, None] // GLA_SUB))
    tt = jnp.concatenate([tri, blk], axis=0).astype(BF16)
    row3 = lambda w: pl.BlockSpec((1, tg, w), lambda b, i: (b, i, 0))
    const = lambda shape: pl.BlockSpec(shape, lambda b, i: tuple(0 for _ in shape))
    return pl.pallas_call(
        _gla_kernel,
        grid=(BATCH, SEQ // tg),
        in_specs=[row3(GLA_KEY), row3(GLA_KEY), row3(GLA_KEY), row3(GLA_VAL), row3(GLA_VAL),
                  const((1, GLA_DV)), const((2 * c, c))],
        out_specs=row3(GLA_VAL),
        out_shape=jax.ShapeDtypeStruct((BATCH, SEQ, GLA_VAL), F32),
        scratch_shapes=[pltpu.VMEM((GLA_HEADS, GLA_DV, GLA_DK), F32)],
        compiler_params=_params(("parallel", "arbitrary")),
        name="gla",
    )(q, k, la, v, rs, norm_g.reshape(1, GLA_DV), tt)


def _merge_kernel(ya_ref, yb_ref, ga_ref, gb_ref, x_ref, g1_ref, sc2_ref, sh2_ref, g2_ref, n2_ref,
                  wglu_ref, bglu_ref, wpa_ref, wpb_ref, wout_ref, wr_ref, wsgu_ref, wsd_ref,
                  base_ref, hp_ref, sc_ref):
    ya = ya_ref[...]
    z = _dot(ya.astype(BF16), wglu_ref[...]) + bglu_ref[...]
    ya2 = ya * _sigmoid(z)
    mixed = (ga_ref[0] * _dot(ya2.astype(BF16), wpa_ref[...])
             + gb_ref[0] * _dot(yb_ref[0].astype(BF16), wpb_ref[...]))
    x1 = x_ref[0] + g1_ref[0] * _dot(mixed.astype(BF16), wout_ref[...])
    r = lax.rsqrt(jnp.mean(x1 * x1, axis=-1, keepdims=True) + EPS)
    h = (x1 * r) * n2_ref[...] * (1.0 + sc2_ref[0]) + sh2_ref[0]
    hb = h.astype(BF16)
    sc_ref[...] = _sigmoid(_dot(hb, wr_ref[...]))
    gu = _dot(hb, wsgu_ref[...])
    gate = gu[:, :EXPERT_FF]
    mid = (gate * _sigmoid(gate) * gu[:, EXPERT_FF:]).astype(BF16)
    base_ref[...] = (x1 + g2_ref[0] * _dot(mid, wsd_ref[...])).reshape(MERGE_TM, *TOKEN_TILE)
    hp_ref[...] = h.reshape(MERGE_TM, *TOKEN_TILE)


def _merge(ya_tb, yb, ga, gb, x, g1, sc2, sh2, g2, n2, wglu, bglu, wpa, wpb, wout, wr, wsgu, wsd):
    tm = MERGE_TM
    row3 = lambda w: pl.BlockSpec((1, tm, w), lambda b, i: (b, i, 0))
    const = lambda a: pl.BlockSpec(a.shape, lambda b, i: tuple(0 for _ in a.shape))
    mod = pl.BlockSpec((1, 1, D_MODEL), lambda b, i: (b, 0, 0))
    nt = SEQ // tm
    flat = lambda w: pl.BlockSpec((tm, w), lambda b, i: (b * nt + i, 0))
    tiles = pl.BlockSpec((tm,) + TOKEN_TILE, lambda b, i: (b * nt + i, 0, 0))
    return pl.pallas_call(
        _merge_kernel,
        grid=(BATCH, nt),
        in_specs=[pl.BlockSpec((tm, S5_WIDTH), lambda b, i: (i, b)),
                  row3(GLA_VAL), row3(D_MODEL), row3(D_MODEL), row3(D_MODEL),
                  mod, mod, mod, mod, const(n2),
                  const(wglu), const(bglu), const(wpa), const(wpb), const(wout), const(wr),
                  const(wsgu), const(wsd)],
        out_specs=[tiles, tiles, flat(N_EXPERTS)],
        out_shape=[jax.ShapeDtypeStruct((TOKENS,) + TOKEN_TILE, F32),
                   jax.ShapeDtypeStruct((TOKENS,) + TOKEN_TILE, F32),
                   jax.ShapeDtypeStruct((TOKENS, N_EXPERTS), F32)],
        compiler_params=_params(("parallel", "parallel")),
        name="merge",
    )(ya_tb, yb, ga, gb, x, g1, sc2, sh2, g2, n2, wglu, bglu, wpa, wpb, wout, wr, wsgu, wsd)


def _first_argmax(x, iota, size):
    m = jnp.max(x, axis=0, keepdims=True)
    idx = jnp.min(jnp.where(x == m, iota, size), axis=0, keepdims=True)
    return m, idx


def _route_kernel(s_ref, bias_ref, tri_ref, e_ref, r_ref, w_ref, cnt_ref, carry_ref):
    @pl.when(pl.program_id(0) == 0)
    def _():
        carry_ref[...] = jnp.zeros_like(carry_ref)

    tr = ROUTE_TR
    neg = -jnp.inf
    s_t = s_ref[...].T
    biased = s_t + bias_ref[...]
    io_g = lax.broadcasted_iota(I32, (GROUP_SIZE, tr), 0)
    rows = []
    for g in range(N_GROUPS):
        xg = biased[g * GROUP_SIZE:(g + 1) * GROUP_SIZE, :]
        m1, i1 = _first_argmax(xg, io_g, GROUP_SIZE)
        m2 = jnp.max(jnp.where(io_g == i1, neg, xg), axis=0, keepdims=True)
        rows.append(m1 + m2)
    gs = jnp.concatenate(rows, axis=0)
    io_n = lax.broadcasted_iota(I32, (N_GROUPS, tr), 0)
    gsel = jnp.zeros((N_GROUPS, tr), F32)
    for _ in range(TOPK_GROUPS):
        _, gi = _first_argmax(gs, io_n, N_GROUPS)
        hit = io_n == gi
        gsel = jnp.where(hit, 1.0, gsel)
        gs = jnp.where(hit, neg, gs)
    masked = jnp.concatenate(
        [jnp.where(gsel[g:g + 1, :] > 0.0, biased[g * GROUP_SIZE:(g + 1) * GROUP_SIZE, :], neg)
         for g in range(N_GROUPS)], axis=0)
    io_e = lax.broadcasted_iota(I32, (N_EXPERTS, tr), 0)
    sel = jnp.zeros((N_EXPERTS, tr), F32)
    idxs = []
    for _ in range(TOP_K):
        _, ei = _first_argmax(masked, io_e, N_EXPERTS)
        hit = io_e == ei
        sel = jnp.where(hit, 1.0, sel)
        masked = jnp.where(hit, neg, masked)
        idxs.append(ei)
    den = jnp.sum(sel * s_t, axis=0, keepdims=True)
    rank = _dot(sel.astype(BF16), tri_ref[...]) + carry_ref[...]
    rks, sks = [], []
    for kk in range(TOP_K):
        hit = io_e == idxs[kk]
        rks.append(jnp.sum(jnp.where(hit, rank, 0.0), axis=0, keepdims=True))
        sks.append(jnp.sum(jnp.where(hit, s_t, 0.0), axis=0, keepdims=True))
    e_ref[...] = jnp.concatenate(idxs, axis=0)
    r_ref[...] = jnp.concatenate(rks, axis=0).astype(I32)
    w_ref[...] = jnp.concatenate(sks, axis=0) / den * ROUTE_SCALE
    carry_ref[...] += jnp.sum(sel, axis=1, keepdims=True)
    cnt_ref[...] = carry_ref[...]


def _route(scores, bias):
    tr = ROUTE_TR
    r = jnp.arange(tr)
    tri = (r[:, None] < r[None, :]).astype(BF16)
    kt = lambda dt: jax.ShapeDtypeStruct((TOP_K, TOKENS), dt)
    blk = pl.BlockSpec((TOP_K, tr), lambda i: (0, i))
    return pl.pallas_call(
        _route_kernel,
        grid=(TOKENS // tr,),
        in_specs=[pl.BlockSpec((tr, N_EXPERTS), lambda i: (i, 0)),
                  pl.BlockSpec((N_EXPERTS, 1), lambda i: (0, 0)),
                  pl.BlockSpec((tr, tr), lambda i: (0, 0))],
        out_specs=[blk, blk, blk, pl.BlockSpec((N_EXPERTS, 1), lambda i: (0, 0))],
        out_shape=[kt(I32), kt(I32), kt(F32), jax.ShapeDtypeStruct((N_EXPERTS, 1), F32)],
        scratch_shapes=[pltpu.VMEM((N_EXPERTS, 1), F32)],
        compiler_params=_params(("arbitrary",)),
        name="route",
    )(scores, bias.reshape(N_EXPERTS, 1), tri)


def _slots_kernel(e_ref, r_ref, ps_ref, d_ref):
    tr = ROUTE_TR
    io_e = lax.broadcasted_iota(I32, (N_EXPERTS, tr), 0)
    ps = ps_ref[...]
    rows = []
    for kk in range(TOP_K):
        hit = io_e == e_ref[kk:kk + 1, :]
        rows.append(jnp.sum(jnp.where(hit, ps, 0.0), axis=0, keepdims=True))
    d_ref[...] = jnp.concatenate(rows, axis=0).astype(I32) + r_ref[...]


def _slots(e_idx, rank, pad_start):
    tr = ROUTE_TR
    blk = pl.BlockSpec((TOP_K, tr), lambda i: (0, i))
    return pl.pallas_call(
        _slots_kernel,
        grid=(TOKENS // tr,),
        in_specs=[blk, blk, pl.BlockSpec((N_EXPERTS, 1), lambda i: (0, 0))],
        out_specs=blk,
        out_shape=jax.ShapeDtypeStruct((TOP_K, TOKENS), I32),
        compiler_params=_params(("parallel",)),
        name="slots",
    )(e_idx, rank, pad_start.astype(F32).reshape(N_EXPERTS, 1))


def _dispatch_kernel(ps_ref, cnt_ref, nb_ref, dest_ref, h_ref, xs_ref, zbuf, sem, zsem):
    tq = DISPATCH_TQ
    step = pl.program_id(0)
    n_used = ps_ref[N_EXPERTS - 1] // MOE_BLOCK + nb_ref[N_EXPERTS - 1]

    def zero_fill(act):
        def block(j, carry):
            r = pl.ds(pl.multiple_of(j * MOE_BLOCK, MOE_BLOCK), MOE_BLOCK)
            act(pltpu.make_async_copy(zbuf, xs_ref.at[r], zsem))
            return carry

        def expert(e, carry):
            @pl.when(cnt_ref[e] < nb_ref[e] * MOE_BLOCK)
            def _():
                block(ps_ref[e] // MOE_BLOCK + nb_ref[e] - 1, 0)
            return carry

        lax.fori_loop(0, N_EXPERTS, expert, 0)
        lax.fori_loop(n_used, N_BLOCKS, block, 0)

    @pl.when(step == 0)
    def _():
        zbuf[...] = jnp.zeros_like(zbuf)
        zero_fill(lambda cp: cp.start())
        zero_fill(lambda cp: cp.wait())

    def start(i, carry):
        for kk in range(TOP_K):
            d = dest_ref[i * TOP_K + kk]
            pltpu.make_async_copy(h_ref.at[i], xs_ref.at[d],
                                  sem).start(priority=kk % 2)
        return carry

    lax.fori_loop(0, tq, start, 0, unroll=2)
    for _ in range(TOP_K):
        pltpu.make_async_copy(h_ref, xs_ref.at[pl.ds(0, tq)], sem).wait()


def _dispatch(pad_start, counts, n_blk, dest, hp):
    tq = DISPATCH_TQ
    grid_spec = pltpu.PrefetchScalarGridSpec(
        num_scalar_prefetch=3,
        grid=(TOKENS // tq,),
        in_specs=[pl.BlockSpec((tq * TOP_K,), lambda i, *_: (i,), memory_space=pltpu.SMEM),
                  pl.BlockSpec((tq,) + TOKEN_TILE, lambda i, *_: (i, 0, 0))],
        out_specs=pl.BlockSpec(memory_space=pl.ANY),
        scratch_shapes=[pltpu.VMEM((MOE_BLOCK,) + TOKEN_TILE, F32),
                        pltpu.SemaphoreType.DMA, pltpu.SemaphoreType.DMA],
    )
    return pl.pallas_call(
        _dispatch_kernel,
        grid_spec=grid_spec,
        out_shape=jax.ShapeDtypeStruct((N_SLOTS,) + TOKEN_TILE, F32),
        compiler_params=_params(("arbitrary",)),
        name="dispatch",
    )(pad_start, counts, n_blk, dest, hp)


def _expert_kernel(ps_ref, nb_ref, wg_ref, wu_ref, wd_ref, xs_ref, ys_ref,
                   xbuf, ybuf, wgus, wds, xsem, ysem):
    e = pl.program_id(0)
    nb = nb_ref[e]
    first = ps_ref[e] // MOE_BLOCK
    n_used = ps_ref[N_EXPERTS - 1] // MOE_BLOCK + nb_ref[N_EXPERTS - 1]
    nbuf = EXPERT_NBUF

    def rows(g):
        return pl.ds(pl.multiple_of(g * MOE_BLOCK, MOE_BLOCK), MOE_BLOCK)

    def x_copy(g):
        slot = g % nbuf
        return pltpu.make_async_copy(xs_ref.at[rows(g)], xbuf.at[slot], xsem.at[slot])

    def y_copy(g):
        slot = g % nbuf
        return pltpu.make_async_copy(ybuf.at[slot], ys_ref.at[rows(g)], ysem.at[slot])

    @pl.when(e == 0)
    def _():
        for g in range(nbuf - 1):
            @pl.when(g < n_used)
            def _():
                x_copy(g).start()

    @pl.when(nb > 0)
    def _():
        wgus[:, :EXPERT_FF] = wg_ref[0].astype(BF16)
        wgus[:, EXPERT_FF:] = wu_ref[0].astype(BF16)
        wds[...] = wd_ref[0].astype(BF16)

        def body(j, carry):
            g = first + j
            slot = g % nbuf
            x_copy(g).wait()

            @pl.when(g + nbuf - 1 < n_used)
            def _():
                x_copy(g + nbuf - 1).start()

            @pl.when(g >= nbuf)
            def _():
                y_copy(g - nbuf).wait()

            gu = _dot(xbuf[slot].reshape(MOE_BLOCK, D_MODEL).astype(BF16), wgus[...])
            gate = gu[:, :EXPERT_FF]
            mid = (gate * _sigmoid(gate) * gu[:, EXPERT_FF:]).astype(BF16)
            ybuf[slot] = _dot(mid, wds[...]).reshape(MOE_BLOCK, *TOKEN_TILE)
            y_copy(g).start()
            return carry

        lax.fori_loop(0, nb, body, 0)

    @pl.when(e == N_EXPERTS - 1)
    def _():
        for d in range(nbuf):
            g = n_used - nbuf + d

            @pl.when(g >= 0)
            def _():
                y_copy(g).wait()

        ybuf[0] = jnp.zeros((MOE_BLOCK,) + TOKEN_TILE, F32)

        def z_copy(j):
            return pltpu.make_async_copy(ybuf.at[0], ys_ref.at[rows(j)], ysem.at[0])

        def z_start(j, carry):
            z_copy(j).start()
            return carry

        def z_wait(j, carry):
            z_copy(j).wait()
            return carry

        lax.fori_loop(n_used, N_BLOCKS, z_start, 0)
        lax.fori_loop(n_used, N_BLOCKS, z_wait, 0)


def _experts(pad_start, n_blk, xs, wg, wu, wd):
    wspec = lambda shape: pl.BlockSpec((1,) + shape, lambda e, ps, nb: (e, 0, 0))
    grid_spec = pltpu.PrefetchScalarGridSpec(
        num_scalar_prefetch=2,
        grid=(N_EXPERTS,),
        in_specs=[wspec((D_MODEL, EXPERT_FF)), wspec((D_MODEL, EXPERT_FF)),
                  wspec((EXPERT_FF, D_MODEL)), pl.BlockSpec(memory_space=pl.ANY)],
        out_specs=pl.BlockSpec(memory_space=pl.ANY),
        scratch_shapes=[pltpu.VMEM((EXPERT_NBUF, MOE_BLOCK) + TOKEN_TILE, F32),
                        pltpu.VMEM((EXPERT_NBUF, MOE_BLOCK) + TOKEN_TILE, F32),
                        pltpu.VMEM((D_MODEL, 2 * EXPERT_FF), BF16),
                        pltpu.VMEM((EXPERT_FF, D_MODEL), BF16),
                        pltpu.SemaphoreType.DMA((EXPERT_NBUF,)),
                        pltpu.SemaphoreType.DMA((EXPERT_NBUF,))],
    )
    return pl.pallas_call(
        _expert_kernel,
        grid_spec=grid_spec,
        out_shape=jax.ShapeDtypeStruct((N_SLOTS,) + TOKEN_TILE, F32),
        compiler_params=_params(("arbitrary",)),
        name="experts",
    )(pad_start, n_blk, wg, wu, wd, xs)


def _combine_kernel(dcur_ref, dnext_ref, w_ref, base_ref, g2_ref, fg_ref, ys_ref, o_ref,
                    buf, acc_ref, sem):
    tq = COMBINE_TQ
    i = pl.program_id(0)
    n = pl.num_programs(0)

    def issue(dest_ref, slot):
        def start(t, carry):
            for kk in range(TOP_K):
                d = dest_ref[t * TOP_K + kk]
                pltpu.make_async_copy(ys_ref.at[d], buf.at[slot, kk, t],
                                      sem.at[slot]).start(priority=kk % 2)
            return carry
        lax.fori_loop(0, tq, start, 0, unroll=2)

    @pl.when(i == 0)
    def _():
        issue(dcur_ref, 0)

    @pl.when(i + 1 < n)
    def _():
        issue(dnext_ref, (i + 1) % 2)

    slot = i % 2
    for kk in range(TOP_K):
        pltpu.make_async_copy(ys_ref.at[pl.ds(0, tq)], buf.at[slot, kk], sem.at[slot]).wait()

    def token(t, carry):
        a = buf[slot, 0, t] * w_ref[t * TOP_K]
        for kk in range(1, TOP_K):
            a = a + buf[slot, kk, t] * w_ref[t * TOP_K + kk]
        acc_ref[t] = a
        return carry

    lax.fori_loop(0, tq, token, 0, unroll=4)
    xo = base_ref[...] + g2_ref[0] * acc_ref[...]
    ms = jnp.sum(jnp.sum(xo * xo, axis=2, keepdims=True), axis=1, keepdims=True) * (1.0 / D_MODEL)
    o_ref[0] = ((xo * lax.rsqrt(ms + EPS)) * fg_ref[...]).reshape(tq, D_MODEL)


def _combine(dest, w_flat, base, g2, final_g, ys):
    tq = COMBINE_TQ
    nt = SEQ // tq
    n = TOKENS // tq
    smem = lambda imap: pl.BlockSpec((tq * TOP_K,), imap, memory_space=pltpu.SMEM)
    return pl.pallas_call(
        _combine_kernel,
        grid=(n,),
        in_specs=[smem(lambda i: (i,)), smem(lambda i: (jnp.minimum(i + 1, n - 1),)),
                  smem(lambda i: (i,)),
                  pl.BlockSpec((tq,) + TOKEN_TILE, lambda i: (i, 0, 0)),
                  pl.BlockSpec((1,) + TOKEN_TILE, lambda i: (i // nt, 0, 0)),
                  pl.BlockSpec(TOKEN_TILE, lambda i: (0, 0)),
                  pl.BlockSpec(memory_space=pl.ANY)],
        out_specs=pl.BlockSpec((1, tq, D_MODEL), lambda i: (i // nt, i % nt, 0)),
        out_shape=jax.ShapeDtypeStruct((BATCH, SEQ, D_MODEL), F32),
        scratch_shapes=[pltpu.VMEM((2, TOP_K, tq) + TOKEN_TILE, F32),
                        pltpu.VMEM((tq,) + TOKEN_TILE, F32),
                        pltpu.SemaphoreType.DMA((2,))],
        compiler_params=_params(("arbitrary",)),
        name="combine",
    )(dest, dest, w_flat, base, g2.reshape((BATCH,) + TOKEN_TILE), final_g.reshape(TOKEN_TILE), ys)


def kernel(x, c, ada_w, ada_b, norm1_g, w_in, s5_lam_re, s5_lam_im, s5_log_dt, s5_b_re, s5_b_im,
           s5_c_re, s5_c_im, s5_d, s5_w_glu, s5_b_glu, w_proj_a, gla_w_gk2, gla_b_gk2, gla_norm_g,
           w_proj_b, w_out, norm2_g, router_w, router_bias, exp_w_gate, exp_w_up, exp_w_down,
           sh_w_gate, sh_w_up, sh_w_down, final_g):
    l = 0
    mod = _ada(c, ada_w[l], ada_b[l])
    sh1, sc1, g1, sh2, sc2, g2 = [m.reshape(BATCH, 1, D_MODEL) for m in jnp.split(mod, 6, axis=-1)]

    w = w_in[l]
    gk0 = _V0 + GLA_VAL
    w_main = jnp.concatenate([w[:, :gk0], w[:, gk0 + GLA_GATE_RANK:]], axis=1).astype(BF16)
    w_gk = jnp.pad(w[:, gk0:gk0 + GLA_GATE_RANK], ((0, 0), (0, LANES - GLA_GATE_RANK))).astype(BF16)
    w_gk2 = jnp.pad(gla_w_gk2[l], ((0, LANES - GLA_GATE_RANK), (0, 0))).astype(BF16)
    u_tb, q, k, v, la, rs, ga, gb = _inproj(
        x, norm1_g[l].reshape(1, D_MODEL), sc1, sh1, w_main, w_gk, w_gk2,
        gla_b_gk2[l].reshape(1, GLA_KEY))
    tables = _s5_tables(s5_lam_re[l], s5_lam_im[l], s5_log_dt[l], s5_b_re[l], s5_b_im[l],
                        s5_c_re[l], s5_c_im[l], s5_d[l])
    ya_tb = _s5(u_tb.reshape(TOKENS, S5_WIDTH), *tables)
    yb = _gla(q, k, la, v, rs, gla_norm_g[l])

    wsgu = jnp.concatenate([sh_w_gate[l], sh_w_up[l]], axis=1).astype(BF16)
    base, hp, scores = _merge(
        ya_tb.reshape(SEQ, BATCH * S5_WIDTH), yb, ga, gb, x, g1, sc2, sh2, g2,
        norm2_g[l].reshape(1, D_MODEL), s5_w_glu[l].astype(BF16), s5_b_glu[l].reshape(1, S5_WIDTH),
        w_proj_a[l].astype(BF16), w_proj_b[l].astype(BF16), w_out[l].astype(BF16),
        router_w[l].astype(BF16), wsgu, sh_w_down[l].astype(BF16))

    e_idx, rank, w_k, cnt = _route(scores, router_bias[l])
    counts = cnt[:, 0].astype(I32)
    padded = (counts + MOE_BLOCK - 1) // MOE_BLOCK * MOE_BLOCK
    pad_end = jnp.cumsum(padded)
    pad_start = pad_end - padded
    dest = _slots(e_idx, rank, pad_start).T.reshape(-1)
    pad_start = pad_start.astype(I32)
    n_blk = (padded // MOE_BLOCK).astype(I32)
    xs = _dispatch(pad_start, counts, n_blk, dest, hp)
    ys = _experts(pad_start, n_blk, xs, exp_w_gate[l], exp_w_up[l], exp_w_down[l])
    return _combine(dest, w_k.T.reshape(-1), base, g2, final_g, ys)
```

```python
import functools

import jax
import jax.numpy as jnp
from jax import lax
from jax.experimental import pallas as pl
from jax.experimental.pallas import tpu as pltpu

F32 = jnp.float32
BF16 = jnp.bfloat16
I32 = jnp.int32

D_MODEL = 1024
BATCH = 8
SEQ = 2048
TOKENS = BATCH * SEQ
S5_WIDTH = 512
S5_GROUP = 16
S5_GROUPS = 32
S5_STATE = 64
S5_COLS = S5_GROUPS * S5_STATE
GLA_HEADS = 4
GLA_DK = 64
GLA_DV = 128
GLA_KEY = GLA_HEADS * GLA_DK
GLA_VAL = GLA_HEADS * GLA_DV
GLA_GATE_RANK = 16
GLA_GATE_TAU = 16.0
GLA_CHUNK = 64
GLA_SUB = 16
N_EXPERTS = 256
TOP_K = 8
N_GROUPS = 8
GROUP_SIZE = N_EXPERTS // N_GROUPS
TOPK_GROUPS = 4
EXPERT_FF = 256
ROUTE_SCALE = 2.5
MOE_BLOCK = 128
EPS = 1e-6
N_SLOTS = -(-(TOKENS * TOP_K + N_EXPERTS * (MOE_BLOCK - 1)) // MOE_BLOCK) * MOE_BLOCK
N_BLOCKS = N_SLOTS // MOE_BLOCK

LANES = 128
SUBLANES = 8
TOKEN_TILE = (SUBLANES, LANES)
assert D_MODEL == SUBLANES * LANES
VMEM_LIMIT = 56 * 1024 * 1024

ADA_TN = 1536
INPROJ_TM = 512
S5_TC = 64
S5_COL_CHUNK = 512
GLA_TG = 256
MERGE_TM = 512
ROUTE_TR = 512
DISPATCH_TQ = 1024
COMBINE_TQ = 128
EXPERT_UNIT = 2
EXPERT_NBUF = 6
EXP_CLAMP = 60.0


def _params(sem, vmem=VMEM_LIMIT):
    return pltpu.CompilerParams(dimension_semantics=sem, vmem_limit_bytes=vmem)


def _dot(a, b):
    return jnp.dot(a, b, preferred_element_type=F32)


def _dot_nt(a, b):
    return lax.dot_general(a, b, (((1,), (1,)), ((), ())), preferred_element_type=F32)


def _sigmoid(x):
    return jax.nn.sigmoid(x)


def _ada_kernel(c_ref, w_ref, b_ref, o_ref):
    c = c_ref[...]
    s = (c * _sigmoid(c)).astype(BF16)
    o_ref[...] = _dot(s, w_ref[...].astype(BF16)) + b_ref[...]


def _ada(c, w, b):
    n = w.shape[1]
    return pl.pallas_call(
        _ada_kernel,
        grid=(n // ADA_TN,),
        in_specs=[
            pl.BlockSpec((BATCH, D_MODEL), lambda j: (0, 0)),
            pl.BlockSpec((D_MODEL, ADA_TN), lambda j: (0, j)),
            pl.BlockSpec((1, ADA_TN), lambda j: (0, j)),
        ],
        out_specs=pl.BlockSpec((BATCH, ADA_TN), lambda j: (0, j)),
        out_shape=jax.ShapeDtypeStruct((BATCH, n), F32),
        compiler_params=_params(("arbitrary",)),
        name="ada",
    )(c, w, b.reshape(1, n))


_U0, _Q0, _K0, _V0, _R0, _GA0, _GB0, _END = 0, 512, 768, 1024, 1536, 2048, 3072, 4096


def _inproj_kernel(x_ref, g_ref, sc_ref, sh_ref, wm_ref, wgk_ref, wgk2_ref, bgk_ref,
                   u_ref, q_ref, k_ref, v_ref, la_ref, rs_ref, ga_ref, gb_ref):
    x = x_ref[0]
    r = lax.rsqrt(jnp.mean(x * x, axis=-1, keepdims=True) + EPS)
    h = (x * r) * g_ref[...] * (1.0 + sc_ref[0]) + sh_ref[0]
    hb = h.astype(BF16)

    def seg(lo, hi):
        return _dot(hb, wm_ref[:, lo:hi])

    u_ref[...] = seg(_U0, _Q0)
    q_ref[0] = seg(_Q0, _K0) * (GLA_DK ** -0.5)
    k_ref[0] = seg(_K0, _V0)
    v_ref[0] = seg(_V0, _R0)
    rr = seg(_R0, _GA0)
    rs_ref[0] = rr * _sigmoid(rr)
    ga_ref[0] = _sigmoid(seg(_GA0, _GB0))
    gb_ref[0] = _sigmoid(seg(_GB0, _END))
    gk = _dot(hb, wgk_ref[...])
    z = _dot(gk.astype(BF16), wgk2_ref[...]) + bgk_ref[...]
    la_ref[0] = -(jnp.maximum(-z, 0.0) + jnp.log1p(jnp.exp(-jnp.abs(z)))) * (1.0 / GLA_GATE_TAU)


def _inproj(x, g, sc, sh, w_main, w_gk, w_gk2, b_gk2):
    tm = INPROJ_TM
    nt = SEQ // tm
    row3 = lambda w: pl.BlockSpec((1, tm, w), lambda b, i: (b, i, 0))
    const = lambda shape: pl.BlockSpec(shape, lambda b, i: tuple(0 for _ in shape))
    mod = pl.BlockSpec((1, 1, D_MODEL), lambda b, i: (b, 0, 0))
    bld = lambda w: jax.ShapeDtypeStruct((BATCH, SEQ, w), F32)
    return pl.pallas_call(
        _inproj_kernel,
        grid=(BATCH, nt),
        in_specs=[row3(D_MODEL), const((1, D_MODEL)), mod, mod,
                  const((D_MODEL, _END)), const((D_MODEL, LANES)), const((LANES, GLA_KEY)),
                  const((1, GLA_KEY))],
        out_specs=[pl.BlockSpec((tm, S5_WIDTH), lambda b, i: (i, b)),
                   row3(GLA_KEY), row3(GLA_KEY), row3(GLA_VAL), row3(GLA_KEY), row3(GLA_VAL),
                   row3(D_MODEL), row3(D_MODEL)],
        out_shape=[jax.ShapeDtypeStruct((SEQ, BATCH * S5_WIDTH), F32),
                   bld(GLA_KEY), bld(GLA_KEY), bld(GLA_VAL), bld(GLA_KEY), bld(GLA_VAL),
                   bld(D_MODEL), bld(D_MODEL)],
        compiler_params=_params(("parallel", "parallel")),
        name="inproj",
    )(x, g, sc, sh, w_main, w_gk, w_gk2, b_gk2)


def _s5_kernel(u_ref, bre_ref, bim_ref, cre_ref, cim_ref, are_ref, aim_ref, d_ref,
               y_ref, s_ref, st_ref):
    half = S5_COLS // 2

    @pl.when(pl.program_id(0) == 0)
    def _():
        st_ref[...] = jnp.zeros_like(st_ref)

    u = u_ref[...].reshape(S5_TC, BATCH, S5_WIDTH).reshape(S5_TC * BATCH, S5_WIDTH)
    ub = u.astype(BF16)
    for j in range(2):
        uj = ub[:, j * 256:(j + 1) * 256]
        s_ref[:, j * half:(j + 1) * half] = _dot(uj, bre_ref[j])
        s_ref[:, S5_COLS + j * half:S5_COLS + (j + 1) * half] = _dot(uj, bim_ref[j])

    for cc in range(S5_COLS // S5_COL_CHUNK):
        re_sl = slice(cc * S5_COL_CHUNK, (cc + 1) * S5_COL_CHUNK)
        im_sl = slice(S5_COLS + cc * S5_COL_CHUNK, S5_COLS + (cc + 1) * S5_COL_CHUNK)
        ar = are_ref[:, re_sl]
        ai = aim_ref[:, re_sl]

        def body(t, carry, re_sl=re_sl, im_sl=im_sl, ar=ar, ai=ai):
            sr, si = carry
            rows = pl.ds(pl.multiple_of(t * BATCH, BATCH), BATCH)
            nr = ar * sr - ai * si + s_ref[rows, re_sl]
            ni = ar * si + ai * sr + s_ref[rows, im_sl]
            s_ref[rows, re_sl] = nr
            s_ref[rows, im_sl] = ni
            return nr, ni

        sr, si = lax.fori_loop(0, S5_TC, body, (st_ref[:, re_sl], st_ref[:, im_sl]), unroll=4)
        st_ref[:, re_sl] = sr
        st_ref[:, im_sl] = si

    ys = []
    for j in range(2):
        sre = s_ref[:, j * half:(j + 1) * half].astype(BF16)
        sim = s_ref[:, S5_COLS + j * half:S5_COLS + (j + 1) * half].astype(BF16)
        ys.append(_dot(sre, cre_ref[j]) + _dot(sim, cim_ref[j]))
    y = jnp.concatenate(ys, axis=1) + d_ref[...] * u
    y_ref[...] = jax.nn.gelu(y).reshape(S5_TC, BATCH, S5_WIDTH).reshape(S5_TC, BATCH * S5_WIDTH)


def _s5(u_tb, bre, bim, cre, cimn, are, aim, dflat):
    rows = S5_TC * BATCH
    const = lambda shape: pl.BlockSpec(shape, lambda i: tuple(0 for _ in shape))
    return pl.pallas_call(
        _s5_kernel,
        grid=(SEQ // S5_TC,),
        in_specs=[pl.BlockSpec((S5_TC, BATCH * S5_WIDTH), lambda i: (i, 0)),
                  const(bre.shape), const(bim.shape), const(cre.shape), const(cimn.shape),
                  const(are.shape), const(aim.shape), const(dflat.shape)],
        out_specs=pl.BlockSpec((S5_TC, BATCH * S5_WIDTH), lambda i: (i, 0)),
        out_shape=jax.ShapeDtypeStruct((SEQ, BATCH * S5_WIDTH), F32),
        scratch_shapes=[pltpu.VMEM((rows, 2 * S5_COLS), F32),
                        pltpu.VMEM((BATCH, 2 * S5_COLS), F32)],
        compiler_params=_params(("arbitrary",)),
        name="s5",
    )(u_tb, bre, bim, cre, cimn, are, aim, dflat)


def _s5_tables(lam_re, lam_im, log_dt, b_re, b_im, c_re, c_im, d_skip):
    lr = lam_re.astype(F32)
    li = lam_im.astype(F32)
    dt = jnp.exp(log_dt.astype(F32))[:, None]
    mag = jnp.exp(lr * dt)
    abar_re = mag * jnp.cos(li * dt)
    abar_im = mag * jnp.sin(li * dt)
    den = lr * lr + li * li
    num_re = abar_re - 1.0
    coef_re = (num_re * lr + abar_im * li) / den
    coef_im = (abar_im * lr - num_re * li) / den
    cr, ci = coef_re[..., None], coef_im[..., None]
    br, bi = b_re.astype(F32), b_im.astype(F32)
    bbar_re = cr * br - ci * bi
    bbar_im = cr * bi + ci * br
    eye = jnp.eye(S5_GROUPS // 2, dtype=F32)

    def in_map(bb):
        bb = bb.reshape(2, S5_GROUPS // 2, S5_STATE, S5_GROUP)
        return jnp.einsum('jgph,gk->jghkp', bb, eye).reshape(2, 256, S5_COLS // 2).astype(BF16)

    def out_map(cc):
        cc = cc.reshape(2, S5_GROUPS // 2, S5_GROUP, S5_STATE)
        return jnp.einsum('jghp,gk->jgpkh', cc, eye).reshape(2, S5_COLS // 2, 256).astype(BF16)

    are = jnp.broadcast_to(abar_re.reshape(1, S5_COLS), (BATCH, S5_COLS))
    aim = jnp.broadcast_to(abar_im.reshape(1, S5_COLS), (BATCH, S5_COLS))
    return (in_map(bbar_re), in_map(bbar_im), out_map(c_re.astype(F32)), out_map(-c_im.astype(F32)),
            are, aim, d_skip.astype(F32).reshape(1, S5_WIDTH))


def _gla_kernel(q_ref, k_ref, la_ref, v_ref, rs_ref, ng_ref, tt_ref, o_ref, st_ref):
    @pl.when(pl.program_id(1) == 0)
    def _():
        st_ref[...] = jnp.zeros_like(st_ref)

    c = GLA_CHUNK
    tt = tt_ref[...]
    row = lax.broadcasted_iota(I32, (c, GLA_KEY), 0)
    ri = lax.broadcasted_iota(I32, (c, c), 0)
    ci = lax.broadcasted_iota(I32, (c, c), 1)
    causal = ci <= ri
    nsub = c // GLA_SUB

    for ch in range(GLA_TG // c):
        sl = slice(ch * c, (ch + 1) * c)
        g = la_ref[0, sl, :]
        g1 = g.astype(BF16)
        r1 = g - g1.astype(F32)
        g2 = r1.astype(BF16)
        g3 = (r1 - g2.astype(F32)).astype(BF16)
        cs = _dot(tt, g1) + _dot(tt, g2) + _dot(tt, g3)
        b = cs[0:c]
        cl = cs[c:2 * c]
        ref_pt = b - cl
        q = q_ref[0, sl, :]
        k = k_ref[0, sl, :]
        qt = q * jnp.exp(cl)
        qe = q * jnp.exp(b)
        blast = b[c - 1:c, :]
        ks = k * jnp.exp(blast - b)
        k_sub = []
        q_sub = []
        for s in range(nsub):
            rs_ = ref_pt[s * GLA_SUB:s * GLA_SUB + 1, :]
            k_sub.append(k * jnp.exp(jnp.minimum(rs_ - b, EXP_CLAMP)))
            q_sub.append(jnp.where((row >= s * GLA_SUB) & (row < (s + 1) * GLA_SUB), qt, 0.0))
        eblast = jnp.exp(blast)
        for h in range(GLA_HEADS):
            hs = slice(h * GLA_DK, (h + 1) * GLA_DK)
            vs = slice(h * GLA_DV, (h + 1) * GLA_DV)
            qbig = jnp.concatenate([x[:, hs] for x in q_sub], axis=1).astype(BF16)
            kbig = jnp.concatenate([x[:, hs] for x in k_sub], axis=1).astype(BF16)
            sc = jnp.where(causal, _dot_nt(qbig, kbig), 0.0)
            vh = v_ref[0, sl, vs]
            vb = vh.astype(BF16)
            st = st_ref[h]
            o = _dot_nt(qe[:, hs].astype(BF16), st.astype(BF16)) + _dot(sc.astype(BF16), vb)
            st_ref[h] = st * eblast[:, hs] + _dot(vh.T.astype(BF16), ks[:, hs].astype(BF16))
            on = o * lax.rsqrt(jnp.mean(o * o, axis=-1, keepdims=True) + EPS) * ng_ref[...]
            o_ref[0, sl, vs] = on * rs_ref[0, sl, vs]


def _gla(q, k, la, v, rs, norm_g):
    tg = GLA_TG
    c = GLA_CHUNK
    r = jnp.arange(c)
    tri = (r[None, :] <= r[:, None])
    blk = tri & ((r[None, :] // GLA_SUB) == (r[:, None] // GLA_SUB))
    tt = jnp.concatenate([tri, blk], axis=0).astype(BF16)
    row3 = lambda w: pl.BlockSpec((1, tg, w), lambda b, i: (b, i, 0))
    const = lambda shape: pl.BlockSpec(shape, lambda b, i: tuple(0 for _ in shape))
    return pl.pallas_call(
        _gla_kernel,
        grid=(BATCH, SEQ // tg),
        in_specs=[row3(GLA_KEY), row3(GLA_KEY), row3(GLA_KEY), row3(GLA_VAL), row3(GLA_VAL),
                  const((1, GLA_DV)), const((2 * c, c))],
        out_specs=row3(GLA_VAL),
        out_shape=jax.ShapeDtypeStruct((BATCH, SEQ, GLA_VAL), F32),
        scratch_shapes=[pltpu.VMEM((GLA_HEADS, GLA_DV, GLA_DK), F32)],
        compiler_params=_params(("parallel", "arbitrary")),
        name="gla",
    )(q, k, la, v, rs, norm_g.reshape(1, GLA_DV), tt)


def _merge_kernel(ya_ref, yb_ref, ga_ref, gb_ref, x_ref, g1_ref, sc2_ref, sh2_ref, g2_ref, n2_ref,
                  wglu_ref, bglu_ref, wpa_ref, wpb_ref, wout_ref, wr_ref, wsgu_ref, wsd_ref,
                  base_ref, hp_ref, sc_ref):
    ya = ya_ref[...]
    z = _dot(ya.astype(BF16), wglu_ref[...]) + bglu_ref[...]
    ya2 = ya * _sigmoid(z)
    mixed = (ga_ref[0] * _dot(ya2.astype(BF16), wpa_ref[...])
             + gb_ref[0] * _dot(yb_ref[0].astype(BF16), wpb_ref[...]))
    x1 = x_ref[0] + g1_ref[0] * _dot(mixed.astype(BF16), wout_ref[...])
    r = lax.rsqrt(jnp.mean(x1 * x1, axis=-1, keepdims=True) + EPS)
    h = (x1 * r) * n2_ref[...] * (1.0 + sc2_ref[0]) + sh2_ref[0]
    hb = h.astype(BF16)
    sc_ref[...] = _sigmoid(_dot(hb, wr_ref[...]))
    gu = _dot(hb, wsgu_ref[...])
    gate = gu[:, :EXPERT_FF]
    mid = (gate * _sigmoid(gate) * gu[:, EXPERT_FF:]).astype(BF16)
    base_ref[...] = (x1 + g2_ref[0] * _dot(mid, wsd_ref[...])).reshape(MERGE_TM, *TOKEN_TILE)
    hp_ref[...] = h.reshape(MERGE_TM, *TOKEN_TILE)


def _merge(ya_tb, yb, ga, gb, x, g1, sc2, sh2, g2, n2, wglu, bglu, wpa, wpb, wout, wr, wsgu, wsd):
    tm = MERGE_TM
    row3 = lambda w: pl.BlockSpec((1, tm, w), lambda b, i: (b, i, 0))
    const = lambda a: pl.BlockSpec(a.shape, lambda b, i: tuple(0 for _ in a.shape))
    mod = pl.BlockSpec((1, 1, D_MODEL), lambda b, i: (b, 0, 0))
    nt = SEQ // tm
    flat = lambda w: pl.BlockSpec((tm, w), lambda b, i: (b * nt + i, 0))
    tiles = pl.BlockSpec((tm,) + TOKEN_TILE, lambda b, i: (b * nt + i, 0, 0))
    return pl.pallas_call(
        _merge_kernel,
        grid=(BATCH, nt),
        in_specs=[pl.BlockSpec((tm, S5_WIDTH), lambda b, i: (i, b)),
                  row3(GLA_VAL), row3(D_MODEL), row3(D_MODEL), row3(D_MODEL),
                  mod, mod, mod, mod, const(n2),
                  const(wglu), const(bglu), const(wpa), const(wpb), const(wout), const(wr),
                  const(wsgu), const(wsd)],
        out_specs=[tiles, tiles, flat(N_EXPERTS)],
        out_shape=[jax.ShapeDtypeStruct((TOKENS,) + TOKEN_TILE, F32),
                   jax.ShapeDtypeStruct((TOKENS,) + TOKEN_TILE, F32),
                   jax.ShapeDtypeStruct((TOKENS, N_EXPERTS), F32)],
        compiler_params=_params(("parallel", "parallel")),
        name="merge",
    )(ya_tb, yb, ga, gb, x, g1, sc2, sh2, g2, n2, wglu, bglu, wpa, wpb, wout, wr, wsgu, wsd)


def _first_argmax(x, iota, size):
    m = jnp.max(x, axis=0, keepdims=True)
    idx = jnp.min(jnp.where(x == m, iota, size), axis=0, keepdims=True)
    return m, idx


def _route_kernel(s_ref, bias_ref, tri_ref, e_ref, r_ref, w_ref, cnt_ref, carry_ref):
    @pl.when(pl.program_id(0) == 0)
    def _():
        carry_ref[...] = jnp.zeros_like(carry_ref)

    tr = ROUTE_TR
    neg = -jnp.inf
    s_t = s_ref[...].T
    biased = s_t + bias_ref[...]
    io_g = lax.broadcasted_iota(I32, (GROUP_SIZE, tr), 0)
    rows = []
    for g in range(N_GROUPS):
        xg = biased[g * GROUP_SIZE:(g + 1) * GROUP_SIZE, :]
        m1, i1 = _first_argmax(xg, io_g, GROUP_SIZE)
        m2 = jnp.max(jnp.where(io_g == i1, neg, xg), axis=0, keepdims=True)
        rows.append(m1 + m2)
    gs = jnp.concatenate(rows, axis=0)
    io_n = lax.broadcasted_iota(I32, (N_GROUPS, tr), 0)
    gsel = jnp.zeros((N_GROUPS, tr), F32)
    for _ in range(TOPK_GROUPS):
        _, gi = _first_argmax(gs, io_n, N_GROUPS)
        hit = io_n == gi
        gsel = jnp.where(hit, 1.0, gsel)
        gs = jnp.where(hit, neg, gs)
    masked = jnp.concatenate(
        [jnp.where(gsel[g:g + 1, :] > 0.0, biased[g * GROUP_SIZE:(g + 1) * GROUP_SIZE, :], neg)
         for g in range(N_GROUPS)], axis=0)
    io_e = lax.broadcasted_iota(I32, (N_EXPERTS, tr), 0)
    sel = jnp.zeros((N_EXPERTS, tr), F32)
    idxs = []
    for _ in range(TOP_K):
        _, ei = _first_argmax(masked, io_e, N_EXPERTS)
        hit = io_e == ei
        sel = jnp.where(hit, 1.0, sel)
        masked = jnp.where(hit, neg, masked)
        idxs.append(ei)
    den = jnp.sum(sel * s_t, axis=0, keepdims=True)
    rank = _dot(sel.astype(BF16), tri_ref[...]) + carry_ref[...]
    rks, sks = [], []
    for kk in range(TOP_K):
        hit = io_e == idxs[kk]
        rks.append(jnp.sum(jnp.where(hit, rank, 0.0), axis=0, keepdims=True))
        sks.append(jnp.sum(jnp.where(hit, s_t, 0.0), axis=0, keepdims=True))
    e_ref[...] = jnp.concatenate(idxs, axis=0)
    r_ref[...] = jnp.concatenate(rks, axis=0).astype(I32)
    w_ref[...] = jnp.concatenate(sks, axis=0) / den * ROUTE_SCALE
    carry_ref[...] += jnp.sum(sel, axis=1, keepdims=True)
    cnt_ref[...] = carry_ref[...]


def _route(scores, bias):
    tr = ROUTE_TR
    r = jnp.arange(tr)
    tri = (r[:, None] < r[None, :]).astype(BF16)
    kt = lambda dt: jax.ShapeDtypeStruct((TOP_K, TOKENS), dt)
    blk = pl.BlockSpec((TOP_K, tr), lambda i: (0, i))
    return pl.pallas_call(
        _route_kernel,
        grid=(TOKENS // tr,),
        in_specs=[pl.BlockSpec((tr, N_EXPERTS), lambda i: (i, 0)),
                  pl.BlockSpec((N_EXPERTS, 1), lambda i: (0, 0)),
                  pl.BlockSpec((tr, tr), lambda i: (0, 0))],
        out_specs=[blk, blk, blk, pl.BlockSpec((N_EXPERTS, 1), lambda i: (0, 0))],
        out_shape=[kt(I32), kt(I32), kt(F32), jax.ShapeDtypeStruct((N_EXPERTS, 1), F32)],
        scratch_shapes=[pltpu.VMEM((N_EXPERTS, 1), F32)],
        compiler_params=_params(("arbitrary",)),
        name="route",
    )(scores, bias.reshape(N_EXPERTS, 1), tri)


def _slots_kernel(e_ref, r_ref, ps_ref, d_ref):
    tr = ROUTE_TR
    io_e = lax.broadcasted_iota(I32, (N_EXPERTS, tr), 0)
    ps = ps_ref[...]
    rows = []
    for kk in range(TOP_K):
        hit = io_e == e_ref[kk:kk + 1, :]
        rows.append(jnp.sum(jnp.where(hit, ps, 0.0), axis=0, keepdims=True))
    d_ref[...] = jnp.concatenate(rows, axis=0).astype(I32) + r_ref[...]


def _slots(e_idx, rank, pad_start):
    tr = ROUTE_TR
    blk = pl.BlockSpec((TOP_K, tr), lambda i: (0, i))
    return pl.pallas_call(
        _slots_kernel,
        grid=(TOKENS // tr,),
        in_specs=[blk, blk, pl.BlockSpec((N_EXPERTS, 1), lambda i: (0, 0))],
        out_specs=blk,
        out_shape=jax.ShapeDtypeStruct((TOP_K, TOKENS), I32),
        compiler_params=_params(("parallel",)),
        name="slots",
    )(e_idx, rank, pad_start.astype(F32).reshape(N_EXPERTS, 1))


def _dispatch_kernel(ps_ref, cnt_ref, nb_ref, dest_ref, h_ref, xs_ref, zbuf, sem, zsem):
    tq = DISPATCH_TQ
    step = pl.program_id(0)
    n_used = ps_ref[N_EXPERTS - 1] // MOE_BLOCK + nb_ref[N_EXPERTS - 1]

    def zero_fill(act):
        def block(j, carry):
            r = pl.ds(pl.multiple_of(j * MOE_BLOCK, MOE_BLOCK), MOE_BLOCK)
            act(pltpu.make_async_copy(zbuf, xs_ref.at[r], zsem))
            return carry

        def expert(e, carry):
            @pl.when(cnt_ref[e] < nb_ref[e] * MOE_BLOCK)
            def _():
                block(ps_ref[e] // MOE_BLOCK + nb_ref[e] - 1, 0)
            return carry

        lax.fori_loop(0, N_EXPERTS, expert, 0)
        lax.fori_loop(n_used, N_BLOCKS, block, 0)

    @pl.when(step == 0)
    def _():
        zbuf[...] = jnp.zeros_like(zbuf)
        zero_fill(lambda cp: cp.start())
        zero_fill(lambda cp: cp.wait())

    def start(i, carry):
        for kk in range(TOP_K):
            d = dest_ref[i * TOP_K + kk]
            pltpu.make_async_copy(h_ref.at[i], xs_ref.at[d],
                                  sem).start(priority=kk % 2)
        return carry

    lax.fori_loop(0, tq, start, 0, unroll=2)
    for _ in range(TOP_K):
        pltpu.make_async_copy(h_ref, xs_ref.at[pl.ds(0, tq)], sem).wait()


def _dispatch(pad_start, counts, n_blk, dest, hp):
    tq = DISPATCH_TQ
    grid_spec = pltpu.PrefetchScalarGridSpec(
        num_scalar_prefetch=3,
        grid=(TOKENS // tq,),
        in_specs=[pl.BlockSpec((tq * TOP_K,), lambda i, *_: (i,), memory_space=pltpu.SMEM),
                  pl.BlockSpec((tq,) + TOKEN_TILE, lambda i, *_: (i, 0, 0))],
        out_specs=pl.BlockSpec(memory_space=pl.ANY),
        scratch_shapes=[pltpu.VMEM((MOE_BLOCK,) + TOKEN_TILE, F32),
                        pltpu.SemaphoreType.DMA, pltpu.SemaphoreType.DMA],
    )
    return pl.pallas_call(
        _dispatch_kernel,
        grid_spec=grid_spec,
        out_shape=jax.ShapeDtypeStruct((N_SLOTS,) + TOKEN_TILE, F32),
        compiler_params=_params(("arbitrary",)),
        name="dispatch",
    )(pad_start, counts, n_blk, dest, hp)


def _expert_kernel(ps_ref, nb_ref, wg_ref, wu_ref, wd_ref, xs_ref, ys_ref,
                   xbuf, ybuf, wgus, wds, xsem, ysem):
    e = pl.program_id(0)
    nb = nb_ref[e]
    first = ps_ref[e] // MOE_BLOCK
    n_used = ps_ref[N_EXPERTS - 1] // MOE_BLOCK + nb_ref[N_EXPERTS - 1]
    nbuf = EXPERT_NBUF

    def rows(g):
        return pl.ds(pl.multiple_of(g * MOE_BLOCK, MOE_BLOCK), MOE_BLOCK)

    def x_copy(g):
        slot = g % nbuf
        return pltpu.make_async_copy(xs_ref.at[rows(g)], xbuf.at[slot], xsem.at[slot])

    def y_copy(g):
        slot = g % nbuf
        return pltpu.make_async_copy(ybuf.at[slot], ys_ref.at[rows(g)], ysem.at[slot])

    look = nbuf - EXPERT_UNIT

    @pl.when(e == 0)
    def _():
        for g in range(look):
            @pl.when(g < n_used)
            def _():
                x_copy(g).start()

    def run(g, u):
        for d in range(u):
            @pl.when(g + look + d < n_used)
            def _():
                x_copy(g + look + d).start()

        parts = []
        for d in range(u):
            x_copy(g + d).wait()
            parts.append(xbuf[(g + d) % nbuf].reshape(MOE_BLOCK, D_MODEL))
        x = parts[0] if u == 1 else jnp.concatenate(parts, axis=0)
        gu = _dot(x.astype(BF16), wgus[...])
        gate = gu[:, :EXPERT_FF]
        mid = (gate * _sigmoid(gate) * gu[:, EXPERT_FF:]).astype(BF16)
        y = _dot(mid, wds[...])
        for d in range(u):
            b = g + d

            @pl.when(b >= nbuf)
            def _():
                y_copy(b - nbuf).wait()

            ybuf[b % nbuf] = y[d * MOE_BLOCK:(d + 1) * MOE_BLOCK].reshape(MOE_BLOCK, *TOKEN_TILE)
            y_copy(b).start()

    @pl.when(nb > 0)
    def _():
        wgus[:, :EXPERT_FF] = wg_ref[0].astype(BF16)
        wgus[:, EXPERT_FF:] = wu_ref[0].astype(BF16)
        wds[...] = wd_ref[0].astype(BF16)

        def unit(j, carry):
            run(first + j * EXPERT_UNIT, EXPERT_UNIT)
            return carry

        lax.fori_loop(0, nb // EXPERT_UNIT, unit, 0)
        for rem in range(1, EXPERT_UNIT):
            @pl.when(nb % EXPERT_UNIT == rem)
            def _():
                run(first + nb - rem, rem)

    @pl.when(e == N_EXPERTS - 1)
    def _():
        for d in range(nbuf):
            g = n_used - nbuf + d

            @pl.when(g >= 0)
            def _():
                y_copy(g).wait()

        ybuf[0] = jnp.zeros((MOE_BLOCK,) + TOKEN_TILE, F32)

        def z_copy(j):
            return pltpu.make_async_copy(ybuf.at[0], ys_ref.at[rows(j)], ysem.at[0])

        def z_start(j, carry):
            z_copy(j).start()
            return carry

        def z_wait(j, carry):
            z_copy(j).wait()
            return carry

        lax.fori_loop(n_used, N_BLOCKS, z_start, 0)
        lax.fori_loop(n_used, N_BLOCKS, z_wait, 0)


def _experts(pad_start, n_blk, xs, wg, wu, wd):
    wspec = lambda shape: pl.BlockSpec((1,) + shape, lambda e, ps, nb: (e, 0, 0))
    grid_spec = pltpu.PrefetchScalarGridSpec(
        num_scalar_prefetch=2,
        grid=(N_EXPERTS,),
        in_specs=[wspec((D_MODEL, EXPERT_FF)), wspec((D_MODEL, EXPERT_FF)),
                  wspec((EXPERT_FF, D_MODEL)), pl.BlockSpec(memory_space=pl.ANY)],
        out_specs=pl.BlockSpec(memory_space=pl.ANY),
        scratch_shapes=[pltpu.VMEM((EXPERT_NBUF, MOE_BLOCK) + TOKEN_TILE, F32),
                        pltpu.VMEM((EXPERT_NBUF, MOE_BLOCK) + TOKEN_TILE, F32),
                        pltpu.VMEM((D_MODEL, 2 * EXPERT_FF), BF16),
                        pltpu.VMEM((EXPERT_FF, D_MODEL), BF16),
                        pltpu.SemaphoreType.DMA((EXPERT_NBUF,)),
                        pltpu.SemaphoreType.DMA((EXPERT_NBUF,))],
    )
    return pl.pallas_call(
        _expert_kernel,
        grid_spec=grid_spec,
        out_shape=jax.ShapeDtypeStruct((N_SLOTS,) + TOKEN_TILE, F32),
        compiler_params=_params(("arbitrary",)),
        name="experts",
    )(pad_start, n_blk, wg, wu, wd, xs)


def _combine_kernel(dcur_ref, dnext_ref, w_ref, base_ref, g2_ref, fg_ref, ys_ref, o_ref,
                    buf, acc_ref, sem):
    tq = COMBINE_TQ
    i = pl.program_id(0)
    n = pl.num_programs(0)

    slot = i % 2

    def start(dest_ref, to_slot, t):
        for kk in range(TOP_K):
            d = dest_ref[t * TOP_K + kk]
            pltpu.make_async_copy(ys_ref.at[d], buf.at[to_slot, kk, t],
                                  sem.at[to_slot]).start(priority=kk % 2)

    def reduce(t):
        a = buf[slot, 0, t] * w_ref[t * TOP_K]
        for kk in range(1, TOP_K):
            a = a + buf[slot, kk, t] * w_ref[t * TOP_K + kk]
        acc_ref[t] = a

    @pl.when(i == 0)
    def _():
        lax.fori_loop(0, tq, lambda t, c: start(dcur_ref, 0, t) or c, 0, unroll=2)

    @pl.when(i + 1 < n)
    def _():
        lax.fori_loop(0, tq, lambda t, c: start(dnext_ref, 1 - slot, t) or c, 0, unroll=2)

    for kk in range(TOP_K):
        pltpu.make_async_copy(ys_ref.at[pl.ds(0, tq)], buf.at[slot, kk], sem.at[slot]).wait()
    lax.fori_loop(0, tq, lambda t, c: reduce(t) or c, 0, unroll=4)

    xo = base_ref[...] + g2_ref[0] * acc_ref[...]
    ms = jnp.sum(jnp.sum(xo * xo, axis=2, keepdims=True), axis=1, keepdims=True) * (1.0 / D_MODEL)
    o_ref[0] = ((xo * lax.rsqrt(ms + EPS)) * fg_ref[...]).reshape(tq, D_MODEL)


def _combine(dest, w_flat, base, g2, final_g, ys):
    tq = COMBINE_TQ
    nt = SEQ // tq
    n = TOKENS // tq
    smem = lambda imap: pl.BlockSpec((tq * TOP_K,), imap, memory_space=pltpu.SMEM)
    return pl.pallas_call(
        _combine_kernel,
        grid=(n,),
        in_specs=[smem(lambda i: (i,)), smem(lambda i: (jnp.minimum(i + 1, n - 1),)),
                  smem(lambda i: (i,)),
                  pl.BlockSpec((tq,) + TOKEN_TILE, lambda i: (i, 0, 0)),
                  pl.BlockSpec((1,) + TOKEN_TILE, lambda i: (i // nt, 0, 0)),
                  pl.BlockSpec(TOKEN_TILE, lambda i: (0, 0)),
                  pl.BlockSpec(memory_space=pl.ANY)],
        out_specs=pl.BlockSpec((1, tq, D_MODEL), lambda i: (i // nt, i % nt, 0)),
        out_shape=jax.ShapeDtypeStruct((BATCH, SEQ, D_MODEL), F32),
        scratch_shapes=[pltpu.VMEM((2, TOP_K, tq) + TOKEN_TILE, F32),
                        pltpu.VMEM((tq,) + TOKEN_TILE, F32),
                        pltpu.SemaphoreType.DMA((2,))],
        compiler_params=_params(("arbitrary",)),
        name="combine",
    )(dest, dest, w_flat, base, g2.reshape((BATCH,) + TOKEN_TILE), final_g.reshape(TOKEN_TILE), ys)


def kernel(x, c, ada_w, ada_b, norm1_g, w_in, s5_lam_re, s5_lam_im, s5_log_dt, s5_b_re, s5_b_im,
           s5_c_re, s5_c_im, s5_d, s5_w_glu, s5_b_glu, w_proj_a, gla_w_gk2, gla_b_gk2, gla_norm_g,
           w_proj_b, w_out, norm2_g, router_w, router_bias, exp_w_gate, exp_w_up, exp_w_down,
           sh_w_gate, sh_w_up, sh_w_down, final_g):
    l = 0
    mod = _ada(c, ada_w[l], ada_b[l])
    sh1, sc1, g1, sh2, sc2, g2 = [m.reshape(BATCH, 1, D_MODEL) for m in jnp.split(mod, 6, axis=-1)]

    w = w_in[l]
    gk0 = _V0 + GLA_VAL
    w_main = jnp.concatenate([w[:, :gk0], w[:, gk0 + GLA_GATE_RANK:]], axis=1).astype(BF16)
    w_gk = jnp.pad(w[:, gk0:gk0 + GLA_GATE_RANK], ((0, 0), (0, LANES - GLA_GATE_RANK))).astype(BF16)
    w_gk2 = jnp.pad(gla_w_gk2[l], ((0, LANES - GLA_GATE_RANK), (0, 0))).astype(BF16)
    u_tb, q, k, v, la, rs, ga, gb = _inproj(
        x, norm1_g[l].reshape(1, D_MODEL), sc1, sh1, w_main, w_gk, w_gk2,
        gla_b_gk2[l].reshape(1, GLA_KEY))
    tables = _s5_tables(s5_lam_re[l], s5_lam_im[l], s5_log_dt[l], s5_b_re[l], s5_b_im[l],
                        s5_c_re[l], s5_c_im[l], s5_d[l])
    ya_tb = _s5(u_tb, *tables)
    yb = _gla(q, k, la, v, rs, gla_norm_g[l])

    wsgu = jnp.concatenate([sh_w_gate[l], sh_w_up[l]], axis=1).astype(BF16)
    base, hp, scores = _merge(
        ya_tb, yb, ga, gb, x, g1, sc2, sh2, g2,
        norm2_g[l].reshape(1, D_MODEL), s5_w_glu[l].astype(BF16), s5_b_glu[l].reshape(1, S5_WIDTH),
        w_proj_a[l].astype(BF16), w_proj_b[l].astype(BF16), w_out[l].astype(BF16),
        router_w[l].astype(BF16), wsgu, sh_w_down[l].astype(BF16))

    e_idx, rank, w_k, cnt = _route(scores, router_bias[l])
    counts = cnt[:, 0].astype(I32)
    padded = (counts + MOE_BLOCK - 1) // MOE_BLOCK * MOE_BLOCK
    pad_end = jnp.cumsum(padded)
    pad_start = pad_end - padded
    dest = _slots(e_idx, rank, pad_start).T.reshape(-1)
    pad_start = pad_start.astype(I32)
    n_blk = (padded // MOE_BLOCK).astype(I32)
    xs = _dispatch(pad_start, counts, n_blk, dest, hp)
    ys = _experts(pad_start, n_blk, xs, exp_w_gate[l], exp_w_up[l], exp_w_down[l])
    return _combine(dest, w_k.T.reshape(-1), base, g2, final_g, ys)
```

```python
import functools

import jax
import jax.numpy as jnp
from jax import lax
from jax.experimental import pallas as pl
from jax.experimental.pallas import tpu as pltpu

F32 = jnp.float32
BF16 = jnp.bfloat16
I32 = jnp.int32

D_MODEL = 1024
BATCH = 8
SEQ = 2048
TOKENS = BATCH * SEQ
S5_WIDTH = 512
S5_GROUP = 16
S5_GROUPS = 32
S5_STATE = 64
S5_COLS = S5_GROUPS * S5_STATE
GLA_HEADS = 4
GLA_DK = 64
GLA_DV = 128
GLA_KEY = GLA_HEADS * GLA_DK
GLA_VAL = GLA_HEADS * GLA_DV
GLA_GATE_RANK = 16
GLA_GATE_TAU = 16.0
GLA_CHUNK = 64
GLA_SUB = 16
N_EXPERTS = 256
TOP_K = 8
N_GROUPS = 8
GROUP_SIZE = N_EXPERTS // N_GROUPS
TOPK_GROUPS = 4
EXPERT_FF = 256
ROUTE_SCALE = 2.5
MOE_BLOCK = 128
EPS = 1e-6
N_SLOTS = -(-(TOKENS * TOP_K + N_EXPERTS * (MOE_BLOCK - 1)) // MOE_BLOCK) * MOE_BLOCK
N_BLOCKS = N_SLOTS // MOE_BLOCK

LANES = 128
SUBLANES = 8
TOKEN_TILE = (SUBLANES, LANES)
assert D_MODEL == SUBLANES * LANES
VMEM_LIMIT = 56 * 1024 * 1024

ADA_TN = 1536
INPROJ_TM = 512
S5_TC = 64
S5_COL_CHUNK = 512
GLA_TG = 256
MERGE_TM = 512
ROUTE_TR = 512
DISPATCH_TQ = 1024
COMBINE_TQ = 256
EXPERT_UNIT = 2
EXPERT_NBUF = 6
EXP_CLAMP = 60.0


def _params(sem, vmem=VMEM_LIMIT):
    return pltpu.CompilerParams(dimension_semantics=sem, vmem_limit_bytes=vmem)


def _dot(a, b):
    return jnp.dot(a, b, preferred_element_type=F32)


def _dot_nt(a, b):
    return lax.dot_general(a, b, (((1,), (1,)), ((), ())), preferred_element_type=F32)


def _sigmoid(x):
    return jax.nn.sigmoid(x)


def _ada_kernel(c_ref, w_ref, b_ref, o_ref):
    c = c_ref[...]
    s = (c * _sigmoid(c)).astype(BF16)
    o_ref[...] = _dot(s, w_ref[...].astype(BF16)) + b_ref[...]


def _ada(c, w, b):
    n = w.shape[1]
    return pl.pallas_call(
        _ada_kernel,
        grid=(n // ADA_TN,),
        in_specs=[
            pl.BlockSpec((BATCH, D_MODEL), lambda j: (0, 0)),
            pl.BlockSpec((D_MODEL, ADA_TN), lambda j: (0, j)),
            pl.BlockSpec((1, ADA_TN), lambda j: (0, j)),
        ],
        out_specs=pl.BlockSpec((BATCH, ADA_TN), lambda j: (0, j)),
        out_shape=jax.ShapeDtypeStruct((BATCH, n), F32),
        compiler_params=_params(("arbitrary",)),
        name="ada",
    )(c, w, b.reshape(1, n))


_U0, _Q0, _K0, _V0, _R0, _GA0, _GB0, _END = 0, 512, 768, 1024, 1536, 2048, 3072, 4096


def _inproj_kernel(x_ref, g_ref, sc_ref, sh_ref, wm_ref, wgk_ref, wgk2_ref, bgk_ref,
                   u_ref, q_ref, k_ref, v_ref, la_ref, rs_ref, ga_ref, gb_ref):
    x = x_ref[0]
    r = lax.rsqrt(jnp.mean(x * x, axis=-1, keepdims=True) + EPS)
    h = (x * r) * g_ref[...] * (1.0 + sc_ref[0]) + sh_ref[0]
    hb = h.astype(BF16)

    def seg(lo, hi):
        return _dot(hb, wm_ref[:, lo:hi])

    u_ref[...] = seg(_U0, _Q0)
    q_ref[0] = seg(_Q0, _K0) * (GLA_DK ** -0.5)
    k_ref[0] = seg(_K0, _V0)
    v_ref[0] = seg(_V0, _R0)
    rr = seg(_R0, _GA0)
    rs_ref[0] = rr * _sigmoid(rr)
    ga_ref[0] = _sigmoid(seg(_GA0, _GB0))
    gb_ref[0] = _sigmoid(seg(_GB0, _END))
    gk = _dot(hb, wgk_ref[...])
    z = _dot(gk.astype(BF16), wgk2_ref[...]) + bgk_ref[...]
    la_ref[0] = -(jnp.maximum(-z, 0.0) + jnp.log1p(jnp.exp(-jnp.abs(z)))) * (1.0 / GLA_GATE_TAU)


def _inproj(x, g, sc, sh, w_main, w_gk, w_gk2, b_gk2):
    tm = INPROJ_TM
    nt = SEQ // tm
    row3 = lambda w: pl.BlockSpec((1, tm, w), lambda b, i: (b, i, 0))
    const = lambda shape: pl.BlockSpec(shape, lambda b, i: tuple(0 for _ in shape))
    mod = pl.BlockSpec((1, 1, D_MODEL), lambda b, i: (b, 0, 0))
    bld = lambda w: jax.ShapeDtypeStruct((BATCH, SEQ, w), F32)
    return pl.pallas_call(
        _inproj_kernel,
        grid=(BATCH, nt),
        in_specs=[row3(D_MODEL), const((1, D_MODEL)), mod, mod,
                  const((D_MODEL, _END)), const((D_MODEL, LANES)), const((LANES, GLA_KEY)),
                  const((1, GLA_KEY))],
        out_specs=[pl.BlockSpec((tm, S5_WIDTH), lambda b, i: (i, b)),
                   row3(GLA_KEY), row3(GLA_KEY), row3(GLA_VAL), row3(GLA_KEY), row3(GLA_VAL),
                   row3(D_MODEL), row3(D_MODEL)],
        out_shape=[jax.ShapeDtypeStruct((SEQ, BATCH * S5_WIDTH), F32),
                   bld(GLA_KEY), bld(GLA_KEY), bld(GLA_VAL), bld(GLA_KEY), bld(GLA_VAL),
                   bld(D_MODEL), bld(D_MODEL)],
        compiler_params=_params(("parallel", "parallel")),
        name="inproj",
    )(x, g, sc, sh, w_main, w_gk, w_gk2, b_gk2)


def _s5_kernel(u_ref, bre_ref, bim_ref, cre_ref, cim_ref, are_ref, aim_ref, d_ref,
               y_ref, s_ref, st_ref):
    half = S5_COLS // 2

    @pl.when(pl.program_id(0) == 0)
    def _():
        st_ref[...] = jnp.zeros_like(st_ref)

    u = u_ref[...].reshape(S5_TC, BATCH, S5_WIDTH).reshape(S5_TC * BATCH, S5_WIDTH)
    ub = u.astype(BF16)
    for j in range(2):
        uj = ub[:, j * 256:(j + 1) * 256]
        s_ref[:, j * half:(j + 1) * half] = _dot(uj, bre_ref[j])
        s_ref[:, S5_COLS + j * half:S5_COLS + (j + 1) * half] = _dot(uj, bim_ref[j])

    for cc in range(S5_COLS // S5_COL_CHUNK):
        re_sl = slice(cc * S5_COL_CHUNK, (cc + 1) * S5_COL_CHUNK)
        im_sl = slice(S5_COLS + cc * S5_COL_CHUNK, S5_COLS + (cc + 1) * S5_COL_CHUNK)
        ar = are_ref[:, re_sl]
        ai = aim_ref[:, re_sl]

        def body(t, carry, re_sl=re_sl, im_sl=im_sl, ar=ar, ai=ai):
            sr, si = carry
            rows = pl.ds(pl.multiple_of(t * BATCH, BATCH), BATCH)
            nr = ar * sr - ai * si + s_ref[rows, re_sl]
            ni = ar * si + ai * sr + s_ref[rows, im_sl]
            s_ref[rows, re_sl] = nr
            s_ref[rows, im_sl] = ni
            return nr, ni

        sr, si = lax.fori_loop(0, S5_TC, body, (st_ref[:, re_sl], st_ref[:, im_sl]), unroll=4)
        st_ref[:, re_sl] = sr
        st_ref[:, im_sl] = si

    ys = []
    for j in range(2):
        sre = s_ref[:, j * half:(j + 1) * half].astype(BF16)
        sim = s_ref[:, S5_COLS + j * half:S5_COLS + (j + 1) * half].astype(BF16)
        ys.append(_dot(sre, cre_ref[j]) + _dot(sim, cim_ref[j]))
    y = jnp.concatenate(ys, axis=1) + d_ref[...] * u
    y_ref[...] = jax.nn.gelu(y).reshape(S5_TC, BATCH, S5_WIDTH).reshape(S5_TC, BATCH * S5_WIDTH)


def _s5(u_tb, bre, bim, cre, cimn, are, aim, dflat):
    rows = S5_TC * BATCH
    const = lambda shape: pl.BlockSpec(shape, lambda i: tuple(0 for _ in shape))
    return pl.pallas_call(
        _s5_kernel,
        grid=(SEQ // S5_TC,),
        in_specs=[pl.BlockSpec((S5_TC, BATCH * S5_WIDTH), lambda i: (i, 0)),
                  const(bre.shape), const(bim.shape), const(cre.shape), const(cimn.shape),
                  const(are.shape), const(aim.shape), const(dflat.shape)],
        out_specs=pl.BlockSpec((S5_TC, BATCH * S5_WIDTH), lambda i: (i, 0)),
        out_shape=jax.ShapeDtypeStruct((SEQ, BATCH * S5_WIDTH), F32),
        scratch_shapes=[pltpu.VMEM((rows, 2 * S5_COLS), F32),
                        pltpu.VMEM((BATCH, 2 * S5_COLS), F32)],
        compiler_params=_params(("arbitrary",)),
        name="s5",
    )(u_tb, bre, bim, cre, cimn, are, aim, dflat)


def _s5_tables(lam_re, lam_im, log_dt, b_re, b_im, c_re, c_im, d_skip):
    lr = lam_re.astype(F32)
    li = lam_im.astype(F32)
    dt = jnp.exp(log_dt.astype(F32))[:, None]
    mag = jnp.exp(lr * dt)
    abar_re = mag * jnp.cos(li * dt)
    abar_im = mag * jnp.sin(li * dt)
    den = lr * lr + li * li
    num_re = abar_re - 1.0
    coef_re = (num_re * lr + abar_im * li) / den
    coef_im = (abar_im * lr - num_re * li) / den
    cr, ci = coef_re[..., None], coef_im[..., None]
    br, bi = b_re.astype(F32), b_im.astype(F32)
    bbar_re = cr * br - ci * bi
    bbar_im = cr * bi + ci * br
    eye = jnp.eye(S5_GROUPS // 2, dtype=F32)

    def in_map(bb):
        bb = bb.reshape(2, S5_GROUPS // 2, S5_STATE, S5_GROUP)
        return jnp.einsum('jgph,gk->jghkp', bb, eye).reshape(2, 256, S5_COLS // 2).astype(BF16)

    def out_map(cc):
        cc = cc.reshape(2, S5_GROUPS // 2, S5_GROUP, S5_STATE)
        return jnp.einsum('jghp,gk->jgpkh', cc, eye).reshape(2, S5_COLS // 2, 256).astype(BF16)

    are = jnp.broadcast_to(abar_re.reshape(1, S5_COLS), (BATCH, S5_COLS))
    aim = jnp.broadcast_to(abar_im.reshape(1, S5_COLS), (BATCH, S5_COLS))
    return (in_map(bbar_re), in_map(bbar_im), out_map(c_re.astype(F32)), out_map(-c_im.astype(F32)),
            are, aim, d_skip.astype(F32).reshape(1, S5_WIDTH))


def _gla_kernel(q_ref, k_ref, la_ref, v_ref, rs_ref, ng_ref, tt_ref, o_ref, st_ref):
    @pl.when(pl.program_id(1) == 0)
    def _():
        st_ref[...] = jnp.zeros_like(st_ref)

    c = GLA_CHUNK
    tt = tt_ref[...]
    row = lax.broadcasted_iota(I32, (c, GLA_KEY), 0)
    ri = lax.broadcasted_iota(I32, (c, c), 0)
    ci = lax.broadcasted_iota(I32, (c, c), 1)
    causal = ci <= ri
    nsub = c // GLA_SUB

    for ch in range(GLA_TG // c):
        sl = slice(ch * c, (ch + 1) * c)
        g = la_ref[0, sl, :]
        g1 = g.astype(BF16)
        r1 = g - g1.astype(F32)
        g2 = r1.astype(BF16)
        g3 = (r1 - g2.astype(F32)).astype(BF16)
        cs = _dot(tt, g1) + _dot(tt, g2) + _dot(tt, g3)
        b = cs[0:c]
        cl = cs[c:2 * c]
        ref_pt = b - cl
        q = q_ref[0, sl, :]
        k = k_ref[0, sl, :]
        qt = q * jnp.exp(cl)
        qe = q * jnp.exp(b)
        blast = b[c - 1:c, :]
        ks = k * jnp.exp(blast - b)
        k_sub = []
        q_sub = []
        for s in range(nsub):
            rs_ = ref_pt[s * GLA_SUB:s * GLA_SUB + 1, :]
            k_sub.append(k * jnp.exp(jnp.minimum(rs_ - b, EXP_CLAMP)))
            q_sub.append(jnp.where((row >= s * GLA_SUB) & (row < (s + 1) * GLA_SUB), qt, 0.0))
        eblast = jnp.exp(blast)
        for h in range(GLA_HEADS):
            hs = slice(h * GLA_DK, (h + 1) * GLA_DK)
            vs = slice(h * GLA_DV, (h + 1) * GLA_DV)
            qbig = jnp.concatenate([x[:, hs] for x in q_sub], axis=1).astype(BF16)
            kbig = jnp.concatenate([x[:, hs] for x in k_sub], axis=1).astype(BF16)
            sc = jnp.where(causal, _dot_nt(qbig, kbig), 0.0)
            vh = v_ref[0, sl, vs]
            vb = vh.astype(BF16)
            st = st_ref[h]
            o = _dot_nt(qe[:, hs].astype(BF16), st.astype(BF16)) + _dot(sc.astype(BF16), vb)
            st_ref[h] = st * eblast[:, hs] + _dot(vh.T.astype(BF16), ks[:, hs].astype(BF16))
            on = o * lax.rsqrt(jnp.mean(o * o, axis=-1, keepdims=True) + EPS) * ng_ref[...]
            o_ref[0, sl, vs] = on * rs_ref[0, sl, vs]


def _gla(q, k, la, v, rs, norm_g):
    tg = GLA_TG
    c = GLA_CHUNK
    r = jnp.arange(c)
    tri = (r[None, :] <= r[:, None])
    blk = tri & ((r[None, :] // GLA_SUB) == (r[:, None] // GLA_SUB))
    tt = jnp.concatenate([tri, blk], axis=0).astype(BF16)
    row3 = lambda w: pl.BlockSpec((1, tg, w), lambda b, i: (b, i, 0))
    const = lambda shape: pl.BlockSpec(shape, lambda b, i: tuple(0 for _ in shape))
    return pl.pallas_call(
        _gla_kernel,
        grid=(BATCH, SEQ // tg),
        in_specs=[row3(GLA_KEY), row3(GLA_KEY), row3(GLA_KEY), row3(GLA_VAL), row3(GLA_VAL),
                  const((1, GLA_DV)), const((2 * c, c))],
        out_specs=row3(GLA_VAL),
        out_shape=jax.ShapeDtypeStruct((BATCH, SEQ, GLA_VAL), F32),
        scratch_shapes=[pltpu.VMEM((GLA_HEADS, GLA_DV, GLA_DK), F32)],
        compiler_params=_params(("parallel", "arbitrary")),
        name="gla",
    )(q, k, la, v, rs, norm_g.reshape(1, GLA_DV), tt)


def _merge_kernel(ya_ref, yb_ref, ga_ref, gb_ref, x_ref, g1_ref, sc2_ref, sh2_ref, g2_ref, n2_ref,
                  wglu_ref, bglu_ref, wpa_ref, wpb_ref, wout_ref, wr_ref, wsgu_ref, wsd_ref,
                  base_ref, hp_ref, sc_ref):
    ya = ya_ref[...]
    z = _dot(ya.astype(BF16), wglu_ref[...]) + bglu_ref[...]
    ya2 = ya * _sigmoid(z)
    mixed = (ga_ref[0] * _dot(ya2.astype(BF16), wpa_ref[...])
             + gb_ref[0] * _dot(yb_ref[0].astype(BF16), wpb_ref[...]))
    x1 = x_ref[0] + g1_ref[0] * _dot(mixed.astype(BF16), wout_ref[...])
    r = lax.rsqrt(jnp.mean(x1 * x1, axis=-1, keepdims=True) + EPS)
    h = (x1 * r) * n2_ref[...] * (1.0 + sc2_ref[0]) + sh2_ref[0]
    hb = h.astype(BF16)
    sc_ref[...] = _sigmoid(_dot(hb, wr_ref[...]))
    gu = _dot(hb, wsgu_ref[...])
    gate = gu[:, :EXPERT_FF]
    mid = (gate * _sigmoid(gate) * gu[:, EXPERT_FF:]).astype(BF16)
    base_ref[...] = (x1 + g2_ref[0] * _dot(mid, wsd_ref[...])).reshape(MERGE_TM, *TOKEN_TILE)
    hp_ref[...] = h.reshape(MERGE_TM, *TOKEN_TILE)


def _merge(ya_tb, yb, ga, gb, x, g1, sc2, sh2, g2, n2, wglu, bglu, wpa, wpb, wout, wr, wsgu, wsd):
    tm = MERGE_TM
    row3 = lambda w: pl.BlockSpec((1, tm, w), lambda b, i: (b, i, 0))
    const = lambda a: pl.BlockSpec(a.shape, lambda b, i: tuple(0 for _ in a.shape))
    mod = pl.BlockSpec((1, 1, D_MODEL), lambda b, i: (b, 0, 0))
    nt = SEQ // tm
    flat = lambda w: pl.BlockSpec((tm, w), lambda b, i: (b * nt + i, 0))
    tiles = pl.BlockSpec((tm,) + TOKEN_TILE, lambda b, i: (b * nt + i, 0, 0))
    return pl.pallas_call(
        _merge_kernel,
        grid=(BATCH, nt),
        in_specs=[pl.BlockSpec((tm, S5_WIDTH), lambda b, i: (i, b)),
                  row3(GLA_VAL), row3(D_MODEL), row3(D_MODEL), row3(D_MODEL),
                  mod, mod, mod, mod, const(n2),
                  const(wglu), const(bglu), const(wpa), const(wpb), const(wout), const(wr),
                  const(wsgu), const(wsd)],
        out_specs=[tiles, tiles, flat(N_EXPERTS)],
        out_shape=[jax.ShapeDtypeStruct((TOKENS,) + TOKEN_TILE, F32),
                   jax.ShapeDtypeStruct((TOKENS,) + TOKEN_TILE, F32),
                   jax.ShapeDtypeStruct((TOKENS, N_EXPERTS), F32)],
        compiler_params=_params(("parallel", "parallel")),
        name="merge",
    )(ya_tb, yb, ga, gb, x, g1, sc2, sh2, g2, n2, wglu, bglu, wpa, wpb, wout, wr, wsgu, wsd)


def _first_argmax(x, iota, size):
    m = jnp.max(x, axis=0, keepdims=True)
    idx = jnp.min(jnp.where(x == m, iota, size), axis=0, keepdims=True)
    return m, idx


def _route_kernel(s_ref, bias_ref, tri_ref, e_ref, r_ref, w_ref, cnt_ref, carry_ref):
    @pl.when(pl.program_id(0) == 0)
    def _():
        carry_ref[...] = jnp.zeros_like(carry_ref)

    tr = ROUTE_TR
    neg = -jnp.inf
    s_t = s_ref[...].T
    biased = s_t + bias_ref[...]
    io_g = lax.broadcasted_iota(I32, (GROUP_SIZE, tr), 0)
    rows = []
    for g in range(N_GROUPS):
        xg = biased[g * GROUP_SIZE:(g + 1) * GROUP_SIZE, :]
        m1, i1 = _first_argmax(xg, io_g, GROUP_SIZE)
        m2 = jnp.max(jnp.where(io_g == i1, neg, xg), axis=0, keepdims=True)
        rows.append(m1 + m2)
    gs = jnp.concatenate(rows, axis=0)
    io_n = lax.broadcasted_iota(I32, (N_GROUPS, tr), 0)
    gsel = jnp.zeros((N_GROUPS, tr), F32)
    for _ in range(TOPK_GROUPS):
        _, gi = _first_argmax(gs, io_n, N_GROUPS)
        hit = io_n == gi
        gsel = jnp.where(hit, 1.0, gsel)
        gs = jnp.where(hit, neg, gs)
    masked = jnp.concatenate(
        [jnp.where(gsel[g:g + 1, :] > 0.0, biased[g * GROUP_SIZE:(g + 1) * GROUP_SIZE, :], neg)
         for g in range(N_GROUPS)], axis=0)
    io_e = lax.broadcasted_iota(I32, (N_EXPERTS, tr), 0)
    sel = jnp.zeros((N_EXPERTS, tr), F32)
    idxs = []
    for _ in range(TOP_K):
        _, ei = _first_argmax(masked, io_e, N_EXPERTS)
        hit = io_e == ei
        sel = jnp.where(hit, 1.0, sel)
        masked = jnp.where(hit, neg, masked)
        idxs.append(ei)
    den = jnp.sum(sel * s_t, axis=0, keepdims=True)
    rank = _dot(sel.astype(BF16), tri_ref[...]) + carry_ref[...]
    rks, sks = [], []
    for kk in range(TOP_K):
        hit = io_e == idxs[kk]
        rks.append(jnp.sum(jnp.where(hit, rank, 0.0), axis=0, keepdims=True))
        sks.append(jnp.sum(jnp.where(hit, s_t, 0.0), axis=0, keepdims=True))
    e_ref[...] = jnp.concatenate(idxs, axis=0)
    r_ref[...] = jnp.concatenate(rks, axis=0).astype(I32)
    w_ref[...] = jnp.concatenate(sks, axis=0) / den * ROUTE_SCALE
    carry_ref[...] += jnp.sum(sel, axis=1, keepdims=True)
    cnt_ref[...] = carry_ref[...]


def _route(scores, bias):
    tr = ROUTE_TR
    r = jnp.arange(tr)
    tri = (r[:, None] < r[None, :]).astype(BF16)
    kt = lambda dt: jax.ShapeDtypeStruct((TOP_K, TOKENS), dt)
    blk = pl.BlockSpec((TOP_K, tr), lambda i: (0, i))
    return pl.pallas_call(
        _route_kernel,
        grid=(TOKENS // tr,),
        in_specs=[pl.BlockSpec((tr, N_EXPERTS), lambda i: (i, 0)),
                  pl.BlockSpec((N_EXPERTS, 1), lambda i: (0, 0)),
                  pl.BlockSpec((tr, tr), lambda i: (0, 0))],
        out_specs=[blk, blk, blk, pl.BlockSpec((N_EXPERTS, 1), lambda i: (0, 0))],
        out_shape=[kt(I32), kt(I32), kt(F32), jax.ShapeDtypeStruct((N_EXPERTS, 1), F32)],
        scratch_shapes=[pltpu.VMEM((N_EXPERTS, 1), F32)],
        compiler_params=_params(("arbitrary",)),
        name="route",
    )(scores, bias.reshape(N_EXPERTS, 1), tri)


def _slots_kernel(e_ref, r_ref, ps_ref, d_ref):
    tr = ROUTE_TR
    io_e = lax.broadcasted_iota(I32, (N_EXPERTS, tr), 0)
    ps = ps_ref[...]
    rows = []
    for kk in range(TOP_K):
        hit = io_e == e_ref[kk:kk + 1, :]
        rows.append(jnp.sum(jnp.where(hit, ps, 0.0), axis=0, keepdims=True))
    d_ref[...] = jnp.concatenate(rows, axis=0).astype(I32) + r_ref[...]


def _slots(e_idx, rank, pad_start):
    tr = ROUTE_TR
    blk = pl.BlockSpec((TOP_K, tr), lambda i: (0, i))
    return pl.pallas_call(
        _slots_kernel,
        grid=(TOKENS // tr,),
        in_specs=[blk, blk, pl.BlockSpec((N_EXPERTS, 1), lambda i: (0, 0))],
        out_specs=blk,
        out_shape=jax.ShapeDtypeStruct((TOP_K, TOKENS), I32),
        compiler_params=_params(("parallel",)),
        name="slots",
    )(e_idx, rank, pad_start.astype(F32).reshape(N_EXPERTS, 1))


def _dispatch_kernel(ps_ref, cnt_ref, nb_ref, dest_ref, h_ref, xs_ref, zbuf, sem, zsem):
    tq = DISPATCH_TQ
    step = pl.program_id(0)
    n_used = ps_ref[N_EXPERTS - 1] // MOE_BLOCK + nb_ref[N_EXPERTS - 1]

    def zero_fill(act):
        def block(j, carry):
            r = pl.ds(pl.multiple_of(j * MOE_BLOCK, MOE_BLOCK), MOE_BLOCK)
            act(pltpu.make_async_copy(zbuf, xs_ref.at[r], zsem))
            return carry

        def expert(e, carry):
            @pl.when(cnt_ref[e] < nb_ref[e] * MOE_BLOCK)
            def _():
                block(ps_ref[e] // MOE_BLOCK + nb_ref[e] - 1, 0)
            return carry

        lax.fori_loop(0, N_EXPERTS, expert, 0)
        lax.fori_loop(n_used, N_BLOCKS, block, 0)

    @pl.when(step == 0)
    def _():
        zbuf[...] = jnp.zeros_like(zbuf)
        zero_fill(lambda cp: cp.start())
        zero_fill(lambda cp: cp.wait())

    def start(i, carry):
        for kk in range(TOP_K):
            d = dest_ref[i * TOP_K + kk]
            pltpu.make_async_copy(h_ref.at[i], xs_ref.at[d],
                                  sem).start(priority=kk % 2)
        return carry

    lax.fori_loop(0, tq, start, 0, unroll=2)
    for _ in range(TOP_K):
        pltpu.make_async_copy(h_ref, xs_ref.at[pl.ds(0, tq)], sem).wait()


def _dispatch(pad_start, counts, n_blk, dest, hp):
    tq = DISPATCH_TQ
    grid_spec = pltpu.PrefetchScalarGridSpec(
        num_scalar_prefetch=3,
        grid=(TOKENS // tq,),
        in_specs=[pl.BlockSpec((tq * TOP_K,), lambda i, *_: (i,), memory_space=pltpu.SMEM),
                  pl.BlockSpec((tq,) + TOKEN_TILE, lambda i, *_: (i, 0, 0))],
        out_specs=pl.BlockSpec(memory_space=pl.ANY),
        scratch_shapes=[pltpu.VMEM((MOE_BLOCK,) + TOKEN_TILE, F32),
                        pltpu.SemaphoreType.DMA, pltpu.SemaphoreType.DMA],
    )
    return pl.pallas_call(
        _dispatch_kernel,
        grid_spec=grid_spec,
        out_shape=jax.ShapeDtypeStruct((N_SLOTS,) + TOKEN_TILE, F32),
        compiler_params=_params(("arbitrary",)),
        name="dispatch",
    )(pad_start, counts, n_blk, dest, hp)


def _expert_kernel(ps_ref, nb_ref, wg_ref, wu_ref, wd_ref, xs_ref, ys_ref,
                   xbuf, ybuf, wgus, wds, xsem, ysem):
    e = pl.program_id(0)
    nb = nb_ref[e]
    first = ps_ref[e] // MOE_BLOCK
    n_used = ps_ref[N_EXPERTS - 1] // MOE_BLOCK + nb_ref[N_EXPERTS - 1]
    nbuf = EXPERT_NBUF

    def rows(g):
        return pl.ds(pl.multiple_of(g * MOE_BLOCK, MOE_BLOCK), MOE_BLOCK)

    def x_copy(g):
        slot = g % nbuf
        return pltpu.make_async_copy(xs_ref.at[rows(g)], xbuf.at[slot], xsem.at[slot])

    def y_copy(g):
        slot = g % nbuf
        return pltpu.make_async_copy(ybuf.at[slot], ys_ref.at[rows(g)], ysem.at[slot])

    look = nbuf - EXPERT_UNIT

    @pl.when(e == 0)
    def _():
        for g in range(look):
            @pl.when(g < n_used)
            def _():
                x_copy(g).start()

    def run(g, u):
        for d in range(u):
            @pl.when(g + look + d < n_used)
            def _():
                x_copy(g + look + d).start()

        parts = []
        for d in range(u):
            x_copy(g + d).wait()
            parts.append(xbuf[(g + d) % nbuf].reshape(MOE_BLOCK, D_MODEL))
        x = parts[0] if u == 1 else jnp.concatenate(parts, axis=0)
        gu = _dot(x.astype(BF16), wgus[...])
        gate = gu[:, :EXPERT_FF]
        mid = (gate * _sigmoid(gate) * gu[:, EXPERT_FF:]).astype(BF16)
        y = _dot(mid, wds[...])
        for d in range(u):
            b = g + d

            @pl.when(b >= nbuf)
            def _():
                y_copy(b - nbuf).wait()

            ybuf[b % nbuf] = y[d * MOE_BLOCK:(d + 1) * MOE_BLOCK].reshape(MOE_BLOCK, *TOKEN_TILE)
            y_copy(b).start()

    @pl.when(nb > 0)
    def _():
        wgus[:, :EXPERT_FF] = wg_ref[0].astype(BF16)
        wgus[:, EXPERT_FF:] = wu_ref[0].astype(BF16)
        wds[...] = wd_ref[0].astype(BF16)

        def unit(j, carry):
            run(first + j * EXPERT_UNIT, EXPERT_UNIT)
            return carry

        lax.fori_loop(0, nb // EXPERT_UNIT, unit, 0)
        for rem in range(1, EXPERT_UNIT):
            @pl.when(nb % EXPERT_UNIT == rem)
            def _():
                run(first + nb - rem, rem)

    @pl.when(e == N_EXPERTS - 1)
    def _():
        for d in range(nbuf):
            g = n_used - nbuf + d

            @pl.when(g >= 0)
            def _():
                y_copy(g).wait()

        ybuf[0] = jnp.zeros((MOE_BLOCK,) + TOKEN_TILE, F32)

        def z_copy(j):
            return pltpu.make_async_copy(ybuf.at[0], ys_ref.at[rows(j)], ysem.at[0])

        def z_start(j, carry):
            z_copy(j).start()
            return carry

        def z_wait(j, carry):
            z_copy(j).wait()
            return carry

        lax.fori_loop(n_used, N_BLOCKS, z_start, 0)
        lax.fori_loop(n_used, N_BLOCKS, z_wait, 0)


def _experts(pad_start, n_blk, xs, wg, wu, wd):
    wspec = lambda shape: pl.BlockSpec((1,) + shape, lambda e, ps, nb: (e, 0, 0))
    grid_spec = pltpu.PrefetchScalarGridSpec(
        num_scalar_prefetch=2,
        grid=(N_EXPERTS,),
        in_specs=[wspec((D_MODEL, EXPERT_FF)), wspec((D_MODEL, EXPERT_FF)),
                  wspec((EXPERT_FF, D_MODEL)), pl.BlockSpec(memory_space=pl.ANY)],
        out_specs=pl.BlockSpec(memory_space=pl.ANY),
        scratch_shapes=[pltpu.VMEM((EXPERT_NBUF, MOE_BLOCK) + TOKEN_TILE, F32),
                        pltpu.VMEM((EXPERT_NBUF, MOE_BLOCK) + TOKEN_TILE, F32),
                        pltpu.VMEM((D_MODEL, 2 * EXPERT_FF), BF16),
                        pltpu.VMEM((EXPERT_FF, D_MODEL), BF16),
                        pltpu.SemaphoreType.DMA((EXPERT_NBUF,)),
                        pltpu.SemaphoreType.DMA((EXPERT_NBUF,))],
    )
    return pl.pallas_call(
        _expert_kernel,
        grid_spec=grid_spec,
        out_shape=jax.ShapeDtypeStruct((N_SLOTS,) + TOKEN_TILE, F32),
        compiler_params=_params(("arbitrary",)),
        name="experts",
    )(pad_start, n_blk, wg, wu, wd, xs)


def _combine_kernel(dcur_ref, dnext_ref, w_ref, base_ref, g2_ref, fg_ref, ys_ref, o_ref,
                    buf, acc_ref, sem):
    tq = COMBINE_TQ
    i = pl.program_id(0)
    n = pl.num_programs(0)

    def start(dest_ref, to_slot, t):
        for kk in range(TOP_K):
            d = dest_ref[t * TOP_K + kk]
            pltpu.make_async_copy(ys_ref.at[d], buf.at[to_slot, kk, t],
                                  sem.at[to_slot]).start(priority=kk % 2)

    def reduce(slot, t):
        a = buf[slot, 0, t] * w_ref[t * TOP_K]
        for kk in range(1, TOP_K):
            a = a + buf[slot, kk, t] * w_ref[t * TOP_K + kk]
        acc_ref[t] = a

    @pl.when(i == 0)
    def _():
        lax.fori_loop(0, tq, lambda t, c: start(dcur_ref, 0, t) or c, 0, unroll=4)

    for slot in range(2):
        @pl.when(i % 2 == slot)
        def _(slot=slot):
            @pl.when(i + 1 < n)
            def _():
                lax.fori_loop(0, tq, lambda t, c: start(dnext_ref, 1 - slot, t) or c, 0, unroll=4)

            for kk in range(TOP_K):
                pltpu.make_async_copy(ys_ref.at[pl.ds(0, tq)], buf.at[slot, kk],
                                      sem.at[slot]).wait()
            lax.fori_loop(0, tq, lambda t, c: reduce(slot, t) or c, 0, unroll=8)

    xo = base_ref[...] + g2_ref[0] * acc_ref[...]
    ms = jnp.sum(jnp.sum(xo * xo, axis=2, keepdims=True), axis=1, keepdims=True) * (1.0 / D_MODEL)
    o_ref[0] = ((xo * lax.rsqrt(ms + EPS)) * fg_ref[...]).reshape(tq, D_MODEL)


def _combine(dest, w_flat, base, g2, final_g, ys):
    tq = COMBINE_TQ
    nt = SEQ // tq
    n = TOKENS // tq
    smem = lambda imap: pl.BlockSpec((tq * TOP_K,), imap, memory_space=pltpu.SMEM)
    return pl.pallas_call(
        _combine_kernel,
        grid=(n,),
        in_specs=[smem(lambda i: (i,)), smem(lambda i: (jnp.minimum(i + 1, n - 1),)),
                  smem(lambda i: (i,)),
                  pl.BlockSpec((tq,) + TOKEN_TILE, lambda i: (i, 0, 0)),
                  pl.BlockSpec((1,) + TOKEN_TILE, lambda i: (i // nt, 0, 0)),
                  pl.BlockSpec(TOKEN_TILE, lambda i: (0, 0)),
                  pl.BlockSpec(memory_space=pl.ANY)],
        out_specs=pl.BlockSpec((1, tq, D_MODEL), lambda i: (i // nt, i % nt, 0)),
        out_shape=jax.ShapeDtypeStruct((BATCH, SEQ, D_MODEL), F32),
        scratch_shapes=[pltpu.VMEM((2, TOP_K, tq) + TOKEN_TILE, F32),
                        pltpu.VMEM((tq,) + TOKEN_TILE, F32),
                        pltpu.SemaphoreType.DMA((2,))],
        compiler_params=_params(("arbitrary",)),
        name="combine",
    )(dest, dest, w_flat, base, g2.reshape((BATCH,) + TOKEN_TILE), final_g.reshape(TOKEN_TILE), ys)


def kernel(x, c, ada_w, ada_b, norm1_g, w_in, s5_lam_re, s5_lam_im, s5_log_dt, s5_b_re, s5_b_im,
           s5_c_re, s5_c_im, s5_d, s5_w_glu, s5_b_glu, w_proj_a, gla_w_gk2, gla_b_gk2, gla_norm_g,
           w_proj_b, w_out, norm2_g, router_w, router_bias, exp_w_gate, exp_w_up, exp_w_down,
           sh_w_gate, sh_w_up, sh_w_down, final_g):
    l = 0
    mod = _ada(c, ada_w[l], ada_b[l])
    sh1, sc1, g1, sh2, sc2, g2 = [m.reshape(BATCH, 1, D_MODEL) for m in jnp.split(mod, 6, axis=-1)]

    w = w_in[l]
    gk0 = _V0 + GLA_VAL
    w_main = jnp.concatenate([w[:, :gk0], w[:, gk0 + GLA_GATE_RANK:]], axis=1).astype(BF16)
    w_gk = jnp.pad(w[:, gk0:gk0 + GLA_GATE_RANK], ((0, 0), (0, LANES - GLA_GATE_RANK))).astype(BF16)
    w_gk2 = jnp.pad(gla_w_gk2[l], ((0, LANES - GLA_GATE_RANK), (0, 0))).astype(BF16)
    u_tb, q, k, v, la, rs, ga, gb = _inproj(
        x, norm1_g[l].reshape(1, D_MODEL), sc1, sh1, w_main, w_gk, w_gk2,
        gla_b_gk2[l].reshape(1, GLA_KEY))
    tables = _s5_tables(s5_lam_re[l], s5_lam_im[l], s5_log_dt[l], s5_b_re[l], s5_b_im[l],
                        s5_c_re[l], s5_c_im[l], s5_d[l])
    ya_tb = _s5(u_tb, *tables)
    yb = _gla(q, k, la, v, rs, gla_norm_g[l])

    wsgu = jnp.concatenate([sh_w_gate[l], sh_w_up[l]], axis=1).astype(BF16)
    base, hp, scores = _merge(
        ya_tb, yb, ga, gb, x, g1, sc2, sh2, g2,
        norm2_g[l].reshape(1, D_MODEL), s5_w_glu[l].astype(BF16), s5_b_glu[l].reshape(1, S5_WIDTH),
        w_proj_a[l].astype(BF16), w_proj_b[l].astype(BF16), w_out[l].astype(BF16),
        router_w[l].astype(BF16), wsgu, sh_w_down[l].astype(BF16))

    e_idx, rank, w_k, cnt = _route(scores, router_bias[l])
    counts = cnt[:, 0].astype(I32)
    padded = (counts + MOE_BLOCK - 1) // MOE_BLOCK * MOE_BLOCK
    pad_end = jnp.cumsum(padded)
    pad_start = pad_end - padded
    dest = _slots(e_idx, rank, pad_start).T.reshape(-1)
    pad_start = pad_start.astype(I32)
    n_blk = (padded // MOE_BLOCK).astype(I32)
    xs = _dispatch(pad_start, counts, n_blk, dest, hp)
    ys = _experts(pad_start, n_blk, xs, exp_w_gate[l], exp_w_up[l], exp_w_down[l])
    return _combine(dest, w_k.T.reshape(-1), base, g2, final_g, ys)
```

```python
import functools

import jax
import jax.numpy as jnp
from jax import lax
from jax.experimental import pallas as pl
from jax.experimental.pallas import tpu as pltpu

F32 = jnp.float32
BF16 = jnp.bfloat16
I32 = jnp.int32

D_MODEL = 1024
BATCH = 8
SEQ = 2048
TOKENS = BATCH * SEQ
S5_WIDTH = 512
S5_GROUP = 16
S5_GROUPS = 32
S5_STATE = 64
S5_COLS = S5_GROUPS * S5_STATE
GLA_HEADS = 4
GLA_DK = 64
GLA_DV = 128
GLA_KEY = GLA_HEADS * GLA_DK
GLA_VAL = GLA_HEADS * GLA_DV
GLA_GATE_RANK = 16
GLA_GATE_TAU = 16.0
GLA_CHUNK = 64
GLA_SUB = 16
N_EXPERTS = 256
TOP_K = 8
N_GROUPS = 8
GROUP_SIZE = N_EXPERTS // N_GROUPS
TOPK_GROUPS = 4
EXPERT_FF = 256
ROUTE_SCALE = 2.5
MOE_BLOCK = 128
EPS = 1e-6
N_SLOTS = -(-(TOKENS * TOP_K + N_EXPERTS * (MOE_BLOCK - 1)) // MOE_BLOCK) * MOE_BLOCK
N_BLOCKS = N_SLOTS // MOE_BLOCK

LANES = 128
SUBLANES = 8
TOKEN_TILE = (SUBLANES, LANES)
assert D_MODEL == SUBLANES * LANES
VMEM_LIMIT = 56 * 1024 * 1024

ADA_TN = 1536
INPROJ_TM = 512
S5_TC = 128
S5_COL_CHUNK = 512
GLA_TG = 512
MERGE_TM = 512
MERGE_SPLIT = 2
ROUTE_TR = 512
DISPATCH_TQ = 1024
COMBINE_TQ = 256
EXPERT_UNIT = 2
EXPERT_NBUF = 6
EXP_CLAMP = 60.0


def _params(sem, vmem=VMEM_LIMIT):
    return pltpu.CompilerParams(dimension_semantics=sem, vmem_limit_bytes=vmem)


def _dot(a, b):
    return jnp.dot(a, b, preferred_element_type=F32)


def _dot_nt(a, b):
    return lax.dot_general(a, b, (((1,), (1,)), ((), ())), preferred_element_type=F32)


def _sigmoid(x):
    return jax.nn.sigmoid(x)


def _ada_kernel(c_ref, w_ref, b_ref, o_ref):
    c = c_ref[...]
    s = (c * _sigmoid(c)).astype(BF16)
    o_ref[...] = _dot(s, w_ref[...].astype(BF16)) + b_ref[...]


def _ada(c, w, b):
    n = w.shape[1]
    return pl.pallas_call(
        _ada_kernel,
        grid=(n // ADA_TN,),
        in_specs=[
            pl.BlockSpec((BATCH, D_MODEL), lambda j: (0, 0)),
            pl.BlockSpec((D_MODEL, ADA_TN), lambda j: (0, j)),
            pl.BlockSpec((1, ADA_TN), lambda j: (0, j)),
        ],
        out_specs=pl.BlockSpec((BATCH, ADA_TN), lambda j: (0, j)),
        out_shape=jax.ShapeDtypeStruct((BATCH, n), F32),
        compiler_params=_params(("arbitrary",)),
        name="ada",
    )(c, w, b.reshape(1, n))


_U0, _Q0, _K0, _V0, _R0, _GA0, _GB0, _END = 0, 512, 768, 1024, 1536, 2048, 3072, 4096


def _inproj_kernel(x_ref, g_ref, sc_ref, sh_ref, wm_ref, wgk_ref, wgk2_ref, bgk_ref,
                   u_ref, q_ref, k_ref, v_ref, la_ref, rs_ref, ga_ref, gb_ref):
    x = x_ref[0]
    r = lax.rsqrt(jnp.mean(x * x, axis=-1, keepdims=True) + EPS)
    h = (x * r) * g_ref[...] * (1.0 + sc_ref[0]) + sh_ref[0]
    hb = h.astype(BF16)

    def seg(lo, hi):
        return _dot(hb, wm_ref[:, lo:hi])

    u_ref[...] = seg(_U0, _Q0)
    q_ref[0] = seg(_Q0, _K0) * (GLA_DK ** -0.5)
    k_ref[0] = seg(_K0, _V0)
    v_ref[0] = seg(_V0, _R0)
    rr = seg(_R0, _GA0)
    rs_ref[0] = rr * _sigmoid(rr)
    ga_ref[0] = _sigmoid(seg(_GA0, _GB0))
    gb_ref[0] = _sigmoid(seg(_GB0, _END))
    gk = _dot(hb, wgk_ref[...])
    z = _dot(gk.astype(BF16), wgk2_ref[...]) + bgk_ref[...]
    la_ref[0] = -(jnp.maximum(-z, 0.0) + jnp.log1p(jnp.exp(-jnp.abs(z)))) * (1.0 / GLA_GATE_TAU)


def _inproj(x, g, sc, sh, w_main, w_gk, w_gk2, b_gk2):
    tm = INPROJ_TM
    nt = SEQ // tm
    row3 = lambda w: pl.BlockSpec((1, tm, w), lambda b, i: (b, i, 0))
    const = lambda shape: pl.BlockSpec(shape, lambda b, i: tuple(0 for _ in shape))
    mod = pl.BlockSpec((1, 1, D_MODEL), lambda b, i: (b, 0, 0))
    bld = lambda w: jax.ShapeDtypeStruct((BATCH, SEQ, w), F32)
    return pl.pallas_call(
        _inproj_kernel,
        grid=(BATCH, nt),
        in_specs=[row3(D_MODEL), const((1, D_MODEL)), mod, mod,
                  const((D_MODEL, _END)), const((D_MODEL, LANES)), const((LANES, GLA_KEY)),
                  const((1, GLA_KEY))],
        out_specs=[pl.BlockSpec((tm, S5_WIDTH), lambda b, i: (i, b)),
                   row3(GLA_KEY), row3(GLA_KEY), row3(GLA_VAL), row3(GLA_KEY), row3(GLA_VAL),
                   row3(D_MODEL), row3(D_MODEL)],
        out_shape=[jax.ShapeDtypeStruct((SEQ, BATCH * S5_WIDTH), F32),
                   bld(GLA_KEY), bld(GLA_KEY), bld(GLA_VAL), bld(GLA_KEY), bld(GLA_VAL),
                   bld(D_MODEL), bld(D_MODEL)],
        compiler_params=_params(("parallel", "parallel")),
        name="inproj",
    )(x, g, sc, sh, w_main, w_gk, w_gk2, b_gk2)


def _s5_kernel(u_ref, bre_ref, bim_ref, cre_ref, cim_ref, are_ref, aim_ref, d_ref,
               y_ref, s_ref, st_ref):
    half = S5_COLS // 2

    @pl.when(pl.program_id(0) == 0)
    def _():
        st_ref[...] = jnp.zeros_like(st_ref)

    u = u_ref[...].reshape(S5_TC, BATCH, S5_WIDTH).reshape(S5_TC * BATCH, S5_WIDTH)
    ub = u.astype(BF16)
    for j in range(2):
        uj = ub[:, j * 256:(j + 1) * 256]
        s_ref[:, j * half:(j + 1) * half] = _dot(uj, bre_ref[j])
        s_ref[:, S5_COLS + j * half:S5_COLS + (j + 1) * half] = _dot(uj, bim_ref[j])

    for cc in range(S5_COLS // S5_COL_CHUNK):
        re_sl = slice(cc * S5_COL_CHUNK, (cc + 1) * S5_COL_CHUNK)
        im_sl = slice(S5_COLS + cc * S5_COL_CHUNK, S5_COLS + (cc + 1) * S5_COL_CHUNK)
        ar = are_ref[:, re_sl]
        ai = aim_ref[:, re_sl]

        def body(t, carry, re_sl=re_sl, im_sl=im_sl, ar=ar, ai=ai):
            sr, si = carry
            rows = pl.ds(pl.multiple_of(t * BATCH, BATCH), BATCH)
            nr = ar * sr - ai * si + s_ref[rows, re_sl]
            ni = ar * si + ai * sr + s_ref[rows, im_sl]
            s_ref[rows, re_sl] = nr
            s_ref[rows, im_sl] = ni
            return nr, ni

        sr, si = lax.fori_loop(0, S5_TC, body, (st_ref[:, re_sl], st_ref[:, im_sl]), unroll=4)
        st_ref[:, re_sl] = sr
        st_ref[:, im_sl] = si

    ys = []
    for j in range(2):
        sre = s_ref[:, j * half:(j + 1) * half].astype(BF16)
        sim = s_ref[:, S5_COLS + j * half:S5_COLS + (j + 1) * half].astype(BF16)
        ys.append(_dot(sre, cre_ref[j]) + _dot(sim, cim_ref[j]))
    y = jnp.concatenate(ys, axis=1) + d_ref[...] * u
    y_ref[...] = jax.nn.gelu(y).reshape(S5_TC, BATCH, S5_WIDTH).reshape(S5_TC, BATCH * S5_WIDTH)


def _s5(u_tb, bre, bim, cre, cimn, are, aim, dflat):
    rows = S5_TC * BATCH
    const = lambda shape: pl.BlockSpec(shape, lambda i: tuple(0 for _ in shape))
    return pl.pallas_call(
        _s5_kernel,
        grid=(SEQ // S5_TC,),
        in_specs=[pl.BlockSpec((S5_TC, BATCH * S5_WIDTH), lambda i: (i, 0)),
                  const(bre.shape), const(bim.shape), const(cre.shape), const(cimn.shape),
                  const(are.shape), const(aim.shape), const(dflat.shape)],
        out_specs=pl.BlockSpec((S5_TC, BATCH * S5_WIDTH), lambda i: (i, 0)),
        out_shape=jax.ShapeDtypeStruct((SEQ, BATCH * S5_WIDTH), F32),
        scratch_shapes=[pltpu.VMEM((rows, 2 * S5_COLS), F32),
                        pltpu.VMEM((BATCH, 2 * S5_COLS), F32)],
        compiler_params=_params(("arbitrary",)),
        name="s5",
    )(u_tb, bre, bim, cre, cimn, are, aim, dflat)


def _s5_tables(lam_re, lam_im, log_dt, b_re, b_im, c_re, c_im, d_skip):
    lr = lam_re.astype(F32)
    li = lam_im.astype(F32)
    dt = jnp.exp(log_dt.astype(F32))[:, None]
    mag = jnp.exp(lr * dt)
    abar_re = mag * jnp.cos(li * dt)
    abar_im = mag * jnp.sin(li * dt)
    den = lr * lr + li * li
    num_re = abar_re - 1.0
    coef_re = (num_re * lr + abar_im * li) / den
    coef_im = (abar_im * lr - num_re * li) / den
    cr, ci = coef_re[..., None], coef_im[..., None]
    br, bi = b_re.astype(F32), b_im.astype(F32)
    bbar_re = cr * br - ci * bi
    bbar_im = cr * bi + ci * br
    eye = jnp.eye(S5_GROUPS // 2, dtype=F32)

    def in_map(bb):
        bb = bb.reshape(2, S5_GROUPS // 2, S5_STATE, S5_GROUP)
        return jnp.einsum('jgph,gk->jghkp', bb, eye).reshape(2, 256, S5_COLS // 2).astype(BF16)

    def out_map(cc):
        cc = cc.reshape(2, S5_GROUPS // 2, S5_GROUP, S5_STATE)
        return jnp.einsum('jghp,gk->jgpkh', cc, eye).reshape(2, S5_COLS // 2, 256).astype(BF16)

    are = jnp.broadcast_to(abar_re.reshape(1, S5_COLS), (BATCH, S5_COLS))
    aim = jnp.broadcast_to(abar_im.reshape(1, S5_COLS), (BATCH, S5_COLS))
    return (in_map(bbar_re), in_map(bbar_im), out_map(c_re.astype(F32)), out_map(-c_im.astype(F32)),
            are, aim, d_skip.astype(F32).reshape(1, S5_WIDTH))


def _gla_kernel(q_ref, k_ref, la_ref, v_ref, rs_ref, ng_ref, tt_ref, o_ref, st_ref):
    @pl.when(pl.program_id(1) == 0)
    def _():
        st_ref[...] = jnp.zeros_like(st_ref)

    c = GLA_CHUNK
    tt = tt_ref[...]
    row = lax.broadcasted_iota(I32, (c, GLA_KEY), 0)
    ri = lax.broadcasted_iota(I32, (c, c), 0)
    ci = lax.broadcasted_iota(I32, (c, c), 1)
    causal = ci <= ri
    nsub = c // GLA_SUB

    heads = range(GLA_HEADS)
    hsl = [slice(h * GLA_DK, (h + 1) * GLA_DK) for h in heads]
    vsl = [slice(h * GLA_DV, (h + 1) * GLA_DV) for h in heads]
    chunks = range(GLA_TG // c)
    intra, kv, qdec, dec = [], [], [], []

    sls = [slice(ch * c, (ch + 1) * c) for ch in chunks]
    csum = []
    for ch in chunks:
        g = la_ref[0, sls[ch], :]
        g1 = g.astype(BF16)
        r1 = g - g1.astype(F32)
        g2 = r1.astype(BF16)
        g3 = (r1 - g2.astype(F32)).astype(BF16)
        csum.append(_dot(tt, g1) + _dot(tt, g2) + _dot(tt, g3))
    qbigs, kbigs, kdec = [], [], []
    for ch in chunks:
        sl = sls[ch]
        cs = csum[ch]
        b = cs[0:c]
        cl = cs[c:2 * c]
        ref_pt = b - cl
        q = q_ref[0, sl, :]
        k = k_ref[0, sl, :]
        qt = q * jnp.exp(cl)
        qe = (q * jnp.exp(b)).astype(BF16)
        blast = b[c - 1:c, :]
        ks = (k * jnp.exp(blast - b)).astype(BF16)
        k_sub = []
        q_sub = []
        for s in range(nsub):
            rs_ = ref_pt[s * GLA_SUB:s * GLA_SUB + 1, :]
            k_sub.append(k * jnp.exp(jnp.minimum(rs_ - b, EXP_CLAMP)))
            q_sub.append(jnp.where((row >= s * GLA_SUB) & (row < (s + 1) * GLA_SUB), qt, 0.0))
        qbigs.append([jnp.concatenate([x[:, hsl[h]] for x in q_sub], axis=1).astype(BF16)
                      for h in heads])
        kbigs.append([jnp.concatenate([x[:, hsl[h]] for x in k_sub], axis=1).astype(BF16)
                      for h in heads])
        kdec.append(ks)
        qdec.append(qe)
        dec.append(jnp.exp(blast))
    scores = [[_dot_nt(qbigs[ch][h], kbigs[ch][h]) for h in heads] for ch in chunks]
    for ch in chunks:
        kv.append([_dot(v_ref[0, sls[ch], vsl[h]].T.astype(BF16), kdec[ch][:, hsl[h]])
                   for h in heads])
    for ch in chunks:
        intra.append([_dot(jnp.where(causal, scores[ch][h], 0.0).astype(BF16),
                           v_ref[0, sls[ch], vsl[h]].astype(BF16)) for h in heads])

    st = [st_ref[h] for h in heads]
    for ch in chunks:
        sl = slice(ch * c, (ch + 1) * c)
        inter = [_dot_nt(qdec[ch][:, hsl[h]], st[h].astype(BF16)) for h in heads]
        for h in heads:
            o = inter[h] + intra[ch][h]
            st[h] = st[h] * dec[ch][:, hsl[h]] + kv[ch][h]
            on = o * lax.rsqrt(jnp.mean(o * o, axis=-1, keepdims=True) + EPS) * ng_ref[...]
            o_ref[0, sl, vsl[h]] = on * rs_ref[0, sl, vsl[h]]
    for h in heads:
        st_ref[h] = st[h]


def _gla(q, k, la, v, rs, norm_g):
    tg = GLA_TG
    c = GLA_CHUNK
    r = jnp.arange(c)
    tri = (r[None, :] <= r[:, None])
    blk = tri & ((r[None, :] // GLA_SUB) == (r[:, None] // GLA_SUB))
    tt = jnp.concatenate([tri, blk], axis=0).astype(BF16)
    row3 = lambda w: pl.BlockSpec((1, tg, w), lambda b, i: (b, i, 0))
    const = lambda shape: pl.BlockSpec(shape, lambda b, i: tuple(0 for _ in shape))
    return pl.pallas_call(
        _gla_kernel,
        grid=(BATCH, SEQ // tg),
        in_specs=[row3(GLA_KEY), row3(GLA_KEY), row3(GLA_KEY), row3(GLA_VAL), row3(GLA_VAL),
                  const((1, GLA_DV)), const((2 * c, c))],
        out_specs=row3(GLA_VAL),
        out_shape=jax.ShapeDtypeStruct((BATCH, SEQ, GLA_VAL), F32),
        scratch_shapes=[pltpu.VMEM((GLA_HEADS, GLA_DV, GLA_DK), F32)],
        compiler_params=_params(("parallel", "arbitrary")),
        name="gla",
    )(q, k, la, v, rs, norm_g.reshape(1, GLA_DV), tt)


def _merge_kernel(ya_ref, yb_ref, ga_ref, gb_ref, x_ref, g1_ref, sc2_ref, sh2_ref, g2_ref, n2_ref,
                  wglu_ref, bglu_ref, wpa_ref, wpb_ref, wout_ref, wr_ref, wsgu_ref, wsd_ref,
                  base_ref, hp_ref, sc_ref):
    rows = MERGE_TM // MERGE_SPLIT
    subs = [slice(s * rows, (s + 1) * rows) for s in range(MERGE_SPLIT)]
    ya = [ya_ref[sl, :] for sl in subs]
    z = [_dot(a.astype(BF16), wglu_ref[...]) for a in ya]
    pb = [_dot(yb_ref[0, sl, :].astype(BF16), wpb_ref[...]) for sl in subs]
    ya2 = [(a * _sigmoid(zz + bglu_ref[...])).astype(BF16) for a, zz in zip(ya, z)]
    pa = [_dot(a, wpa_ref[...]) for a in ya2]
    mixed = [(ga_ref[0, sl, :] * p + gb_ref[0, sl, :] * q).astype(BF16)
             for sl, p, q in zip(subs, pa, pb)]
    mo = [_dot(m, wout_ref[...]) for m in mixed]
    x1 = [x_ref[0, sl, :] + g1_ref[0] * m for sl, m in zip(subs, mo)]
    h = [(x * lax.rsqrt(jnp.mean(x * x, axis=-1, keepdims=True) + EPS)) * n2_ref[...]
         * (1.0 + sc2_ref[0]) + sh2_ref[0] for x in x1]
    hb = [v.astype(BF16) for v in h]
    logits = [_dot(v, wr_ref[...]) for v in hb]
    gu = [_dot(v, wsgu_ref[...]) for v in hb]
    mid = [(g[:, :EXPERT_FF] * _sigmoid(g[:, :EXPERT_FF]) * g[:, EXPERT_FF:]).astype(BF16) for g in gu]
    shared = [_dot(m, wsd_ref[...]) for m in mid]
    for s, sl in enumerate(subs):
        sc_ref[sl, :] = _sigmoid(logits[s])
        base_ref[sl] = (x1[s] + g2_ref[0] * shared[s]).reshape(rows, *TOKEN_TILE)
        hp_ref[sl] = h[s].reshape(rows, *TOKEN_TILE)


def _merge(ya_tb, yb, ga, gb, x, g1, sc2, sh2, g2, n2, wglu, bglu, wpa, wpb, wout, wr, wsgu, wsd):
    tm = MERGE_TM
    row3 = lambda w: pl.BlockSpec((1, tm, w), lambda b, i: (b, i, 0))
    const = lambda a: pl.BlockSpec(a.shape, lambda b, i: tuple(0 for _ in a.shape))
    mod = pl.BlockSpec((1, 1, D_MODEL), lambda b, i: (b, 0, 0))
    nt = SEQ // tm
    flat = lambda w: pl.BlockSpec((tm, w), lambda b, i: (b * nt + i, 0))
    tiles = pl.BlockSpec((tm,) + TOKEN_TILE, lambda b, i: (b * nt + i, 0, 0))
    return pl.pallas_call(
        _merge_kernel,
        grid=(BATCH, nt),
        in_specs=[pl.BlockSpec((tm, S5_WIDTH), lambda b, i: (i, b)),
                  row3(GLA_VAL), row3(D_MODEL), row3(D_MODEL), row3(D_MODEL),
                  mod, mod, mod, mod, const(n2),
                  const(wglu), const(bglu), const(wpa), const(wpb), const(wout), const(wr),
                  const(wsgu), const(wsd)],
        out_specs=[tiles, tiles, flat(N_EXPERTS)],
        out_shape=[jax.ShapeDtypeStruct((TOKENS,) + TOKEN_TILE, F32),
                   jax.ShapeDtypeStruct((TOKENS,) + TOKEN_TILE, F32),
                   jax.ShapeDtypeStruct((TOKENS, N_EXPERTS), F32)],
        compiler_params=_params(("parallel", "parallel")),
        name="merge",
    )(ya_tb, yb, ga, gb, x, g1, sc2, sh2, g2, n2, wglu, bglu, wpa, wpb, wout, wr, wsgu, wsd)


def _first_argmax(x, iota, size):
    m = jnp.max(x, axis=0, keepdims=True)
    idx = jnp.min(jnp.where(x == m, iota, size), axis=0, keepdims=True)
    return m, idx


def _route_kernel(s_ref, bias_ref, tri_ref, e_ref, r_ref, w_ref, cnt_ref, carry_ref):
    @pl.when(pl.program_id(0) == 0)
    def _():
        carry_ref[...] = jnp.zeros_like(carry_ref)

    tr = ROUTE_TR
    neg = -jnp.inf
    s_t = s_ref[...].T
    biased = s_t + bias_ref[...]
    io_g = lax.broadcasted_iota(I32, (GROUP_SIZE, tr), 0)
    rows = []
    for g in range(N_GROUPS):
        xg = biased[g * GROUP_SIZE:(g + 1) * GROUP_SIZE, :]
        m1, i1 = _first_argmax(xg, io_g, GROUP_SIZE)
        m2 = jnp.max(jnp.where(io_g == i1, neg, xg), axis=0, keepdims=True)
        rows.append(m1 + m2)
    gs = jnp.concatenate(rows, axis=0)
    io_n = lax.broadcasted_iota(I32, (N_GROUPS, tr), 0)
    gsel = jnp.zeros((N_GROUPS, tr), F32)
    for _ in range(TOPK_GROUPS):
        _, gi = _first_argmax(gs, io_n, N_GROUPS)
        hit = io_n == gi
        gsel = jnp.where(hit, 1.0, gsel)
        gs = jnp.where(hit, neg, gs)
    masked = jnp.concatenate(
        [jnp.where(gsel[g:g + 1, :] > 0.0, biased[g * GROUP_SIZE:(g + 1) * GROUP_SIZE, :], neg)
         for g in range(N_GROUPS)], axis=0)
    io_e = lax.broadcasted_iota(I32, (N_EXPERTS, tr), 0)
    sel = jnp.zeros((N_EXPERTS, tr), F32)
    idxs = []
    for _ in range(TOP_K):
        _, ei = _first_argmax(masked, io_e, N_EXPERTS)
        hit = io_e == ei
        sel = jnp.where(hit, 1.0, sel)
        masked = jnp.where(hit, neg, masked)
        idxs.append(ei)
    den = jnp.sum(sel * s_t, axis=0, keepdims=True)
    rank = _dot(sel.astype(BF16), tri_ref[...]) + carry_ref[...]
    rks, sks = [], []
    for kk in range(TOP_K):
        hit = io_e == idxs[kk]
        rks.append(jnp.sum(jnp.where(hit, rank, 0.0), axis=0, keepdims=True))
        sks.append(jnp.sum(jnp.where(hit, s_t, 0.0), axis=0, keepdims=True))
    e_ref[...] = jnp.concatenate(idxs, axis=0)
    r_ref[...] = jnp.concatenate(rks, axis=0).astype(I32)
    w_ref[...] = jnp.concatenate(sks, axis=0) / den * ROUTE_SCALE
    carry_ref[...] += jnp.sum(sel, axis=1, keepdims=True)
    cnt_ref[...] = carry_ref[...]


def _route(scores, bias):
    tr = ROUTE_TR
    r = jnp.arange(tr)
    tri = (r[:, None] < r[None, :]).astype(BF16)
    kt = lambda dt: jax.ShapeDtypeStruct((TOP_K, TOKENS), dt)
    blk = pl.BlockSpec((TOP_K, tr), lambda i: (0, i))
    return pl.pallas_call(
        _route_kernel,
        grid=(TOKENS // tr,),
        in_specs=[pl.BlockSpec((tr, N_EXPERTS), lambda i: (i, 0)),
                  pl.BlockSpec((N_EXPERTS, 1), lambda i: (0, 0)),
                  pl.BlockSpec((tr, tr), lambda i: (0, 0))],
        out_specs=[blk, blk, blk, pl.BlockSpec((N_EXPERTS, 1), lambda i: (0, 0))],
        out_shape=[kt(I32), kt(I32), kt(F32), jax.ShapeDtypeStruct((N_EXPERTS, 1), F32)],
        scratch_shapes=[pltpu.VMEM((N_EXPERTS, 1), F32)],
        compiler_params=_params(("arbitrary",)),
        name="route",
    )(scores, bias.reshape(N_EXPERTS, 1), tri)


def _slots_kernel(e_ref, r_ref, ps_ref, d_ref):
    tr = ROUTE_TR
    io_e = lax.broadcasted_iota(I32, (N_EXPERTS, tr), 0)
    ps = ps_ref[...]
    rows = []
    for kk in range(TOP_K):
        hit = io_e == e_ref[kk:kk + 1, :]
        rows.append(jnp.sum(jnp.where(hit, ps, 0.0), axis=0, keepdims=True))
    d_ref[...] = jnp.concatenate(rows, axis=0).astype(I32) + r_ref[...]


def _slots(e_idx, rank, pad_start):
    tr = ROUTE_TR
    blk = pl.BlockSpec((TOP_K, tr), lambda i: (0, i))
    return pl.pallas_call(
        _slots_kernel,
        grid=(TOKENS // tr,),
        in_specs=[blk, blk, pl.BlockSpec((N_EXPERTS, 1), lambda i: (0, 0))],
        out_specs=blk,
        out_shape=jax.ShapeDtypeStruct((TOP_K, TOKENS), I32),
        compiler_params=_params(("parallel",)),
        name="slots",
    )(e_idx, rank, pad_start.astype(F32).reshape(N_EXPERTS, 1))


def _dispatch_kernel(ps_ref, cnt_ref, nb_ref, dest_ref, h_ref, xs_ref, zbuf, sem, zsem):
    tq = DISPATCH_TQ
    step = pl.program_id(0)
    n_used = ps_ref[N_EXPERTS - 1] // MOE_BLOCK + nb_ref[N_EXPERTS - 1]

    def zero_fill(act):
        def block(j, carry):
            r = pl.ds(pl.multiple_of(j * MOE_BLOCK, MOE_BLOCK), MOE_BLOCK)
            act(pltpu.make_async_copy(zbuf, xs_ref.at[r], zsem))
            return carry

        def expert(e, carry):
            @pl.when(cnt_ref[e] < nb_ref[e] * MOE_BLOCK)
            def _():
                block(ps_ref[e] // MOE_BLOCK + nb_ref[e] - 1, 0)
            return carry

        lax.fori_loop(0, N_EXPERTS, expert, 0)
        lax.fori_loop(n_used, N_BLOCKS, block, 0)

    @pl.when(step == 0)
    def _():
        zbuf[...] = jnp.zeros_like(zbuf)
        zero_fill(lambda cp: cp.start())
        zero_fill(lambda cp: cp.wait())

    def start(i, carry):
        for kk in range(TOP_K):
            d = dest_ref[i * TOP_K + kk]
            pltpu.make_async_copy(h_ref.at[i], xs_ref.at[d],
                                  sem).start(priority=kk % 2)
        return carry

    lax.fori_loop(0, tq, start, 0, unroll=2)
    for _ in range(TOP_K):
        pltpu.make_async_copy(h_ref, xs_ref.at[pl.ds(0, tq)], sem).wait()


def _dispatch(pad_start, counts, n_blk, dest, hp):
    tq = DISPATCH_TQ
    grid_spec = pltpu.PrefetchScalarGridSpec(
        num_scalar_prefetch=3,
        grid=(TOKENS // tq,),
        in_specs=[pl.BlockSpec((tq * TOP_K,), lambda i, *_: (i,), memory_space=pltpu.SMEM),
                  pl.BlockSpec((tq,) + TOKEN_TILE, lambda i, *_: (i, 0, 0))],
        out_specs=pl.BlockSpec(memory_space=pl.ANY),
        scratch_shapes=[pltpu.VMEM((MOE_BLOCK,) + TOKEN_TILE, F32),
                        pltpu.SemaphoreType.DMA, pltpu.SemaphoreType.DMA],
    )
    return pl.pallas_call(
        _dispatch_kernel,
        grid_spec=grid_spec,
        out_shape=jax.ShapeDtypeStruct((N_SLOTS,) + TOKEN_TILE, F32),
        compiler_params=_params(("arbitrary",)),
        name="dispatch",
    )(pad_start, counts, n_blk, dest, hp)


def _expert_kernel(ps_ref, nb_ref, wg_ref, wu_ref, wd_ref, xs_ref, ys_ref,
                   xbuf, ybuf, wgus, wds, xsem, ysem):
    e = pl.program_id(0)
    nb = nb_ref[e]
    first = ps_ref[e] // MOE_BLOCK
    n_used = ps_ref[N_EXPERTS - 1] // MOE_BLOCK + nb_ref[N_EXPERTS - 1]
    nbuf = EXPERT_NBUF

    def rows(g):
        return pl.ds(pl.multiple_of(g * MOE_BLOCK, MOE_BLOCK), MOE_BLOCK)

    def x_copy(g):
        slot = g % nbuf
        return pltpu.make_async_copy(xs_ref.at[rows(g)], xbuf.at[slot], xsem.at[slot])

    def y_copy(g):
        slot = g % nbuf
        return pltpu.make_async_copy(ybuf.at[slot], ys_ref.at[rows(g)], ysem.at[slot])

    look = nbuf - EXPERT_UNIT

    @pl.when(e == 0)
    def _():
        for g in range(look):
            @pl.when(g < n_used)
            def _():
                x_copy(g).start()

    def run(g, u):
        for d in range(u):
            @pl.when(g + look + d < n_used)
            def _():
                x_copy(g + look + d).start()

        for d in range(u):
            b = g + d

            @pl.when(b >= nbuf)
            def _():
                y_copy(b - nbuf).wait()

        for d in range(u):
            x_copy(g + d).wait()
        gus = [_dot(xbuf[(g + d) % nbuf].reshape(MOE_BLOCK, D_MODEL).astype(BF16), wgus[...])
               for d in range(u)]
        for d in range(u):
            gate = gus[d][:, :EXPERT_FF]
            mid = (gate * _sigmoid(gate) * gus[d][:, EXPERT_FF:]).astype(BF16)
            ybuf[(g + d) % nbuf] = _dot(mid, wds[...]).reshape(MOE_BLOCK, *TOKEN_TILE)
        for d in range(u):
            y_copy(g + d).start()

    @pl.when(nb > 0)
    def _():
        wgus[:, :EXPERT_FF] = wg_ref[0].astype(BF16)
        wgus[:, EXPERT_FF:] = wu_ref[0].astype(BF16)
        wds[...] = wd_ref[0].astype(BF16)

        def unit(j, carry):
            run(first + j * EXPERT_UNIT, EXPERT_UNIT)
            return carry

        lax.fori_loop(0, nb // EXPERT_UNIT, unit, 0)
        for rem in range(1, EXPERT_UNIT):
            @pl.when(nb % EXPERT_UNIT == rem)
            def _():
                run(first + nb - rem, rem)

    @pl.when(e == N_EXPERTS - 1)
    def _():
        for d in range(nbuf):
            g = n_used - nbuf + d

            @pl.when(g >= 0)
            def _():
                y_copy(g).wait()

        ybuf[0] = jnp.zeros((MOE_BLOCK,) + TOKEN_TILE, F32)

        def z_copy(j):
            return pltpu.make_async_copy(ybuf.at[0], ys_ref.at[rows(j)], ysem.at[0])

        def z_start(j, carry):
            z_copy(j).start()
            return carry

        def z_wait(j, carry):
            z_copy(j).wait()
            return carry

        lax.fori_loop(n_used, N_BLOCKS, z_start, 0)
        lax.fori_loop(n_used, N_BLOCKS, z_wait, 0)


def _experts(pad_start, n_blk, xs, wg, wu, wd):
    wspec = lambda shape: pl.BlockSpec((1,) + shape, lambda e, ps, nb: (e, 0, 0))
    grid_spec = pltpu.PrefetchScalarGridSpec(
        num_scalar_prefetch=2,
        grid=(N_EXPERTS,),
        in_specs=[wspec((D_MODEL, EXPERT_FF)), wspec((D_MODEL, EXPERT_FF)),
                  wspec((EXPERT_FF, D_MODEL)), pl.BlockSpec(memory_space=pl.ANY)],
        out_specs=pl.BlockSpec(memory_space=pl.ANY),
        scratch_shapes=[pltpu.VMEM((EXPERT_NBUF, MOE_BLOCK) + TOKEN_TILE, F32),
                        pltpu.VMEM((EXPERT_NBUF, MOE_BLOCK) + TOKEN_TILE, F32),
                        pltpu.VMEM((D_MODEL, 2 * EXPERT_FF), BF16),
                        pltpu.VMEM((EXPERT_FF, D_MODEL), BF16),
                        pltpu.SemaphoreType.DMA((EXPERT_NBUF,)),
                        pltpu.SemaphoreType.DMA((EXPERT_NBUF,))],
    )
    return pl.pallas_call(
        _expert_kernel,
        grid_spec=grid_spec,
        out_shape=jax.ShapeDtypeStruct((N_SLOTS,) + TOKEN_TILE, F32),
        compiler_params=_params(("arbitrary",)),
        name="experts",
    )(pad_start, n_blk, wg, wu, wd, xs)


def _combine_kernel(dcur_ref, dnext_ref, w_ref, base_ref, g2_ref, fg_ref, ys_ref, o_ref,
                    buf, acc_ref, sem):
    tq = COMBINE_TQ
    i = pl.program_id(0)
    n = pl.num_programs(0)

    def start(dest_ref, to_slot, t):
        for kk in range(TOP_K):
            d = dest_ref[t * TOP_K + kk]
            pltpu.make_async_copy(ys_ref.at[d], buf.at[to_slot, kk, t],
                                  sem.at[to_slot]).start(priority=kk % 2)

    def reduce(slot, t):
        a = buf[slot, 0, t] * w_ref[t * TOP_K]
        for kk in range(1, TOP_K):
            a = a + buf[slot, kk, t] * w_ref[t * TOP_K + kk]
        acc_ref[t] = a

    @pl.when(i == 0)
    def _():
        lax.fori_loop(0, tq, lambda t, c: start(dcur_ref, 0, t) or c, 0, unroll=4)

    for slot in range(2):
        @pl.when(i % 2 == slot)
        def _(slot=slot):
            @pl.when(i + 1 < n)
            def _():
                lax.fori_loop(0, tq, lambda t, c: start(dnext_ref, 1 - slot, t) or c, 0, unroll=4)

            for kk in range(TOP_K):
                pltpu.make_async_copy(ys_ref.at[pl.ds(0, tq)], buf.at[slot, kk],
                                      sem.at[slot]).wait()
            lax.fori_loop(0, tq, lambda t, c: reduce(slot, t) or c, 0, unroll=8)

    xo = base_ref[...] + g2_ref[0] * acc_ref[...]
    ms = jnp.sum(jnp.sum(xo * xo, axis=2, keepdims=True), axis=1, keepdims=True) * (1.0 / D_MODEL)
    o_ref[0] = ((xo * lax.rsqrt(ms + EPS)) * fg_ref[...]).reshape(tq, D_MODEL)


def _combine(dest, w_flat, base, g2, final_g, ys):
    tq = COMBINE_TQ
    nt = SEQ // tq
    n = TOKENS // tq
    smem = lambda imap: pl.BlockSpec((tq * TOP_K,), imap, memory_space=pltpu.SMEM)
    return pl.pallas_call(
        _combine_kernel,
        grid=(n,),
        in_specs=[smem(lambda i: (i,)), smem(lambda i: (jnp.minimum(i + 1, n - 1),)),
                  smem(lambda i: (i,)),
                  pl.BlockSpec((tq,) + TOKEN_TILE, lambda i: (i, 0, 0)),
                  pl.BlockSpec((1,) + TOKEN_TILE, lambda i: (i // nt, 0, 0)),
                  pl.BlockSpec(TOKEN_TILE, lambda i: (0, 0)),
                  pl.BlockSpec(memory_space=pl.ANY)],
        out_specs=pl.BlockSpec((1, tq, D_MODEL), lambda i: (i // nt, i % nt, 0)),
        out_shape=jax.ShapeDtypeStruct((BATCH, SEQ, D_MODEL), F32),
        scratch_shapes=[pltpu.VMEM((2, TOP_K, tq) + TOKEN_TILE, F32),
                        pltpu.VMEM((tq,) + TOKEN_TILE, F32),
                        pltpu.SemaphoreType.DMA((2,))],
        compiler_params=_params(("arbitrary",)),
        name="combine",
    )(dest, dest, w_flat, base, g2.reshape((BATCH,) + TOKEN_TILE), final_g.reshape(TOKEN_TILE), ys)


def kernel(x, c, ada_w, ada_b, norm1_g, w_in, s5_lam_re, s5_lam_im, s5_log_dt, s5_b_re, s5_b_im,
           s5_c_re, s5_c_im, s5_d, s5_w_glu, s5_b_glu, w_proj_a, gla_w_gk2, gla_b_gk2, gla_norm_g,
           w_proj_b, w_out, norm2_g, router_w, router_bias, exp_w_gate, exp_w_up, exp_w_down,
           sh_w_gate, sh_w_up, sh_w_down, final_g):
    l = 0
    mod = _ada(c, ada_w[l], ada_b[l])
    sh1, sc1, g1, sh2, sc2, g2 = [m.reshape(BATCH, 1, D_MODEL) for m in jnp.split(mod, 6, axis=-1)]

    w = w_in[l]
    gk0 = _V0 + GLA_VAL
    w_main = jnp.concatenate([w[:, :gk0], w[:, gk0 + GLA_GATE_RANK:]], axis=1).astype(BF16)
    w_gk = jnp.pad(w[:, gk0:gk0 + GLA_GATE_RANK], ((0, 0), (0, LANES - GLA_GATE_RANK))).astype(BF16)
    w_gk2 = jnp.pad(gla_w_gk2[l], ((0, LANES - GLA_GATE_RANK), (0, 0))).astype(BF16)
    u_tb, q, k, v, la, rs, ga, gb = _inproj(
        x, norm1_g[l].reshape(1, D_MODEL), sc1, sh1, w_main, w_gk, w_gk2,
        gla_b_gk2[l].reshape(1, GLA_KEY))
    tables = _s5_tables(s5_lam_re[l], s5_lam_im[l], s5_log_dt[l], s5_b_re[l], s5_b_im[l],
                        s5_c_re[l], s5_c_im[l], s5_d[l])
    ya_tb = _s5(u_tb, *tables)
    yb = _gla(q, k, la, v, rs, gla_norm_g[l])

    wsgu = jnp.concatenate([sh_w_gate[l], sh_w_up[l]], axis=1).astype(BF16)
    base, hp, scores = _merge(
        ya_tb, yb, ga, gb, x, g1, sc2, sh2, g2,
        norm2_g[l].reshape(1, D_MODEL), s5_w_glu[l].astype(BF16), s5_b_glu[l].reshape(1, S5_WIDTH),
        w_proj_a[l].astype(BF16), w_proj_b[l].astype(BF16), w_out[l].astype(BF16),
        router_w[l].astype(BF16), wsgu, sh_w_down[l].astype(BF16))

    e_idx, rank, w_k, cnt = _route(scores, router_bias[l])
    counts = cnt[:, 0].astype(I32)
    padded = (counts + MOE_BLOCK - 1) // MOE_BLOCK * MOE_BLOCK
    pad_end = jnp.cumsum(padded)
    pad_start = pad_end - padded
    dest = _slots(e_idx, rank, pad_start).T.reshape(-1)
    pad_start = pad_start.astype(I32)
    n_blk = (padded // MOE_BLOCK).astype(I32)
    xs = _dispatch(pad_start, counts, n_blk, dest, hp)
    ys = _experts(pad_start, n_blk, xs, exp_w_gate[l], exp_w_up[l], exp_w_down[l])
    return _combine(dest, w_k.T.reshape(-1), base, g2, final_g, ys)
```

```python
import functools

import jax
import jax.numpy as jnp
from jax import lax
from jax.experimental import pallas as pl
from jax.experimental.pallas import tpu as pltpu

F32 = jnp.float32
BF16 = jnp.bfloat16
I32 = jnp.int32

D_MODEL = 1024
BATCH = 8
SEQ = 2048
TOKENS = BATCH * SEQ
S5_WIDTH = 512
S5_GROUP = 16
S5_GROUPS = 32
S5_STATE = 64
S5_COLS = S5_GROUPS * S5_STATE
GLA_HEADS = 4
GLA_DK = 64
GLA_DV = 128
GLA_KEY = GLA_HEADS * GLA_DK
GLA_VAL = GLA_HEADS * GLA_DV
GLA_GATE_RANK = 16
GLA_GATE_TAU = 16.0
GLA_CHUNK = 64
GLA_SUB = 16
N_EXPERTS = 256
TOP_K = 8
N_GROUPS = 8
GROUP_SIZE = N_EXPERTS // N_GROUPS
TOPK_GROUPS = 4
EXPERT_FF = 256
ROUTE_SCALE = 2.5
MOE_BLOCK = 128
EPS = 1e-6
N_SLOTS = -(-(TOKENS * TOP_K + N_EXPERTS * (MOE_BLOCK - 1)) // MOE_BLOCK) * MOE_BLOCK
N_BLOCKS = N_SLOTS // MOE_BLOCK

LANES = 128
SUBLANES = 8
TOKEN_TILE = (SUBLANES, LANES)
assert D_MODEL == SUBLANES * LANES
VMEM_LIMIT = 56 * 1024 * 1024

ADA_TN = 1536
INPROJ_TM = 512
S5_TC = 128
S5_RB = 32
S5_COL_CHUNK = 512
GLA_TG = 512
MERGE_TM = 512
MERGE_SPLIT = 2
ROUTE_TR = 512
DISPATCH_TQ = 1024
COMBINE_TQ = 256
EXPERT_UNIT = 2
EXPERT_NBUF = 6
EXP_CLAMP = 60.0


def _params(sem, vmem=VMEM_LIMIT):
    return pltpu.CompilerParams(dimension_semantics=sem, vmem_limit_bytes=vmem)


def _dot(a, b):
    return jnp.dot(a, b, preferred_element_type=F32)


def _dot_nt(a, b):
    return lax.dot_general(a, b, (((1,), (1,)), ((), ())), preferred_element_type=F32)


def _sigmoid(x):
    return jax.nn.sigmoid(x)


def _ada_kernel(c_ref, w_ref, b_ref, o_ref):
    c = c_ref[...]
    s = (c * _sigmoid(c)).astype(BF16)
    o_ref[...] = _dot(s, w_ref[...].astype(BF16)) + b_ref[...]


def _ada(c, w, b):
    n = w.shape[1]
    return pl.pallas_call(
        _ada_kernel,
        grid=(n // ADA_TN,),
        in_specs=[
            pl.BlockSpec((BATCH, D_MODEL), lambda j: (0, 0)),
            pl.BlockSpec((D_MODEL, ADA_TN), lambda j: (0, j)),
            pl.BlockSpec((1, ADA_TN), lambda j: (0, j)),
        ],
        out_specs=pl.BlockSpec((BATCH, ADA_TN), lambda j: (0, j)),
        out_shape=jax.ShapeDtypeStruct((BATCH, n), F32),
        compiler_params=_params(("arbitrary",)),
        name="ada",
    )(c, w, b.reshape(1, n))


_U0, _Q0, _K0, _V0, _R0, _GA0, _GB0, _END = 0, 512, 768, 1024, 1536, 2048, 3072, 4096


def _inproj_kernel(x_ref, g_ref, sc_ref, sh_ref, wm_ref, wgk_ref, wgk2_ref, bgk_ref,
                   u_ref, q_ref, k_ref, v_ref, la_ref, rs_ref, ga_ref, gb_ref):
    x = x_ref[0]
    r = lax.rsqrt(jnp.mean(x * x, axis=-1, keepdims=True) + EPS)
    h = (x * r) * g_ref[...] * (1.0 + sc_ref[0]) + sh_ref[0]
    hb = h.astype(BF16)

    def seg(lo, hi):
        return _dot(hb, wm_ref[:, lo:hi])

    u_ref[...] = seg(_U0, _Q0)
    q_ref[0] = seg(_Q0, _K0) * (GLA_DK ** -0.5)
    k_ref[0] = seg(_K0, _V0)
    v_ref[0] = seg(_V0, _R0)
    rr = seg(_R0, _GA0)
    rs_ref[0] = rr * _sigmoid(rr)
    ga_ref[0] = _sigmoid(seg(_GA0, _GB0))
    gb_ref[0] = _sigmoid(seg(_GB0, _END))
    gk = _dot(hb, wgk_ref[...])
    z = _dot(gk.astype(BF16), wgk2_ref[...]) + bgk_ref[...]
    la_ref[0] = -(jnp.maximum(-z, 0.0) + jnp.log1p(jnp.exp(-jnp.abs(z)))) * (1.0 / GLA_GATE_TAU)


def _inproj(x, g, sc, sh, w_main, w_gk, w_gk2, b_gk2):
    tm = INPROJ_TM
    nt = SEQ // tm
    row3 = lambda w: pl.BlockSpec((1, tm, w), lambda b, i: (b, i, 0))
    const = lambda shape: pl.BlockSpec(shape, lambda b, i: tuple(0 for _ in shape))
    mod = pl.BlockSpec((1, 1, D_MODEL), lambda b, i: (b, 0, 0))
    bld = lambda w: jax.ShapeDtypeStruct((BATCH, SEQ, w), F32)
    return pl.pallas_call(
        _inproj_kernel,
        grid=(BATCH, nt),
        in_specs=[row3(D_MODEL), const((1, D_MODEL)), mod, mod,
                  const((D_MODEL, _END)), const((D_MODEL, LANES)), const((LANES, GLA_KEY)),
                  const((1, GLA_KEY))],
        out_specs=[pl.BlockSpec((tm, S5_WIDTH), lambda b, i: (i, b)),
                   row3(GLA_KEY), row3(GLA_KEY), row3(GLA_VAL), row3(GLA_KEY), row3(GLA_VAL),
                   row3(D_MODEL), row3(D_MODEL)],
        out_shape=[jax.ShapeDtypeStruct((SEQ, BATCH * S5_WIDTH), F32),
                   bld(GLA_KEY), bld(GLA_KEY), bld(GLA_VAL), bld(GLA_KEY), bld(GLA_VAL),
                   bld(D_MODEL), bld(D_MODEL)],
        compiler_params=_params(("parallel", "parallel")),
        name="inproj",
    )(x, g, sc, sh, w_main, w_gk, w_gk2, b_gk2)


def _s5_kernel(u_ref, bre_ref, bim_ref, cre_ref, cim_ref, are_ref, aim_ref, d_ref,
               y_ref, s_ref, st_ref):
    half = S5_COLS // 2

    @pl.when(pl.program_id(0) == 0)
    def _():
        st_ref[...] = jnp.zeros_like(st_ref)

    u = u_ref[...].reshape(S5_TC, BATCH, S5_WIDTH).reshape(S5_TC * BATCH, S5_WIDTH)
    ub = u.astype(BF16)
    nblk = S5_TC // S5_RB
    brows = [slice(r * S5_RB * BATCH, (r + 1) * S5_RB * BATCH) for r in range(nblk)]
    for r in range(nblk):
        for j in range(2):
            uj = ub[brows[r], j * 256:(j + 1) * 256]
            s_ref[brows[r], j * half:(j + 1) * half] = _dot(uj, bre_ref[j])
            s_ref[brows[r], S5_COLS + j * half:S5_COLS + (j + 1) * half] = _dot(uj, bim_ref[j])

    ncc = S5_COLS // S5_COL_CHUNK
    re_sl = [slice(cc * S5_COL_CHUNK, (cc + 1) * S5_COL_CHUNK) for cc in range(ncc)]
    im_sl = [slice(S5_COLS + cc * S5_COL_CHUNK, S5_COLS + (cc + 1) * S5_COL_CHUNK) for cc in range(ncc)]
    sr = [st_ref[:, sl] for sl in re_sl]
    si = [st_ref[:, sl] for sl in im_sl]
    for r in range(nblk):
        for t in range(r * S5_RB, (r + 1) * S5_RB):
            rows = slice(t * BATCH, (t + 1) * BATCH)
            for cc in range(ncc):
                ar = are_ref[:, re_sl[cc]]
                ai = aim_ref[:, re_sl[cc]]
                nr = ar * sr[cc] - ai * si[cc] + s_ref[rows, re_sl[cc]]
                ni = ar * si[cc] + ai * sr[cc] + s_ref[rows, im_sl[cc]]
                s_ref[rows, re_sl[cc]] = nr
                s_ref[rows, im_sl[cc]] = ni
                sr[cc], si[cc] = nr, ni
    for cc in range(ncc):
        st_ref[:, re_sl[cc]] = sr[cc]
        st_ref[:, im_sl[cc]] = si[cc]

    for r in range(nblk):
        ys = []
        for j in range(2):
            sre = s_ref[brows[r], j * half:(j + 1) * half].astype(BF16)
            sim = s_ref[brows[r], S5_COLS + j * half:S5_COLS + (j + 1) * half].astype(BF16)
            ys.append(_dot(sre, cre_ref[j]) + _dot(sim, cim_ref[j]))
        y = jnp.concatenate(ys, axis=1) + d_ref[...] * u[brows[r], :]
        y_ref[r * S5_RB:(r + 1) * S5_RB, :] = (
            jax.nn.gelu(y).reshape(S5_RB, BATCH, S5_WIDTH).reshape(S5_RB, BATCH * S5_WIDTH))


def _s5(u_tb, bre, bim, cre, cimn, are, aim, dflat):
    rows = S5_TC * BATCH
    const = lambda shape: pl.BlockSpec(shape, lambda i: tuple(0 for _ in shape))
    return pl.pallas_call(
        _s5_kernel,
        grid=(SEQ // S5_TC,),
        in_specs=[pl.BlockSpec((S5_TC, BATCH * S5_WIDTH), lambda i: (i, 0)),
                  const(bre.shape), const(bim.shape), const(cre.shape), const(cimn.shape),
                  const(are.shape), const(aim.shape), const(dflat.shape)],
        out_specs=pl.BlockSpec((S5_TC, BATCH * S5_WIDTH), lambda i: (i, 0)),
        out_shape=jax.ShapeDtypeStruct((SEQ, BATCH * S5_WIDTH), F32),
        scratch_shapes=[pltpu.VMEM((rows, 2 * S5_COLS), F32),
                        pltpu.VMEM((BATCH, 2 * S5_COLS), F32)],
        compiler_params=_params(("arbitrary",)),
        name="s5",
    )(u_tb, bre, bim, cre, cimn, are, aim, dflat)


def _s5_tables(lam_re, lam_im, log_dt, b_re, b_im, c_re, c_im, d_skip):
    lr = lam_re.astype(F32)
    li = lam_im.astype(F32)
    dt = jnp.exp(log_dt.astype(F32))[:, None]
    mag = jnp.exp(lr * dt)
    abar_re = mag * jnp.cos(li * dt)
    abar_im = mag * jnp.sin(li * dt)
    den = lr * lr + li * li
    num_re = abar_re - 1.0
    coef_re = (num_re * lr + abar_im * li) / den
    coef_im = (abar_im * lr - num_re * li) / den
    cr, ci = coef_re[..., None], coef_im[..., None]
    br, bi = b_re.astype(F32), b_im.astype(F32)
    bbar_re = cr * br - ci * bi
    bbar_im = cr * bi + ci * br
    eye = jnp.eye(S5_GROUPS // 2, dtype=F32)

    def in_map(bb):
        bb = bb.reshape(2, S5_GROUPS // 2, S5_STATE, S5_GROUP)
        return jnp.einsum('jgph,gk->jghkp', bb, eye).reshape(2, 256, S5_COLS // 2).astype(BF16)

    def out_map(cc):
        cc = cc.reshape(2, S5_GROUPS // 2, S5_GROUP, S5_STATE)
        return jnp.einsum('jghp,gk->jgpkh', cc, eye).reshape(2, S5_COLS // 2, 256).astype(BF16)

    are = jnp.broadcast_to(abar_re.reshape(1, S5_COLS), (BATCH, S5_COLS))
    aim = jnp.broadcast_to(abar_im.reshape(1, S5_COLS), (BATCH, S5_COLS))
    return (in_map(bbar_re), in_map(bbar_im), out_map(c_re.astype(F32)), out_map(-c_im.astype(F32)),
            are, aim, d_skip.astype(F32).reshape(1, S5_WIDTH))


def _gla_kernel(q_ref, k_ref, la_ref, v_ref, rs_ref, ng_ref, tt_ref, o_ref, st_ref):
    @pl.when(pl.program_id(1) == 0)
    def _():
        st_ref[...] = jnp.zeros_like(st_ref)

    c = GLA_CHUNK
    tt = tt_ref[...]
    row = lax.broadcasted_iota(I32, (c, GLA_KEY), 0)
    ri = lax.broadcasted_iota(I32, (c, c), 0)
    ci = lax.broadcasted_iota(I32, (c, c), 1)
    causal = ci <= ri
    nsub = c // GLA_SUB

    heads = range(GLA_HEADS)
    hsl = [slice(h * GLA_DK, (h + 1) * GLA_DK) for h in heads]
    vsl = [slice(h * GLA_DV, (h + 1) * GLA_DV) for h in heads]
    chunks = range(GLA_TG // c)
    intra, kv, qdec, dec = [], [], [], []

    sls = [slice(ch * c, (ch + 1) * c) for ch in chunks]
    csum = []
    for ch in chunks:
        g = la_ref[0, sls[ch], :]
        g1 = g.astype(BF16)
        r1 = g - g1.astype(F32)
        g2 = r1.astype(BF16)
        g3 = (r1 - g2.astype(F32)).astype(BF16)
        csum.append(_dot(tt, g1) + _dot(tt, g2) + _dot(tt, g3))
    qbigs, kbigs, kdec = [], [], []
    for ch in chunks:
        sl = sls[ch]
        cs = csum[ch]
        b = cs[0:c]
        cl = cs[c:2 * c]
        ref_pt = b - cl
        q = q_ref[0, sl, :]
        k = k_ref[0, sl, :]
        qt = q * jnp.exp(cl)
        qe = (q * jnp.exp(b)).astype(BF16)
        blast = b[c - 1:c, :]
        ks = (k * jnp.exp(blast - b)).astype(BF16)
        k_sub = []
        q_sub = []
        for s in range(nsub):
            rs_ = ref_pt[s * GLA_SUB:s * GLA_SUB + 1, :]
            k_sub.append(k * jnp.exp(jnp.minimum(rs_ - b, EXP_CLAMP)))
            q_sub.append(jnp.where((row >= s * GLA_SUB) & (row < (s + 1) * GLA_SUB), qt, 0.0))
        qbigs.append([jnp.concatenate([x[:, hsl[h]] for x in q_sub], axis=1).astype(BF16)
                      for h in heads])
        kbigs.append([jnp.concatenate([x[:, hsl[h]] for x in k_sub], axis=1).astype(BF16)
                      for h in heads])
        kdec.append(ks)
        qdec.append(qe)
        dec.append(jnp.exp(blast))
    scores = [[_dot_nt(qbigs[ch][h], kbigs[ch][h]) for h in heads] for ch in chunks]
    for ch in chunks:
        kv.append([_dot(v_ref[0, sls[ch], vsl[h]].T.astype(BF16), kdec[ch][:, hsl[h]])
                   for h in heads])
    for ch in chunks:
        intra.append([_dot(jnp.where(causal, scores[ch][h], 0.0).astype(BF16),
                           v_ref[0, sls[ch], vsl[h]].astype(BF16)) for h in heads])

    st = [st_ref[h] for h in heads]
    for ch in chunks:
        sl = slice(ch * c, (ch + 1) * c)
        inter = [_dot_nt(qdec[ch][:, hsl[h]], st[h].astype(BF16)) for h in heads]
        for h in heads:
            o = inter[h] + intra[ch][h]
            st[h] = st[h] * dec[ch][:, hsl[h]] + kv[ch][h]
            on = o * lax.rsqrt(jnp.mean(o * o, axis=-1, keepdims=True) + EPS) * ng_ref[...]
            o_ref[0, sl, vsl[h]] = on * rs_ref[0, sl, vsl[h]]
    for h in heads:
        st_ref[h] = st[h]


def _gla(q, k, la, v, rs, norm_g):
    tg = GLA_TG
    c = GLA_CHUNK
    r = jnp.arange(c)
    tri = (r[None, :] <= r[:, None])
    blk = tri & ((r[None, :] // GLA_SUB) == (r[:, None] // GLA_SUB))
    tt = jnp.concatenate([tri, blk], axis=0).astype(BF16)
    row3 = lambda w: pl.BlockSpec((1, tg, w), lambda b, i: (b, i, 0))
    const = lambda shape: pl.BlockSpec(shape, lambda b, i: tuple(0 for _ in shape))
    return pl.pallas_call(
        _gla_kernel,
        grid=(BATCH, SEQ // tg),
        in_specs=[row3(GLA_KEY), row3(GLA_KEY), row3(GLA_KEY), row3(GLA_VAL), row3(GLA_VAL),
                  const((1, GLA_DV)), const((2 * c, c))],
        out_specs=row3(GLA_VAL),
        out_shape=jax.ShapeDtypeStruct((BATCH, SEQ, GLA_VAL), F32),
        scratch_shapes=[pltpu.VMEM((GLA_HEADS, GLA_DV, GLA_DK), F32)],
        compiler_params=_params(("parallel", "arbitrary")),
        name="gla",
    )(q, k, la, v, rs, norm_g.reshape(1, GLA_DV), tt)


def _merge_kernel(ya_ref, yb_ref, ga_ref, gb_ref, x_ref, g1_ref, sc2_ref, sh2_ref, g2_ref, n2_ref,
                  wglu_ref, bglu_ref, wpa_ref, wpb_ref, wout_ref, wr_ref, wsgu_ref, wsd_ref,
                  base_ref, hp_ref, sc_ref):
    rows = MERGE_TM // MERGE_SPLIT
    subs = [slice(s * rows, (s + 1) * rows) for s in range(MERGE_SPLIT)]
    ya = [ya_ref[sl, :] for sl in subs]
    z = [_dot(a.astype(BF16), wglu_ref[...]) for a in ya]
    pb = [_dot(yb_ref[0, sl, :].astype(BF16), wpb_ref[...]) for sl in subs]
    ya2 = [(a * _sigmoid(zz + bglu_ref[...])).astype(BF16) for a, zz in zip(ya, z)]
    pa = [_dot(a, wpa_ref[...]) for a in ya2]
    mixed = [(ga_ref[0, sl, :] * p + gb_ref[0, sl, :] * q).astype(BF16)
             for sl, p, q in zip(subs, pa, pb)]
    mo = [_dot(m, wout_ref[...]) for m in mixed]
    x1 = [x_ref[0, sl, :] + g1_ref[0] * m for sl, m in zip(subs, mo)]
    h = [(x * lax.rsqrt(jnp.mean(x * x, axis=-1, keepdims=True) + EPS)) * n2_ref[...]
         * (1.0 + sc2_ref[0]) + sh2_ref[0] for x in x1]
    hb = [v.astype(BF16) for v in h]
    logits = [_dot(v, wr_ref[...]) for v in hb]
    gu = [_dot(v, wsgu_ref[...]) for v in hb]
    mid = [(g[:, :EXPERT_FF] * _sigmoid(g[:, :EXPERT_FF]) * g[:, EXPERT_FF:]).astype(BF16) for g in gu]
    shared = [_dot(m, wsd_ref[...]) for m in mid]
    for s, sl in enumerate(subs):
        sc_ref[sl, :] = _sigmoid(logits[s])
        base_ref[sl] = (x1[s] + g2_ref[0] * shared[s]).reshape(rows, *TOKEN_TILE)
        hp_ref[sl] = h[s].reshape(rows, *TOKEN_TILE)


def _merge(ya_tb, yb, ga, gb, x, g1, sc2, sh2, g2, n2, wglu, bglu, wpa, wpb, wout, wr, wsgu, wsd):
    tm = MERGE_TM
    row3 = lambda w: pl.BlockSpec((1, tm, w), lambda b, i: (b, i, 0))
    const = lambda a: pl.BlockSpec(a.shape, lambda b, i: tuple(0 for _ in a.shape))
    mod = pl.BlockSpec((1, 1, D_MODEL), lambda b, i: (b, 0, 0))
    nt = SEQ // tm
    flat = lambda w: pl.BlockSpec((tm, w), lambda b, i: (b * nt + i, 0))
    tiles = pl.BlockSpec((tm,) + TOKEN_TILE, lambda b, i: (b * nt + i, 0, 0))
    return pl.pallas_call(
        _merge_kernel,
        grid=(BATCH, nt),
        in_specs=[pl.BlockSpec((tm, S5_WIDTH), lambda b, i: (i, b)),
                  row3(GLA_VAL), row3(D_MODEL), row3(D_MODEL), row3(D_MODEL),
                  mod, mod, mod, mod, const(n2),
                  const(wglu), const(bglu), const(wpa), const(wpb), const(wout), const(wr),
                  const(wsgu), const(wsd)],
        out_specs=[tiles, tiles, flat(N_EXPERTS)],
        out_shape=[jax.ShapeDtypeStruct((TOKENS,) + TOKEN_TILE, F32),
                   jax.ShapeDtypeStruct((TOKENS,) + TOKEN_TILE, F32),
                   jax.ShapeDtypeStruct((TOKENS, N_EXPERTS), F32)],
        compiler_params=_params(("parallel", "parallel")),
        name="merge",
    )(ya_tb, yb, ga, gb, x, g1, sc2, sh2, g2, n2, wglu, bglu, wpa, wpb, wout, wr, wsgu, wsd)


def _first_argmax(x, iota, size):
    m = jnp.max(x, axis=0, keepdims=True)
    idx = jnp.min(jnp.where(x == m, iota, size), axis=0, keepdims=True)
    return m, idx


def _route_kernel(s_ref, bias_ref, tri_ref, e_ref, r_ref, w_ref, cnt_ref, carry_ref):
    @pl.when(pl.program_id(0) == 0)
    def _():
        carry_ref[...] = jnp.zeros_like(carry_ref)

    tr = ROUTE_TR
    neg = -jnp.inf
    s_t = s_ref[...].T
    biased = s_t + bias_ref[...]
    io_g = lax.broadcasted_iota(I32, (GROUP_SIZE, tr), 0)
    rows = []
    for g in range(N_GROUPS):
        xg = biased[g * GROUP_SIZE:(g + 1) * GROUP_SIZE, :]
        m1, i1 = _first_argmax(xg, io_g, GROUP_SIZE)
        m2 = jnp.max(jnp.where(io_g == i1, neg, xg), axis=0, keepdims=True)
        rows.append(m1 + m2)
    gs = jnp.concatenate(rows, axis=0)
    io_n = lax.broadcasted_iota(I32, (N_GROUPS, tr), 0)
    gsel = jnp.zeros((N_GROUPS, tr), F32)
    for _ in range(TOPK_GROUPS):
        _, gi = _first_argmax(gs, io_n, N_GROUPS)
        hit = io_n == gi
        gsel = jnp.where(hit, 1.0, gsel)
        gs = jnp.where(hit, neg, gs)
    masked = jnp.concatenate(
        [jnp.where(gsel[g:g + 1, :] > 0.0, biased[g * GROUP_SIZE:(g + 1) * GROUP_SIZE, :], neg)
         for g in range(N_GROUPS)], axis=0)
    io_e = lax.broadcasted_iota(I32, (N_EXPERTS, tr), 0)
    sel = jnp.zeros((N_EXPERTS, tr), F32)
    idxs = []
    for _ in range(TOP_K):
        _, ei = _first_argmax(masked, io_e, N_EXPERTS)
        hit = io_e == ei
        sel = jnp.where(hit, 1.0, sel)
        masked = jnp.where(hit, neg, masked)
        idxs.append(ei)
    den = jnp.sum(sel * s_t, axis=0, keepdims=True)
    rank = _dot(sel.astype(BF16), tri_ref[...]) + carry_ref[...]
    rks, sks = [], []
    for kk in range(TOP_K):
        hit = io_e == idxs[kk]
        rks.append(jnp.sum(jnp.where(hit, rank, 0.0), axis=0, keepdims=True))
        sks.append(jnp.sum(jnp.where(hit, s_t, 0.0), axis=0, keepdims=True))
    e_ref[...] = jnp.concatenate(idxs, axis=0)
    r_ref[...] = jnp.concatenate(rks, axis=0).astype(I32)
    w_ref[...] = jnp.concatenate(sks, axis=0) / den * ROUTE_SCALE
    carry_ref[...] += jnp.sum(sel, axis=1, keepdims=True)
    cnt_ref[...] = carry_ref[...]


def _route(scores, bias):
    tr = ROUTE_TR
    r = jnp.arange(tr)
    tri = (r[:, None] < r[None, :]).astype(BF16)
    kt = lambda dt: jax.ShapeDtypeStruct((TOP_K, TOKENS), dt)
    blk = pl.BlockSpec((TOP_K, tr), lambda i: (0, i))
    return pl.pallas_call(
        _route_kernel,
        grid=(TOKENS // tr,),
        in_specs=[pl.BlockSpec((tr, N_EXPERTS), lambda i: (i, 0)),
                  pl.BlockSpec((N_EXPERTS, 1), lambda i: (0, 0)),
                  pl.BlockSpec((tr, tr), lambda i: (0, 0))],
        out_specs=[blk, blk, blk, pl.BlockSpec((N_EXPERTS, 1), lambda i: (0, 0))],
        out_shape=[kt(I32), kt(I32), kt(F32), jax.ShapeDtypeStruct((N_EXPERTS, 1), F32)],
        scratch_shapes=[pltpu.VMEM((N_EXPERTS, 1), F32)],
        compiler_params=_params(("arbitrary",)),
        name="route",
    )(scores, bias.reshape(N_EXPERTS, 1), tri)


def _slots_kernel(e_ref, r_ref, ps_ref, d_ref):
    tr = ROUTE_TR
    io_e = lax.broadcasted_iota(I32, (N_EXPERTS, tr), 0)
    ps = ps_ref[...]
    rows = []
    for kk in range(TOP_K):
        hit = io_e == e_ref[kk:kk + 1, :]
        rows.append(jnp.sum(jnp.where(hit, ps, 0.0), axis=0, keepdims=True))
    d_ref[...] = jnp.concatenate(rows, axis=0).astype(I32) + r_ref[...]


def _slots(e_idx, rank, pad_start):
    tr = ROUTE_TR
    blk = pl.BlockSpec((TOP_K, tr), lambda i: (0, i))
    return pl.pallas_call(
        _slots_kernel,
        grid=(TOKENS // tr,),
        in_specs=[blk, blk, pl.BlockSpec((N_EXPERTS, 1), lambda i: (0, 0))],
        out_specs=blk,
        out_shape=jax.ShapeDtypeStruct((TOP_K, TOKENS), I32),
        compiler_params=_params(("parallel",)),
        name="slots",
    )(e_idx, rank, pad_start.astype(F32).reshape(N_EXPERTS, 1))


def _dispatch_kernel(ps_ref, cnt_ref, nb_ref, dest_ref, h_ref, xs_ref, zbuf, sem, zsem):
    tq = DISPATCH_TQ
    step = pl.program_id(0)
    n_used = ps_ref[N_EXPERTS - 1] // MOE_BLOCK + nb_ref[N_EXPERTS - 1]

    def zero_fill(act):
        def block(j, carry):
            r = pl.ds(pl.multiple_of(j * MOE_BLOCK, MOE_BLOCK), MOE_BLOCK)
            act(pltpu.make_async_copy(zbuf, xs_ref.at[r], zsem))
            return carry

        def expert(e, carry):
            @pl.when(cnt_ref[e] < nb_ref[e] * MOE_BLOCK)
            def _():
                block(ps_ref[e] // MOE_BLOCK + nb_ref[e] - 1, 0)
            return carry

        lax.fori_loop(0, N_EXPERTS, expert, 0)
        lax.fori_loop(n_used, N_BLOCKS, block, 0)

    @pl.when(step == 0)
    def _():
        zbuf[...] = jnp.zeros_like(zbuf)
        zero_fill(lambda cp: cp.start())
        zero_fill(lambda cp: cp.wait())

    def start(i, carry):
        for kk in range(TOP_K):
            d = dest_ref[i * TOP_K + kk]
            pltpu.make_async_copy(h_ref.at[i], xs_ref.at[d],
                                  sem).start(priority=kk % 2)
        return carry

    lax.fori_loop(0, tq, start, 0, unroll=2)
    for _ in range(TOP_K):
        pltpu.make_async_copy(h_ref, xs_ref.at[pl.ds(0, tq)], sem).wait()


def _dispatch(pad_start, counts, n_blk, dest, hp):
    tq = DISPATCH_TQ
    grid_spec = pltpu.PrefetchScalarGridSpec(
        num_scalar_prefetch=3,
        grid=(TOKENS // tq,),
        in_specs=[pl.BlockSpec((tq * TOP_K,), lambda i, *_: (i,), memory_space=pltpu.SMEM),
                  pl.BlockSpec((tq,) + TOKEN_TILE, lambda i, *_: (i, 0, 0))],
        out_specs=pl.BlockSpec(memory_space=pl.ANY),
        scratch_shapes=[pltpu.VMEM((MOE_BLOCK,) + TOKEN_TILE, F32),
                        pltpu.SemaphoreType.DMA, pltpu.SemaphoreType.DMA],
    )
    return pl.pallas_call(
        _dispatch_kernel,
        grid_spec=grid_spec,
        out_shape=jax.ShapeDtypeStruct((N_SLOTS,) + TOKEN_TILE, F32),
        compiler_params=_params(("arbitrary",)),
        name="dispatch",
    )(pad_start, counts, n_blk, dest, hp)


def _expert_kernel(ps_ref, nb_ref, wg_ref, wu_ref, wd_ref, xs_ref, ys_ref,
                   xbuf, ybuf, wgus, wds, xsem, ysem):
    e = pl.program_id(0)
    nb = nb_ref[e]
    first = ps_ref[e] // MOE_BLOCK
    n_used = ps_ref[N_EXPERTS - 1] // MOE_BLOCK + nb_ref[N_EXPERTS - 1]
    nbuf = EXPERT_NBUF

    def rows(g):
        return pl.ds(pl.multiple_of(g * MOE_BLOCK, MOE_BLOCK), MOE_BLOCK)

    def x_copy(g):
        slot = g % nbuf
        return pltpu.make_async_copy(xs_ref.at[rows(g)], xbuf.at[slot], xsem.at[slot])

    def y_copy(g):
        slot = g % nbuf
        return pltpu.make_async_copy(ybuf.at[slot], ys_ref.at[rows(g)], ysem.at[slot])

    look = nbuf - EXPERT_UNIT

    @pl.when(e == 0)
    def _():
        for g in range(look):
            @pl.when(g < n_used)
            def _():
                x_copy(g).start()

    def run(g, u):
        for d in range(u):
            @pl.when(g + look + d < n_used)
            def _():
                x_copy(g + look + d).start()

        for d in range(u):
            b = g + d

            @pl.when(b >= nbuf)
            def _():
                y_copy(b - nbuf).wait()

        for d in range(u):
            x_copy(g + d).wait()
        gus = [_dot(xbuf[(g + d) % nbuf].reshape(MOE_BLOCK, D_MODEL).astype(BF16), wgus[...])
               for d in range(u)]
        for d in range(u):
            gate = gus[d][:, :EXPERT_FF]
            mid = (gate * _sigmoid(gate) * gus[d][:, EXPERT_FF:]).astype(BF16)
            ybuf[(g + d) % nbuf] = _dot(mid, wds[...]).reshape(MOE_BLOCK, *TOKEN_TILE)
        for d in range(u):
            y_copy(g + d).start()

    @pl.when(nb > 0)
    def _():
        wgus[:, :EXPERT_FF] = wg_ref[0].astype(BF16)
        wgus[:, EXPERT_FF:] = wu_ref[0].astype(BF16)
        wds[...] = wd_ref[0].astype(BF16)

        def unit(j, carry):
            run(first + j * EXPERT_UNIT, EXPERT_UNIT)
            return carry

        lax.fori_loop(0, nb // EXPERT_UNIT, unit, 0)
        for rem in range(1, EXPERT_UNIT):
            @pl.when(nb % EXPERT_UNIT == rem)
            def _():
                run(first + nb - rem, rem)

    @pl.when(e == N_EXPERTS - 1)
    def _():
        for d in range(nbuf):
            g = n_used - nbuf + d

            @pl.when(g >= 0)
            def _():
                y_copy(g).wait()

        ybuf[0] = jnp.zeros((MOE_BLOCK,) + TOKEN_TILE, F32)

        def z_copy(j):
            return pltpu.make_async_copy(ybuf.at[0], ys_ref.at[rows(j)], ysem.at[0])

        def z_start(j, carry):
            z_copy(j).start()
            return carry

        def z_wait(j, carry):
            z_copy(j).wait()
            return carry

        lax.fori_loop(n_used, N_BLOCKS, z_start, 0)
        lax.fori_loop(n_used, N_BLOCKS, z_wait, 0)


def _experts(pad_start, n_blk, xs, wg, wu, wd):
    wspec = lambda shape: pl.BlockSpec((1,) + shape, lambda e, ps, nb: (e, 0, 0))
    grid_spec = pltpu.PrefetchScalarGridSpec(
        num_scalar_prefetch=2,
        grid=(N_EXPERTS,),
        in_specs=[wspec((D_MODEL, EXPERT_FF)), wspec((D_MODEL, EXPERT_FF)),
                  wspec((EXPERT_FF, D_MODEL)), pl.BlockSpec(memory_space=pl.ANY)],
        out_specs=pl.BlockSpec(memory_space=pl.ANY),
        scratch_shapes=[pltpu.VMEM((EXPERT_NBUF, MOE_BLOCK) + TOKEN_TILE, F32),
                        pltpu.VMEM((EXPERT_NBUF, MOE_BLOCK) + TOKEN_TILE, F32),
                        pltpu.VMEM((D_MODEL, 2 * EXPERT_FF), BF16),
                        pltpu.VMEM((EXPERT_FF, D_MODEL), BF16),
                        pltpu.SemaphoreType.DMA((EXPERT_NBUF,)),
                        pltpu.SemaphoreType.DMA((EXPERT_NBUF,))],
    )
    return pl.pallas_call(
        _expert_kernel,
        grid_spec=grid_spec,
        out_shape=jax.ShapeDtypeStruct((N_SLOTS,) + TOKEN_TILE, F32),
        compiler_params=_params(("arbitrary",)),
        name="experts",
    )(pad_start, n_blk, wg, wu, wd, xs)


def _combine_kernel(dcur_ref, dnext_ref, w_ref, base_ref, g2_ref, fg_ref, ys_ref, o_ref,
                    buf, acc_ref, sem):
    tq = COMBINE_TQ
    i = pl.program_id(0)
    n = pl.num_programs(0)

    def start(dest_ref, to_slot, t):
        for kk in range(TOP_K):
            d = dest_ref[t * TOP_K + kk]
            pltpu.make_async_copy(ys_ref.at[d], buf.at[to_slot, kk, t],
                                  sem.at[to_slot]).start(priority=kk % 2)

    def reduce(slot, t):
        a = buf[slot, 0, t] * w_ref[t * TOP_K]
        for kk in range(1, TOP_K):
            a = a + buf[slot, kk, t] * w_ref[t * TOP_K + kk]
        acc_ref[t] = a

    @pl.when(i == 0)
    def _():
        lax.fori_loop(0, tq, lambda t, c: start(dcur_ref, 0, t) or c, 0, unroll=4)

    for slot in range(2):
        @pl.when(i % 2 == slot)
        def _(slot=slot):
            @pl.when(i + 1 < n)
            def _():
                lax.fori_loop(0, tq, lambda t, c: start(dnext_ref, 1 - slot, t) or c, 0, unroll=4)

            for kk in range(TOP_K):
                pltpu.make_async_copy(ys_ref.at[pl.ds(0, tq)], buf.at[slot, kk],
                                      sem.at[slot]).wait()
            lax.fori_loop(0, tq, lambda t, c: reduce(slot, t) or c, 0, unroll=8)

    xo = base_ref[...] + g2_ref[0] * acc_ref[...]
    ms = jnp.sum(jnp.sum(xo * xo, axis=2, keepdims=True), axis=1, keepdims=True) * (1.0 / D_MODEL)
    o_ref[0] = ((xo * lax.rsqrt(ms + EPS)) * fg_ref[...]).reshape(tq, D_MODEL)


def _combine(dest, w_flat, base, g2, final_g, ys):
    tq = COMBINE_TQ
    nt = SEQ // tq
    n = TOKENS // tq
    smem = lambda imap: pl.BlockSpec((tq * TOP_K,), imap, memory_space=pltpu.SMEM)
    return pl.pallas_call(
        _combine_kernel,
        grid=(n,),
        in_specs=[smem(lambda i: (i,)), smem(lambda i: (jnp.minimum(i + 1, n - 1),)),
                  smem(lambda i: (i,)),
                  pl.BlockSpec((tq,) + TOKEN_TILE, lambda i: (i, 0, 0)),
                  pl.BlockSpec((1,) + TOKEN_TILE, lambda i: (i // nt, 0, 0)),
                  pl.BlockSpec(TOKEN_TILE, lambda i: (0, 0)),
                  pl.BlockSpec(memory_space=pl.ANY)],
        out_specs=pl.BlockSpec((1, tq, D_MODEL), lambda i: (i // nt, i % nt, 0)),
        out_shape=jax.ShapeDtypeStruct((BATCH, SEQ, D_MODEL), F32),
        scratch_shapes=[pltpu.VMEM((2, TOP_K, tq) + TOKEN_TILE, F32),
                        pltpu.VMEM((tq,) + TOKEN_TILE, F32),
                        pltpu.SemaphoreType.DMA((2,))],
        compiler_params=_params(("arbitrary",)),
        name="combine",
    )(dest, dest, w_flat, base, g2.reshape((BATCH,) + TOKEN_TILE), final_g.reshape(TOKEN_TILE), ys)


def kernel(x, c, ada_w, ada_b, norm1_g, w_in, s5_lam_re, s5_lam_im, s5_log_dt, s5_b_re, s5_b_im,
           s5_c_re, s5_c_im, s5_d, s5_w_glu, s5_b_glu, w_proj_a, gla_w_gk2, gla_b_gk2, gla_norm_g,
           w_proj_b, w_out, norm2_g, router_w, router_bias, exp_w_gate, exp_w_up, exp_w_down,
           sh_w_gate, sh_w_up, sh_w_down, final_g):
    l = 0
    mod = _ada(c, ada_w[l], ada_b[l])
    sh1, sc1, g1, sh2, sc2, g2 = [m.reshape(BATCH, 1, D_MODEL) for m in jnp.split(mod, 6, axis=-1)]

    w = w_in[l]
    gk0 = _V0 + GLA_VAL
    w_main = jnp.concatenate([w[:, :gk0], w[:, gk0 + GLA_GATE_RANK:]], axis=1).astype(BF16)
    w_gk = jnp.pad(w[:, gk0:gk0 + GLA_GATE_RANK], ((0, 0), (0, LANES - GLA_GATE_RANK))).astype(BF16)
    w_gk2 = jnp.pad(gla_w_gk2[l], ((0, LANES - GLA_GATE_RANK), (0, 0))).astype(BF16)
    u_tb, q, k, v, la, rs, ga, gb = _inproj(
        x, norm1_g[l].reshape(1, D_MODEL), sc1, sh1, w_main, w_gk, w_gk2,
        gla_b_gk2[l].reshape(1, GLA_KEY))
    tables = _s5_tables(s5_lam_re[l], s5_lam_im[l], s5_log_dt[l], s5_b_re[l], s5_b_im[l],
                        s5_c_re[l], s5_c_im[l], s5_d[l])
    ya_tb = _s5(u_tb, *tables)
    yb = _gla(q, k, la, v, rs, gla_norm_g[l])

    wsgu = jnp.concatenate([sh_w_gate[l], sh_w_up[l]], axis=1).astype(BF16)
    base, hp, scores = _merge(
        ya_tb, yb, ga, gb, x, g1, sc2, sh2, g2,
        norm2_g[l].reshape(1, D_MODEL), s5_w_glu[l].astype(BF16), s5_b_glu[l].reshape(1, S5_WIDTH),
        w_proj_a[l].astype(BF16), w_proj_b[l].astype(BF16), w_out[l].astype(BF16),
        router_w[l].astype(BF16), wsgu, sh_w_down[l].astype(BF16))

    e_idx, rank, w_k, cnt = _route(scores, router_bias[l])
    counts = cnt[:, 0].astype(I32)
    padded = (counts + MOE_BLOCK - 1) // MOE_BLOCK * MOE_BLOCK
    pad_end = jnp.cumsum(padded)
    pad_start = pad_end - padded
    dest = _slots(e_idx, rank, pad_start).T.reshape(-1)
    pad_start = pad_start.astype(I32)
    n_blk = (padded // MOE_BLOCK).astype(I32)
    xs = _dispatch(pad_start, counts, n_blk, dest, hp)
    ys = _experts(pad_start, n_blk, xs, exp_w_gate[l], exp_w_up[l], exp_w_down[l])
    return _combine(dest, w_k.T.reshape(-1), base, g2, final_g, ys)
```

```python
import functools

import jax
import jax.numpy as jnp
from jax import lax
from jax.experimental import pallas as pl
from jax.experimental.pallas import tpu as pltpu

F32 = jnp.float32
BF16 = jnp.bfloat16
I32 = jnp.int32

D_MODEL = 1024
BATCH = 8
SEQ = 2048
TOKENS = BATCH * SEQ
S5_WIDTH = 512
S5_GROUP = 16
S5_GROUPS = 32
S5_STATE = 64
S5_COLS = S5_GROUPS * S5_STATE
GLA_HEADS = 4
GLA_DK = 64
GLA_DV = 128
GLA_KEY = GLA_HEADS * GLA_DK
GLA_VAL = GLA_HEADS * GLA_DV
GLA_GATE_RANK = 16
GLA_GATE_TAU = 16.0
GLA_CHUNK = 64
GLA_SUB = 16
N_EXPERTS = 256
TOP_K = 8
N_GROUPS = 8
GROUP_SIZE = N_EXPERTS // N_GROUPS
TOPK_GROUPS = 4
EXPERT_FF = 256
ROUTE_SCALE = 2.5
MOE_BLOCK = 128
EPS = 1e-6
N_SLOTS = -(-(TOKENS * TOP_K + N_EXPERTS * (MOE_BLOCK - 1)) // MOE_BLOCK) * MOE_BLOCK
N_BLOCKS = N_SLOTS // MOE_BLOCK

LANES = 128
SUBLANES = 8
TOKEN_TILE = (SUBLANES, LANES)
assert D_MODEL == SUBLANES * LANES
VMEM_LIMIT = 56 * 1024 * 1024

ADA_TN = 1536
INPROJ_TM = 512
S5_TC = 128
S5_RB = 32
S5_COL_CHUNK = 512
GLA_TG = 512
MERGE_TM = 512
MERGE_SPLIT = 2
ROUTE_TR = 512
DISPATCH_TQ = 1024
COMBINE_TQ = 256
EXPERT_UNIT = 3
EXPERT_NBUF = 8
EXP_CLAMP = 60.0


def _params(sem, vmem=VMEM_LIMIT):
    return pltpu.CompilerParams(dimension_semantics=sem, vmem_limit_bytes=vmem)


def _dot(a, b):
    return jnp.dot(a, b, preferred_element_type=F32)


def _dot_nt(a, b):
    return lax.dot_general(a, b, (((1,), (1,)), ((), ())), preferred_element_type=F32)


def _sigmoid(x):
    return jax.nn.sigmoid(x)


def _ada_kernel(c_ref, w_ref, b_ref, o_ref):
    c = c_ref[...]
    s = (c * _sigmoid(c)).astype(BF16)
    o_ref[...] = _dot(s, w_ref[...].astype(BF16)) + b_ref[...]


def _ada(c, w, b):
    n = w.shape[1]
    return pl.pallas_call(
        _ada_kernel,
        grid=(n // ADA_TN,),
        in_specs=[
            pl.BlockSpec((BATCH, D_MODEL), lambda j: (0, 0)),
            pl.BlockSpec((D_MODEL, ADA_TN), lambda j: (0, j)),
            pl.BlockSpec((1, ADA_TN), lambda j: (0, j)),
        ],
        out_specs=pl.BlockSpec((BATCH, ADA_TN), lambda j: (0, j)),
        out_shape=jax.ShapeDtypeStruct((BATCH, n), F32),
        compiler_params=_params(("arbitrary",)),
        name="ada",
    )(c, w, b.reshape(1, n))


_U0, _Q0, _K0, _V0, _R0, _GA0, _GB0, _END = 0, 512, 768, 1024, 1536, 2048, 3072, 4096


def _inproj_kernel(x_ref, g_ref, sc_ref, sh_ref, wm_ref, wgk_ref, wgk2_ref, bgk_ref,
                   u_ref, q_ref, k_ref, v_ref, la_ref, rs_ref, ga_ref, gb_ref):
    x = x_ref[0]
    r = lax.rsqrt(jnp.mean(x * x, axis=-1, keepdims=True) + EPS)
    h = (x * r) * g_ref[...] * (1.0 + sc_ref[0]) + sh_ref[0]
    hb = h.astype(BF16)

    def seg(lo, hi):
        return _dot(hb, wm_ref[:, lo:hi])

    u_ref[...] = seg(_U0, _Q0)
    q_ref[0] = seg(_Q0, _K0) * (GLA_DK ** -0.5)
    k_ref[0] = seg(_K0, _V0)
    v_ref[0] = seg(_V0, _R0)
    rr = seg(_R0, _GA0)
    rs_ref[0] = rr * _sigmoid(rr)
    ga_ref[0] = _sigmoid(seg(_GA0, _GB0))
    gb_ref[0] = _sigmoid(seg(_GB0, _END))
    gk = _dot(hb, wgk_ref[...])
    z = _dot(gk.astype(BF16), wgk2_ref[...]) + bgk_ref[...]
    la_ref[0] = -(jnp.maximum(-z, 0.0) + jnp.log1p(jnp.exp(-jnp.abs(z)))) * (1.0 / GLA_GATE_TAU)


def _inproj(x, g, sc, sh, w_main, w_gk, w_gk2, b_gk2):
    tm = INPROJ_TM
    nt = SEQ // tm
    row3 = lambda w: pl.BlockSpec((1, tm, w), lambda b, i: (b, i, 0))
    const = lambda shape: pl.BlockSpec(shape, lambda b, i: tuple(0 for _ in shape))
    mod = pl.BlockSpec((1, 1, D_MODEL), lambda b, i: (b, 0, 0))
    bld = lambda w: jax.ShapeDtypeStruct((BATCH, SEQ, w), F32)
    return pl.pallas_call(
        _inproj_kernel,
        grid=(BATCH, nt),
        in_specs=[row3(D_MODEL), const((1, D_MODEL)), mod, mod,
                  const((D_MODEL, _END)), const((D_MODEL, LANES)), const((LANES, GLA_KEY)),
                  const((1, GLA_KEY))],
        out_specs=[pl.BlockSpec((tm, S5_WIDTH), lambda b, i: (i, b)),
                   row3(GLA_KEY), row3(GLA_KEY), row3(GLA_VAL), row3(GLA_KEY), row3(GLA_VAL),
                   row3(D_MODEL), row3(D_MODEL)],
        out_shape=[jax.ShapeDtypeStruct((SEQ, BATCH * S5_WIDTH), F32),
                   bld(GLA_KEY), bld(GLA_KEY), bld(GLA_VAL), bld(GLA_KEY), bld(GLA_VAL),
                   bld(D_MODEL), bld(D_MODEL)],
        compiler_params=_params(("parallel", "parallel")),
        name="inproj",
    )(x, g, sc, sh, w_main, w_gk, w_gk2, b_gk2)


def _s5_kernel(u_ref, bre_ref, bim_ref, cre_ref, cim_ref, are_ref, aim_ref, d_ref,
               y_ref, s_ref, st_ref):
    half = S5_COLS // 2

    @pl.when(pl.program_id(0) == 0)
    def _():
        st_ref[...] = jnp.zeros_like(st_ref)

    u = u_ref[...].reshape(S5_TC, BATCH, S5_WIDTH).reshape(S5_TC * BATCH, S5_WIDTH)
    ub = u.astype(BF16)
    nblk = S5_TC // S5_RB
    brows = [slice(r * S5_RB * BATCH, (r + 1) * S5_RB * BATCH) for r in range(nblk)]
    for r in range(nblk):
        for j in range(2):
            uj = ub[brows[r], j * 256:(j + 1) * 256]
            s_ref[brows[r], j * half:(j + 1) * half] = _dot(uj, bre_ref[j])
            s_ref[brows[r], S5_COLS + j * half:S5_COLS + (j + 1) * half] = _dot(uj, bim_ref[j])

    ncc = S5_COLS // S5_COL_CHUNK
    re_sl = [slice(cc * S5_COL_CHUNK, (cc + 1) * S5_COL_CHUNK) for cc in range(ncc)]
    im_sl = [slice(S5_COLS + cc * S5_COL_CHUNK, S5_COLS + (cc + 1) * S5_COL_CHUNK) for cc in range(ncc)]
    sr = [st_ref[:, sl] for sl in re_sl]
    si = [st_ref[:, sl] for sl in im_sl]
    for r in range(nblk):
        for t in range(r * S5_RB, (r + 1) * S5_RB):
            rows = slice(t * BATCH, (t + 1) * BATCH)
            for cc in range(ncc):
                ar = are_ref[:, re_sl[cc]]
                ai = aim_ref[:, re_sl[cc]]
                nr = ar * sr[cc] - ai * si[cc] + s_ref[rows, re_sl[cc]]
                ni = ar * si[cc] + ai * sr[cc] + s_ref[rows, im_sl[cc]]
                s_ref[rows, re_sl[cc]] = nr
                s_ref[rows, im_sl[cc]] = ni
                sr[cc], si[cc] = nr, ni
    for cc in range(ncc):
        st_ref[:, re_sl[cc]] = sr[cc]
        st_ref[:, im_sl[cc]] = si[cc]

    for r in range(nblk):
        ys = []
        for j in range(2):
            sre = s_ref[brows[r], j * half:(j + 1) * half].astype(BF16)
            sim = s_ref[brows[r], S5_COLS + j * half:S5_COLS + (j + 1) * half].astype(BF16)
            ys.append(_dot(sre, cre_ref[j]) + _dot(sim, cim_ref[j]))
        y = jnp.concatenate(ys, axis=1) + d_ref[...] * u[brows[r], :]
        y_ref[r * S5_RB:(r + 1) * S5_RB, :] = (
            jax.nn.gelu(y).reshape(S5_RB, BATCH, S5_WIDTH).reshape(S5_RB, BATCH * S5_WIDTH))


def _s5(u_tb, bre, bim, cre, cimn, are, aim, dflat):
    rows = S5_TC * BATCH
    const = lambda shape: pl.BlockSpec(shape, lambda i: tuple(0 for _ in shape))
    return pl.pallas_call(
        _s5_kernel,
        grid=(SEQ // S5_TC,),
        in_specs=[pl.BlockSpec((S5_TC, BATCH * S5_WIDTH), lambda i: (i, 0)),
                  const(bre.shape), const(bim.shape), const(cre.shape), const(cimn.shape),
                  const(are.shape), const(aim.shape), const(dflat.shape)],
        out_specs=pl.BlockSpec((S5_TC, BATCH * S5_WIDTH), lambda i: (i, 0)),
        out_shape=jax.ShapeDtypeStruct((SEQ, BATCH * S5_WIDTH), F32),
        scratch_shapes=[pltpu.VMEM((rows, 2 * S5_COLS), F32),
                        pltpu.VMEM((BATCH, 2 * S5_COLS), F32)],
        compiler_params=_params(("arbitrary",)),
        name="s5",
    )(u_tb, bre, bim, cre, cimn, are, aim, dflat)


def _s5_tables(lam_re, lam_im, log_dt, b_re, b_im, c_re, c_im, d_skip):
    lr = lam_re.astype(F32)
    li = lam_im.astype(F32)
    dt = jnp.exp(log_dt.astype(F32))[:, None]
    mag = jnp.exp(lr * dt)
    abar_re = mag * jnp.cos(li * dt)
    abar_im = mag * jnp.sin(li * dt)
    den = lr * lr + li * li
    num_re = abar_re - 1.0
    coef_re = (num_re * lr + abar_im * li) / den
    coef_im = (abar_im * lr - num_re * li) / den
    cr, ci = coef_re[..., None], coef_im[..., None]
    br, bi = b_re.astype(F32), b_im.astype(F32)
    bbar_re = cr * br - ci * bi
    bbar_im = cr * bi + ci * br
    eye = jnp.eye(S5_GROUPS // 2, dtype=F32)

    def in_map(bb):
        bb = bb.reshape(2, S5_GROUPS // 2, S5_STATE, S5_GROUP)
        return jnp.einsum('jgph,gk->jghkp', bb, eye).reshape(2, 256, S5_COLS // 2).astype(BF16)

    def out_map(cc):
        cc = cc.reshape(2, S5_GROUPS // 2, S5_GROUP, S5_STATE)
        return jnp.einsum('jghp,gk->jgpkh', cc, eye).reshape(2, S5_COLS // 2, 256).astype(BF16)

    are = jnp.broadcast_to(abar_re.reshape(1, S5_COLS), (BATCH, S5_COLS))
    aim = jnp.broadcast_to(abar_im.reshape(1, S5_COLS), (BATCH, S5_COLS))
    return (in_map(bbar_re), in_map(bbar_im), out_map(c_re.astype(F32)), out_map(-c_im.astype(F32)),
            are, aim, d_skip.astype(F32).reshape(1, S5_WIDTH))


def _gla_kernel(q_ref, k_ref, la_ref, v_ref, rs_ref, ng_ref, tt_ref, o_ref, st_ref):
    @pl.when(pl.program_id(1) == 0)
    def _():
        st_ref[...] = jnp.zeros_like(st_ref)

    c = GLA_CHUNK
    tt = tt_ref[...]
    row = lax.broadcasted_iota(I32, (c, GLA_KEY), 0)
    ri = lax.broadcasted_iota(I32, (c, c), 0)
    ci = lax.broadcasted_iota(I32, (c, c), 1)
    causal = ci <= ri
    nsub = c // GLA_SUB

    heads = range(GLA_HEADS)
    hsl = [slice(h * GLA_DK, (h + 1) * GLA_DK) for h in heads]
    vsl = [slice(h * GLA_DV, (h + 1) * GLA_DV) for h in heads]
    chunks = range(GLA_TG // c)
    intra, kv, qdec, dec = [], [], [], []

    sls = [slice(ch * c, (ch + 1) * c) for ch in chunks]
    csum = []
    for ch in chunks:
        g = la_ref[0, sls[ch], :]
        g1 = g.astype(BF16)
        r1 = g - g1.astype(F32)
        g2 = r1.astype(BF16)
        g3 = (r1 - g2.astype(F32)).astype(BF16)
        csum.append(_dot(tt, g1) + _dot(tt, g2) + _dot(tt, g3))
    qbigs, kbigs, kdec = [], [], []
    for ch in chunks:
        sl = sls[ch]
        cs = csum[ch]
        b = cs[0:c]
        cl = cs[c:2 * c]
        ref_pt = b - cl
        q = q_ref[0, sl, :]
        k = k_ref[0, sl, :]
        qt = q * jnp.exp(cl)
        qe = (q * jnp.exp(b)).astype(BF16)
        blast = b[c - 1:c, :]
        ks = (k * jnp.exp(blast - b)).astype(BF16)
        k_sub = []
        q_sub = []
        for s in range(nsub):
            rs_ = ref_pt[s * GLA_SUB:s * GLA_SUB + 1, :]
            k_sub.append(k * jnp.exp(jnp.minimum(rs_ - b, EXP_CLAMP)))
            q_sub.append(jnp.where((row >= s * GLA_SUB) & (row < (s + 1) * GLA_SUB), qt, 0.0))
        qbigs.append([jnp.concatenate([x[:, hsl[h]] for x in q_sub], axis=1).astype(BF16)
                      for h in heads])
        kbigs.append([jnp.concatenate([x[:, hsl[h]] for x in k_sub], axis=1).astype(BF16)
                      for h in heads])
        kdec.append(ks)
        qdec.append(qe)
        dec.append(jnp.exp(blast))
    scores = [[_dot_nt(qbigs[ch][h], kbigs[ch][h]) for h in heads] for ch in chunks]
    for ch in chunks:
        kv.append([_dot(v_ref[0, sls[ch], vsl[h]].T.astype(BF16), kdec[ch][:, hsl[h]])
                   for h in heads])
    for ch in chunks:
        intra.append([_dot(jnp.where(causal, scores[ch][h], 0.0).astype(BF16),
                           v_ref[0, sls[ch], vsl[h]].astype(BF16)) for h in heads])

    st = [st_ref[h] for h in heads]
    for ch in chunks:
        sl = slice(ch * c, (ch + 1) * c)
        inter = [_dot_nt(qdec[ch][:, hsl[h]], st[h].astype(BF16)) for h in heads]
        for h in heads:
            o = inter[h] + intra[ch][h]
            st[h] = st[h] * dec[ch][:, hsl[h]] + kv[ch][h]
            on = o * lax.rsqrt(jnp.mean(o * o, axis=-1, keepdims=True) + EPS) * ng_ref[...]
            o_ref[0, sl, vsl[h]] = on * rs_ref[0, sl, vsl[h]]
    for h in heads:
        st_ref[h] = st[h]


def _gla(q, k, la, v, rs, norm_g):
    tg = GLA_TG
    c = GLA_CHUNK
    r = jnp.arange(c)
    tri = (r[None, :] <= r[:, None])
    blk = tri & ((r[None, :] // GLA_SUB) == (r[:, None] // GLA_SUB))
    tt = jnp.concatenate([tri, blk], axis=0).astype(BF16)
    row3 = lambda w: pl.BlockSpec((1, tg, w), lambda b, i: (b, i, 0))
    const = lambda shape: pl.BlockSpec(shape, lambda b, i: tuple(0 for _ in shape))
    return pl.pallas_call(
        _gla_kernel,
        grid=(BATCH, SEQ // tg),
        in_specs=[row3(GLA_KEY), row3(GLA_KEY), row3(GLA_KEY), row3(GLA_VAL), row3(GLA_VAL),
                  const((1, GLA_DV)), const((2 * c, c))],
        out_specs=row3(GLA_VAL),
        out_shape=jax.ShapeDtypeStruct((BATCH, SEQ, GLA_VAL), F32),
        scratch_shapes=[pltpu.VMEM((GLA_HEADS, GLA_DV, GLA_DK), F32)],
        compiler_params=_params(("parallel", "arbitrary")),
        name="gla",
    )(q, k, la, v, rs, norm_g.reshape(1, GLA_DV), tt)


def _merge_kernel(ya_ref, yb_ref, ga_ref, gb_ref, x_ref, g1_ref, sc2_ref, sh2_ref, g2_ref, n2_ref,
                  wglu_ref, bglu_ref, wpa_ref, wpb_ref, wout_ref, wr_ref, wsgu_ref, wsd_ref,
                  base_ref, hp_ref, sc_ref):
    rows = MERGE_TM // MERGE_SPLIT
    subs = [slice(s * rows, (s + 1) * rows) for s in range(MERGE_SPLIT)]
    ya = [ya_ref[sl, :] for sl in subs]
    z = [_dot(a.astype(BF16), wglu_ref[...]) for a in ya]
    pb = [_dot(yb_ref[0, sl, :].astype(BF16), wpb_ref[...]) for sl in subs]
    ya2 = [(a * _sigmoid(zz + bglu_ref[...])).astype(BF16) for a, zz in zip(ya, z)]
    pa = [_dot(a, wpa_ref[...]) for a in ya2]
    mixed = [(ga_ref[0, sl, :] * p + gb_ref[0, sl, :] * q).astype(BF16)
             for sl, p, q in zip(subs, pa, pb)]
    mo = [_dot(m, wout_ref[...]) for m in mixed]
    x1 = [x_ref[0, sl, :] + g1_ref[0] * m for sl, m in zip(subs, mo)]
    h = [(x * lax.rsqrt(jnp.mean(x * x, axis=-1, keepdims=True) + EPS)) * n2_ref[...]
         * (1.0 + sc2_ref[0]) + sh2_ref[0] for x in x1]
    hb = [v.astype(BF16) for v in h]
    logits = [_dot(v, wr_ref[...]) for v in hb]
    gu = [_dot(v, wsgu_ref[...]) for v in hb]
    mid = [(g[:, :EXPERT_FF] * _sigmoid(g[:, :EXPERT_FF]) * g[:, EXPERT_FF:]).astype(BF16) for g in gu]
    shared = [_dot(m, wsd_ref[...]) for m in mid]
    for s, sl in enumerate(subs):
        sc_ref[sl, :] = _sigmoid(logits[s])
        base_ref[sl] = (x1[s] + g2_ref[0] * shared[s]).reshape(rows, *TOKEN_TILE)
        hp_ref[sl] = h[s].reshape(rows, *TOKEN_TILE)


def _merge(ya_tb, yb, ga, gb, x, g1, sc2, sh2, g2, n2, wglu, bglu, wpa, wpb, wout, wr, wsgu, wsd):
    tm = MERGE_TM
    row3 = lambda w: pl.BlockSpec((1, tm, w), lambda b, i: (b, i, 0))
    const = lambda a: pl.BlockSpec(a.shape, lambda b, i: tuple(0 for _ in a.shape))
    mod = pl.BlockSpec((1, 1, D_MODEL), lambda b, i: (b, 0, 0))
    nt = SEQ // tm
    flat = lambda w: pl.BlockSpec((tm, w), lambda b, i: (b * nt + i, 0))
    tiles = pl.BlockSpec((tm,) + TOKEN_TILE, lambda b, i: (b * nt + i, 0, 0))
    return pl.pallas_call(
        _merge_kernel,
        grid=(BATCH, nt),
        in_specs=[pl.BlockSpec((tm, S5_WIDTH), lambda b, i: (i, b)),
                  row3(GLA_VAL), row3(D_MODEL), row3(D_MODEL), row3(D_MODEL),
                  mod, mod, mod, mod, const(n2),
                  const(wglu), const(bglu), const(wpa), const(wpb), const(wout), const(wr),
                  const(wsgu), const(wsd)],
        out_specs=[tiles, tiles, flat(N_EXPERTS)],
        out_shape=[jax.ShapeDtypeStruct((TOKENS,) + TOKEN_TILE, F32),
                   jax.ShapeDtypeStruct((TOKENS,) + TOKEN_TILE, F32),
                   jax.ShapeDtypeStruct((TOKENS, N_EXPERTS), F32)],
        compiler_params=_params(("parallel", "parallel")),
        name="merge",
    )(ya_tb, yb, ga, gb, x, g1, sc2, sh2, g2, n2, wglu, bglu, wpa, wpb, wout, wr, wsgu, wsd)


def _first_argmax(x, iota, size):
    m = jnp.max(x, axis=0, keepdims=True)
    idx = jnp.min(jnp.where(x == m, iota, size), axis=0, keepdims=True)
    return m, idx


def _route_kernel(s_ref, bias_ref, tri_ref, e_ref, r_ref, w_ref, cnt_ref, carry_ref):
    @pl.when(pl.program_id(0) == 0)
    def _():
        carry_ref[...] = jnp.zeros_like(carry_ref)

    tr = ROUTE_TR
    neg = -jnp.inf
    s_t = s_ref[...].T
    biased = s_t + bias_ref[...]
    io_g = lax.broadcasted_iota(I32, (GROUP_SIZE, tr), 0)
    rows = []
    for g in range(N_GROUPS):
        xg = biased[g * GROUP_SIZE:(g + 1) * GROUP_SIZE, :]
        m1, i1 = _first_argmax(xg, io_g, GROUP_SIZE)
        m2 = jnp.max(jnp.where(io_g == i1, neg, xg), axis=0, keepdims=True)
        rows.append(m1 + m2)
    gs = jnp.concatenate(rows, axis=0)
    io_n = lax.broadcasted_iota(I32, (N_GROUPS, tr), 0)
    gsel = jnp.zeros((N_GROUPS, tr), F32)
    for _ in range(TOPK_GROUPS):
        _, gi = _first_argmax(gs, io_n, N_GROUPS)
        hit = io_n == gi
        gsel = jnp.where(hit, 1.0, gsel)
        gs = jnp.where(hit, neg, gs)
    masked = jnp.concatenate(
        [jnp.where(gsel[g:g + 1, :] > 0.0, biased[g * GROUP_SIZE:(g + 1) * GROUP_SIZE, :], neg)
         for g in range(N_GROUPS)], axis=0)
    io_e = lax.broadcasted_iota(I32, (N_EXPERTS, tr), 0)
    sel = jnp.zeros((N_EXPERTS, tr), F32)
    idxs = []
    for _ in range(TOP_K):
        _, ei = _first_argmax(masked, io_e, N_EXPERTS)
        hit = io_e == ei
        sel = jnp.where(hit, 1.0, sel)
        masked = jnp.where(hit, neg, masked)
        idxs.append(ei)
    den = jnp.sum(sel * s_t, axis=0, keepdims=True)
    rank = _dot(sel.astype(BF16), tri_ref[...]) + carry_ref[...]
    rks, sks = [], []
    for kk in range(TOP_K):
        hit = io_e == idxs[kk]
        rks.append(jnp.sum(jnp.where(hit, rank, 0.0), axis=0, keepdims=True))
        sks.append(jnp.sum(jnp.where(hit, s_t, 0.0), axis=0, keepdims=True))
    e_ref[...] = jnp.concatenate(idxs, axis=0)
    r_ref[...] = jnp.concatenate(rks, axis=0).astype(I32)
    w_ref[...] = jnp.concatenate(sks, axis=0) / den * ROUTE_SCALE
    carry_ref[...] += jnp.sum(sel, axis=1, keepdims=True)
    cnt_ref[...] = carry_ref[...]


def _route(scores, bias):
    tr = ROUTE_TR
    r = jnp.arange(tr)
    tri = (r[:, None] < r[None, :]).astype(BF16)
    kt = lambda dt: jax.ShapeDtypeStruct((TOP_K, TOKENS), dt)
    blk = pl.BlockSpec((TOP_K, tr), lambda i: (0, i))
    return pl.pallas_call(
        _route_kernel,
        grid=(TOKENS // tr,),
        in_specs=[pl.BlockSpec((tr, N_EXPERTS), lambda i: (i, 0)),
                  pl.BlockSpec((N_EXPERTS, 1), lambda i: (0, 0)),
                  pl.BlockSpec((tr, tr), lambda i: (0, 0))],
        out_specs=[blk, blk, blk, pl.BlockSpec((N_EXPERTS, 1), lambda i: (0, 0))],
        out_shape=[kt(I32), kt(I32), kt(F32), jax.ShapeDtypeStruct((N_EXPERTS, 1), F32)],
        scratch_shapes=[pltpu.VMEM((N_EXPERTS, 1), F32)],
        compiler_params=_params(("arbitrary",)),
        name="route",
    )(scores, bias.reshape(N_EXPERTS, 1), tri)


def _slots_kernel(e_ref, r_ref, ps_ref, d_ref):
    tr = ROUTE_TR
    io_e = lax.broadcasted_iota(I32, (N_EXPERTS, tr), 0)
    ps = ps_ref[...]
    rows = []
    for kk in range(TOP_K):
        hit = io_e == e_ref[kk:kk + 1, :]
        rows.append(jnp.sum(jnp.where(hit, ps, 0.0), axis=0, keepdims=True))
    d_ref[...] = jnp.concatenate(rows, axis=0).astype(I32) + r_ref[...]


def _slots(e_idx, rank, pad_start):
    tr = ROUTE_TR
    blk = pl.BlockSpec((TOP_K, tr), lambda i: (0, i))
    return pl.pallas_call(
        _slots_kernel,
        grid=(TOKENS // tr,),
        in_specs=[blk, blk, pl.BlockSpec((N_EXPERTS, 1), lambda i: (0, 0))],
        out_specs=blk,
        out_shape=jax.ShapeDtypeStruct((TOP_K, TOKENS), I32),
        compiler_params=_params(("parallel",)),
        name="slots",
    )(e_idx, rank, pad_start.astype(F32).reshape(N_EXPERTS, 1))


def _dispatch_kernel(ps_ref, cnt_ref, nb_ref, dest_ref, h_ref, xs_ref, zbuf, sem, zsem):
    tq = DISPATCH_TQ
    step = pl.program_id(0)
    n_used = ps_ref[N_EXPERTS - 1] // MOE_BLOCK + nb_ref[N_EXPERTS - 1]

    def zero_fill(act):
        def block(j, carry):
            r = pl.ds(pl.multiple_of(j * MOE_BLOCK, MOE_BLOCK), MOE_BLOCK)
            act(pltpu.make_async_copy(zbuf, xs_ref.at[r], zsem))
            return carry

        def expert(e, carry):
            @pl.when(cnt_ref[e] < nb_ref[e] * MOE_BLOCK)
            def _():
                block(ps_ref[e] // MOE_BLOCK + nb_ref[e] - 1, 0)
            return carry

        lax.fori_loop(0, N_EXPERTS, expert, 0)
        lax.fori_loop(n_used, N_BLOCKS, block, 0)

    @pl.when(step == 0)
    def _():
        zbuf[...] = jnp.zeros_like(zbuf)
        zero_fill(lambda cp: cp.start())
        zero_fill(lambda cp: cp.wait())

    def start(i, carry):
        for kk in range(TOP_K):
            d = dest_ref[i * TOP_K + kk]
            pltpu.make_async_copy(h_ref.at[i], xs_ref.at[d],
                                  sem).start(priority=kk % 2)
        return carry

    lax.fori_loop(0, tq, start, 0, unroll=2)
    for _ in range(TOP_K):
        pltpu.make_async_copy(h_ref, xs_ref.at[pl.ds(0, tq)], sem).wait()


def _dispatch(pad_start, counts, n_blk, dest, hp):
    tq = DISPATCH_TQ
    grid_spec = pltpu.PrefetchScalarGridSpec(
        num_scalar_prefetch=3,
        grid=(TOKENS // tq,),
        in_specs=[pl.BlockSpec((tq * TOP_K,), lambda i, *_: (i,), memory_space=pltpu.SMEM),
                  pl.BlockSpec((tq,) + TOKEN_TILE, lambda i, *_: (i, 0, 0))],
        out_specs=pl.BlockSpec(memory_space=pl.ANY),
        scratch_shapes=[pltpu.VMEM((MOE_BLOCK,) + TOKEN_TILE, F32),
                        pltpu.SemaphoreType.DMA, pltpu.SemaphoreType.DMA],
    )
    return pl.pallas_call(
        _dispatch_kernel,
        grid_spec=grid_spec,
        out_shape=jax.ShapeDtypeStruct((N_SLOTS,) + TOKEN_TILE, F32),
        compiler_params=_params(("arbitrary",)),
        name="dispatch",
    )(pad_start, counts, n_blk, dest, hp)


def _expert_kernel(ps_ref, nb_ref, wg_ref, wu_ref, wd_ref, xs_ref, ys_ref,
                   xbuf, ybuf, wgus, wds, xsem, ysem):
    e = pl.program_id(0)
    nb = nb_ref[e]
    first = ps_ref[e] // MOE_BLOCK
    n_used = ps_ref[N_EXPERTS - 1] // MOE_BLOCK + nb_ref[N_EXPERTS - 1]
    nbuf = EXPERT_NBUF

    def rows(g):
        return pl.ds(pl.multiple_of(g * MOE_BLOCK, MOE_BLOCK), MOE_BLOCK)

    def x_copy(g):
        slot = g % nbuf
        return pltpu.make_async_copy(xs_ref.at[rows(g)], xbuf.at[slot], xsem.at[slot])

    def y_copy(g):
        slot = g % nbuf
        return pltpu.make_async_copy(ybuf.at[slot], ys_ref.at[rows(g)], ysem.at[slot])

    look = nbuf - EXPERT_UNIT

    @pl.when(e == 0)
    def _():
        for g in range(look):
            @pl.when(g < n_used)
            def _():
                x_copy(g).start()

    def run(g, u):
        for d in range(u):
            @pl.when(g + look + d < n_used)
            def _():
                x_copy(g + look + d).start()

        for d in range(u):
            b = g + d

            @pl.when(b >= nbuf)
            def _():
                y_copy(b - nbuf).wait()

        for d in range(u):
            x_copy(g + d).wait()
        gus = [_dot(xbuf[(g + d) % nbuf].reshape(MOE_BLOCK, D_MODEL).astype(BF16), wgus[...])
               for d in range(u)]
        for d in range(u):
            gate = gus[d][:, :EXPERT_FF]
            mid = (gate * _sigmoid(gate) * gus[d][:, EXPERT_FF:]).astype(BF16)
            ybuf[(g + d) % nbuf] = _dot(mid, wds[...]).reshape(MOE_BLOCK, *TOKEN_TILE)
        for d in range(u):
            y_copy(g + d).start()

    @pl.when(nb > 0)
    def _():
        wgus[:, :EXPERT_FF] = wg_ref[0].astype(BF16)
        wgus[:, EXPERT_FF:] = wu_ref[0].astype(BF16)
        wds[...] = wd_ref[0].astype(BF16)

        triple = (nb % 2 == 1) & (nb >= EXPERT_UNIT)
        pairs = jnp.where(triple, (nb - EXPERT_UNIT) // 2, nb // 2)

        def pair(j, carry):
            run(first + j * 2, 2)
            return carry

        lax.fori_loop(0, pairs, pair, 0)

        @pl.when(triple)
        def _():
            run(first + nb - EXPERT_UNIT, EXPERT_UNIT)

        @pl.when(nb == 1)
        def _():
            run(first, 1)

    @pl.when(e == N_EXPERTS - 1)
    def _():
        for d in range(nbuf):
            g = n_used - nbuf + d

            @pl.when(g >= 0)
            def _():
                y_copy(g).wait()

        ybuf[0] = jnp.zeros((MOE_BLOCK,) + TOKEN_TILE, F32)

        def z_copy(j):
            return pltpu.make_async_copy(ybuf.at[0], ys_ref.at[rows(j)], ysem.at[0])

        def z_start(j, carry):
            z_copy(j).start()
            return carry

        def z_wait(j, carry):
            z_copy(j).wait()
            return carry

        lax.fori_loop(n_used, N_BLOCKS, z_start, 0)
        lax.fori_loop(n_used, N_BLOCKS, z_wait, 0)


def _experts(pad_start, n_blk, xs, wg, wu, wd):
    wspec = lambda shape: pl.BlockSpec((1,) + shape, lambda e, ps, nb: (e, 0, 0))
    grid_spec = pltpu.PrefetchScalarGridSpec(
        num_scalar_prefetch=2,
        grid=(N_EXPERTS,),
        in_specs=[wspec((D_MODEL, EXPERT_FF)), wspec((D_MODEL, EXPERT_FF)),
                  wspec((EXPERT_FF, D_MODEL)), pl.BlockSpec(memory_space=pl.ANY)],
        out_specs=pl.BlockSpec(memory_space=pl.ANY),
        scratch_shapes=[pltpu.VMEM((EXPERT_NBUF, MOE_BLOCK) + TOKEN_TILE, F32),
                        pltpu.VMEM((EXPERT_NBUF, MOE_BLOCK) + TOKEN_TILE, F32),
                        pltpu.VMEM((D_MODEL, 2 * EXPERT_FF), BF16),
                        pltpu.VMEM((EXPERT_FF, D_MODEL), BF16),
                        pltpu.SemaphoreType.DMA((EXPERT_NBUF,)),
                        pltpu.SemaphoreType.DMA((EXPERT_NBUF,))],
    )
    return pl.pallas_call(
        _expert_kernel,
        grid_spec=grid_spec,
        out_shape=jax.ShapeDtypeStruct((N_SLOTS,) + TOKEN_TILE, F32),
        compiler_params=_params(("arbitrary",)),
        name="experts",
    )(pad_start, n_blk, wg, wu, wd, xs)


def _combine_kernel(dcur_ref, dnext_ref, w_ref, base_ref, g2_ref, fg_ref, ys_ref, o_ref,
                    buf, acc_ref, sem):
    tq = COMBINE_TQ
    i = pl.program_id(0)
    n = pl.num_programs(0)

    def start(dest_ref, to_slot, t):
        for kk in range(TOP_K):
            d = dest_ref[t * TOP_K + kk]
            pltpu.make_async_copy(ys_ref.at[d], buf.at[to_slot, kk, t],
                                  sem.at[to_slot]).start(priority=kk % 2)

    def reduce(slot, t):
        a = buf[slot, 0, t] * w_ref[t * TOP_K]
        for kk in range(1, TOP_K):
            a = a + buf[slot, kk, t] * w_ref[t * TOP_K + kk]
        acc_ref[t] = a

    @pl.when(i == 0)
    def _():
        lax.fori_loop(0, tq, lambda t, c: start(dcur_ref, 0, t) or c, 0, unroll=4)

    for slot in range(2):
        @pl.when(i % 2 == slot)
        def _(slot=slot):
            @pl.when(i + 1 < n)
            def _():
                lax.fori_loop(0, tq, lambda t, c: start(dnext_ref, 1 - slot, t) or c, 0, unroll=4)

            for kk in range(TOP_K):
                pltpu.make_async_copy(ys_ref.at[pl.ds(0, tq)], buf.at[slot, kk],
                                      sem.at[slot]).wait()
            lax.fori_loop(0, tq, lambda t, c: reduce(slot, t) or c, 0, unroll=8)

    xo = base_ref[...] + g2_ref[0] * acc_ref[...]
    ms = jnp.sum(jnp.sum(xo * xo, axis=2, keepdims=True), axis=1, keepdims=True) * (1.0 / D_MODEL)
    o_ref[0] = ((xo * lax.rsqrt(ms + EPS)) * fg_ref[...]).reshape(tq, D_MODEL)


def _combine(dest, w_flat, base, g2, final_g, ys):
    tq = COMBINE_TQ
    nt = SEQ // tq
    n = TOKENS // tq
    smem = lambda imap: pl.BlockSpec((tq * TOP_K,), imap, memory_space=pltpu.SMEM)
    return pl.pallas_call(
        _combine_kernel,
        grid=(n,),
        in_specs=[smem(lambda i: (i,)), smem(lambda i: (jnp.minimum(i + 1, n - 1),)),
                  smem(lambda i: (i,)),
                  pl.BlockSpec((tq,) + TOKEN_TILE, lambda i: (i, 0, 0)),
                  pl.BlockSpec((1,) + TOKEN_TILE, lambda i: (i // nt, 0, 0)),
                  pl.BlockSpec(TOKEN_TILE, lambda i: (0, 0)),
                  pl.BlockSpec(memory_space=pl.ANY)],
        out_specs=pl.BlockSpec((1, tq, D_MODEL), lambda i: (i // nt, i % nt, 0)),
        out_shape=jax.ShapeDtypeStruct((BATCH, SEQ, D_MODEL), F32),
        scratch_shapes=[pltpu.VMEM((2, TOP_K, tq) + TOKEN_TILE, F32),
                        pltpu.VMEM((tq,) + TOKEN_TILE, F32),
                        pltpu.SemaphoreType.DMA((2,))],
        compiler_params=_params(("arbitrary",)),
        name="combine",
    )(dest, dest, w_flat, base, g2.reshape((BATCH,) + TOKEN_TILE), final_g.reshape(TOKEN_TILE), ys)


def kernel(x, c, ada_w, ada_b, norm1_g, w_in, s5_lam_re, s5_lam_im, s5_log_dt, s5_b_re, s5_b_im,
           s5_c_re, s5_c_im, s5_d, s5_w_glu, s5_b_glu, w_proj_a, gla_w_gk2, gla_b_gk2, gla_norm_g,
           w_proj_b, w_out, norm2_g, router_w, router_bias, exp_w_gate, exp_w_up, exp_w_down,
           sh_w_gate, sh_w_up, sh_w_down, final_g):
    l = 0
    mod = _ada(c, ada_w[l], ada_b[l])
    sh1, sc1, g1, sh2, sc2, g2 = [m.reshape(BATCH, 1, D_MODEL) for m in jnp.split(mod, 6, axis=-1)]

    w = w_in[l]
    gk0 = _V0 + GLA_VAL
    w_main = jnp.concatenate([w[:, :gk0], w[:, gk0 + GLA_GATE_RANK:]], axis=1).astype(BF16)
    w_gk = jnp.pad(w[:, gk0:gk0 + GLA_GATE_RANK], ((0, 0), (0, LANES - GLA_GATE_RANK))).astype(BF16)
    w_gk2 = jnp.pad(gla_w_gk2[l], ((0, LANES - GLA_GATE_RANK), (0, 0))).astype(BF16)
    u_tb, q, k, v, la, rs, ga, gb = _inproj(
        x, norm1_g[l].reshape(1, D_MODEL), sc1, sh1, w_main, w_gk, w_gk2,
        gla_b_gk2[l].reshape(1, GLA_KEY))
    tables = _s5_tables(s5_lam_re[l], s5_lam_im[l], s5_log_dt[l], s5_b_re[l], s5_b_im[l],
                        s5_c_re[l], s5_c_im[l], s5_d[l])
    ya_tb = _s5(u_tb, *tables)
    yb = _gla(q, k, la, v, rs, gla_norm_g[l])

    wsgu = jnp.concatenate([sh_w_gate[l], sh_w_up[l]], axis=1).astype(BF16)
    base, hp, scores = _merge(
        ya_tb, yb, ga, gb, x, g1, sc2, sh2, g2,
        norm2_g[l].reshape(1, D_MODEL), s5_w_glu[l].astype(BF16), s5_b_glu[l].reshape(1, S5_WIDTH),
        w_proj_a[l].astype(BF16), w_proj_b[l].astype(BF16), w_out[l].astype(BF16),
        router_w[l].astype(BF16), wsgu, sh_w_down[l].astype(BF16))

    e_idx, rank, w_k, cnt = _route(scores, router_bias[l])
    counts = cnt[:, 0].astype(I32)
    padded = (counts + MOE_BLOCK - 1) // MOE_BLOCK * MOE_BLOCK
    pad_end = jnp.cumsum(padded)
    pad_start = pad_end - padded
    dest = _slots(e_idx, rank, pad_start).T.reshape(-1)
    pad_start = pad_start.astype(I32)
    n_blk = (padded // MOE_BLOCK).astype(I32)
    xs = _dispatch(pad_start, counts, n_blk, dest, hp)
    ys = _experts(pad_start, n_blk, xs, exp_w_gate[l], exp_w_up[l], exp_w_down[l])
    return _combine(dest, w_k.T.reshape(-1), base, g2, final_g, ys)
```

```python
import jax
import jax.numpy as jnp
from jax import lax
from jax.experimental import pallas as pl
from jax.experimental.pallas import tpu as pltpu

F32 = jnp.float32
BF16 = jnp.bfloat16
I32 = jnp.int32

D_MODEL = 1024
BATCH = 8
SEQ = 2048
TOKENS = BATCH * SEQ
S5_WIDTH = 512
S5_GROUP = 16
S5_GROUPS = 32
S5_STATE = 64
S5_COLS = S5_GROUPS * S5_STATE
GLA_HEADS = 4
GLA_DK = 64
GLA_DV = 128
GLA_KEY = GLA_HEADS * GLA_DK
GLA_VAL = GLA_HEADS * GLA_DV
GLA_GATE_RANK = 16
GLA_GATE_TAU = 16.0
GLA_CHUNK = 64
GLA_SUB = 16
N_EXPERTS = 256
TOP_K = 8
N_GROUPS = 8
GROUP_SIZE = N_EXPERTS // N_GROUPS
TOPK_GROUPS = 4
EXPERT_FF = 256
ROUTE_SCALE = 2.5
MOE_BLOCK = 128
EPS = 1e-6
N_SLOTS = -(-(TOKENS * TOP_K + N_EXPERTS * (MOE_BLOCK - 1)) // MOE_BLOCK) * MOE_BLOCK
N_BLOCKS = N_SLOTS // MOE_BLOCK

LANES = 128
SUBLANES = 8
TOKEN_TILE = (SUBLANES, LANES)
assert D_MODEL == SUBLANES * LANES
VMEM_LIMIT = 56 * 1024 * 1024

ADA_TN = 1536
INPROJ_TM = 512
S5_TC = 128
S5_RB = 32
S5_COL_CHUNK = 512
GLA_TG = 512
MERGE_TM = 512
MERGE_SPLIT = 2
ROUTE_TR = 512
DISPATCH_TQ = 1024
COMBINE_TQ = 256
EXPERT_UNIT = 4
EXPERT_NBUF = 10
EXP_CLAMP = 60.0


def _params(sem, vmem=VMEM_LIMIT):
    return pltpu.CompilerParams(dimension_semantics=sem, vmem_limit_bytes=vmem)


def _dot(a, b):
    return jnp.dot(a, b, preferred_element_type=F32)


def _dot_nt(a, b):
    return lax.dot_general(a, b, (((1,), (1,)), ((), ())), preferred_element_type=F32)


def _sigmoid(x):
    return jax.nn.sigmoid(x)


def _ada_kernel(c_ref, w_ref, b_ref, o_ref):
    c = c_ref[...]
    s = (c * _sigmoid(c)).astype(BF16)
    o_ref[...] = _dot(s, w_ref[...].astype(BF16)) + b_ref[...]


def _ada(c, w, b):
    n = w.shape[1]
    return pl.pallas_call(
        _ada_kernel,
        grid=(n // ADA_TN,),
        in_specs=[
            pl.BlockSpec((BATCH, D_MODEL), lambda j: (0, 0)),
            pl.BlockSpec((D_MODEL, ADA_TN), lambda j: (0, j)),
            pl.BlockSpec((1, ADA_TN), lambda j: (0, j)),
        ],
        out_specs=pl.BlockSpec((BATCH, ADA_TN), lambda j: (0, j)),
        out_shape=jax.ShapeDtypeStruct((BATCH, n), F32),
        compiler_params=_params(("arbitrary",)),
        name="ada",
    )(c, w, b.reshape(1, n))


_U0, _Q0, _K0, _V0, _R0, _GA0, _GB0, _END = 0, 512, 768, 1024, 1536, 2048, 3072, 4096


def _inproj_kernel(x_ref, g_ref, sc_ref, sh_ref, wm_ref, wgk_ref, wgk2_ref, bgk_ref,
                   u_ref, q_ref, k_ref, v_ref, la_ref, rs_ref, ga_ref, gb_ref):
    x = x_ref[0]
    r = lax.rsqrt(jnp.mean(x * x, axis=-1, keepdims=True) + EPS)
    h = (x * r) * g_ref[...] * (1.0 + sc_ref[0]) + sh_ref[0]
    hb = h.astype(BF16)

    def seg(lo, hi):
        return _dot(hb, wm_ref[:, lo:hi])

    u_ref[...] = seg(_U0, _Q0)
    q_ref[0] = seg(_Q0, _K0) * (GLA_DK ** -0.5)
    k_ref[0] = seg(_K0, _V0)
    v_ref[0] = seg(_V0, _R0)
    rr = seg(_R0, _GA0)
    rs_ref[0] = rr * _sigmoid(rr)
    ga_ref[0] = _sigmoid(seg(_GA0, _GB0))
    gb_ref[0] = _sigmoid(seg(_GB0, _END))
    gk = _dot(hb, wgk_ref[...])
    z = _dot(gk.astype(BF16), wgk2_ref[...]) + bgk_ref[...]
    la_ref[0] = -(jnp.maximum(-z, 0.0) + jnp.log1p(jnp.exp(-jnp.abs(z)))) * (1.0 / GLA_GATE_TAU)


def _inproj(x, g, sc, sh, w_main, w_gk, w_gk2, b_gk2):
    tm = INPROJ_TM
    nt = SEQ // tm
    row3 = lambda w: pl.BlockSpec((1, tm, w), lambda b, i: (b, i, 0))
    const = lambda shape: pl.BlockSpec(shape, lambda b, i: tuple(0 for _ in shape))
    mod = pl.BlockSpec((1, 1, D_MODEL), lambda b, i: (b, 0, 0))
    bld = lambda w: jax.ShapeDtypeStruct((BATCH, SEQ, w), F32)
    return pl.pallas_call(
        _inproj_kernel,
        grid=(BATCH, nt),
        in_specs=[row3(D_MODEL), const((1, D_MODEL)), mod, mod,
                  const((D_MODEL, _END)), const((D_MODEL, LANES)), const((LANES, GLA_KEY)),
                  const((1, GLA_KEY))],
        out_specs=[pl.BlockSpec((tm, S5_WIDTH), lambda b, i: (i, b)),
                   row3(GLA_KEY), row3(GLA_KEY), row3(GLA_VAL), row3(GLA_KEY), row3(GLA_VAL),
                   row3(D_MODEL), row3(D_MODEL)],
        out_shape=[jax.ShapeDtypeStruct((SEQ, BATCH * S5_WIDTH), F32),
                   bld(GLA_KEY), bld(GLA_KEY), bld(GLA_VAL), bld(GLA_KEY), bld(GLA_VAL),
                   bld(D_MODEL), bld(D_MODEL)],
        compiler_params=_params(("parallel", "parallel")),
        name="inproj",
    )(x, g, sc, sh, w_main, w_gk, w_gk2, b_gk2)


def _s5_kernel(u_ref, bre_ref, bim_ref, cre_ref, cim_ref, are_ref, aim_ref, d_ref,
               y_ref, s_ref, st_ref):
    half = S5_COLS // 2

    @pl.when(pl.program_id(0) == 0)
    def _():
        st_ref[...] = jnp.zeros_like(st_ref)

    u = u_ref[...].reshape(S5_TC, BATCH, S5_WIDTH).reshape(S5_TC * BATCH, S5_WIDTH)
    ub = u.astype(BF16)
    nblk = S5_TC // S5_RB
    brows = [slice(r * S5_RB * BATCH, (r + 1) * S5_RB * BATCH) for r in range(nblk)]
    for r in range(nblk):
        for j in range(2):
            uj = ub[brows[r], j * 256:(j + 1) * 256]
            s_ref[brows[r], j * half:(j + 1) * half] = _dot(uj, bre_ref[j])
            s_ref[brows[r], S5_COLS + j * half:S5_COLS + (j + 1) * half] = _dot(uj, bim_ref[j])

    ncc = S5_COLS // S5_COL_CHUNK
    re_sl = [slice(cc * S5_COL_CHUNK, (cc + 1) * S5_COL_CHUNK) for cc in range(ncc)]
    im_sl = [slice(S5_COLS + cc * S5_COL_CHUNK, S5_COLS + (cc + 1) * S5_COL_CHUNK) for cc in range(ncc)]
    sr = [st_ref[:, sl] for sl in re_sl]
    si = [st_ref[:, sl] for sl in im_sl]
    for r in range(nblk):
        for t in range(r * S5_RB, (r + 1) * S5_RB):
            rows = slice(t * BATCH, (t + 1) * BATCH)
            for cc in range(ncc):
                ar = are_ref[:, re_sl[cc]]
                ai = aim_ref[:, re_sl[cc]]
                nr = ar * sr[cc] - ai * si[cc] + s_ref[rows, re_sl[cc]]
                ni = ar * si[cc] + ai * sr[cc] + s_ref[rows, im_sl[cc]]
                s_ref[rows, re_sl[cc]] = nr
                s_ref[rows, im_sl[cc]] = ni
                sr[cc], si[cc] = nr, ni
    for cc in range(ncc):
        st_ref[:, re_sl[cc]] = sr[cc]
        st_ref[:, im_sl[cc]] = si[cc]

    for r in range(nblk):
        ys = []
        for j in range(2):
            sre = s_ref[brows[r], j * half:(j + 1) * half].astype(BF16)
            sim = s_ref[brows[r], S5_COLS + j * half:S5_COLS + (j + 1) * half].astype(BF16)
            ys.append(_dot(sre, cre_ref[j]) + _dot(sim, cim_ref[j]))
        y = jnp.concatenate(ys, axis=1) + d_ref[...] * u[brows[r], :]
        y_ref[r * S5_RB:(r + 1) * S5_RB, :] = (
            jax.nn.gelu(y).reshape(S5_RB, BATCH, S5_WIDTH).reshape(S5_RB, BATCH * S5_WIDTH))


def _s5(u_tb, bre, bim, cre, cimn, are, aim, dflat):
    rows = S5_TC * BATCH
    const = lambda shape: pl.BlockSpec(shape, lambda i: tuple(0 for _ in shape))
    return pl.pallas_call(
        _s5_kernel,
        grid=(SEQ // S5_TC,),
        in_specs=[pl.BlockSpec((S5_TC, BATCH * S5_WIDTH), lambda i: (i, 0)),
                  const(bre.shape), const(bim.shape), const(cre.shape), const(cimn.shape),
                  const(are.shape), const(aim.shape), const(dflat.shape)],
        out_specs=pl.BlockSpec((S5_TC, BATCH * S5_WIDTH), lambda i: (i, 0)),
        out_shape=jax.ShapeDtypeStruct((SEQ, BATCH * S5_WIDTH), F32),
        scratch_shapes=[pltpu.VMEM((rows, 2 * S5_COLS), F32),
                        pltpu.VMEM((BATCH, 2 * S5_COLS), F32)],
        compiler_params=_params(("arbitrary",)),
        name="s5",
    )(u_tb, bre, bim, cre, cimn, are, aim, dflat)


def _s5_tables(lam_re, lam_im, log_dt, b_re, b_im, c_re, c_im, d_skip):
    lr = lam_re.astype(F32)
    li = lam_im.astype(F32)
    dt = jnp.exp(log_dt.astype(F32))[:, None]
    mag = jnp.exp(lr * dt)
    abar_re = mag * jnp.cos(li * dt)
    abar_im = mag * jnp.sin(li * dt)
    den = lr * lr + li * li
    num_re = abar_re - 1.0
    coef_re = (num_re * lr + abar_im * li) / den
    coef_im = (abar_im * lr - num_re * li) / den
    cr, ci = coef_re[..., None], coef_im[..., None]
    br, bi = b_re.astype(F32), b_im.astype(F32)
    bbar_re = cr * br - ci * bi
    bbar_im = cr * bi + ci * br
    eye = jnp.eye(S5_GROUPS // 2, dtype=F32)

    def in_map(bb):
        bb = bb.reshape(2, S5_GROUPS // 2, S5_STATE, S5_GROUP)
        return jnp.einsum('jgph,gk->jghkp', bb, eye).reshape(2, 256, S5_COLS // 2).astype(BF16)

    def out_map(cc):
        cc = cc.reshape(2, S5_GROUPS // 2, S5_GROUP, S5_STATE)
        return jnp.einsum('jghp,gk->jgpkh', cc, eye).reshape(2, S5_COLS // 2, 256).astype(BF16)

    are = jnp.broadcast_to(abar_re.reshape(1, S5_COLS), (BATCH, S5_COLS))
    aim = jnp.broadcast_to(abar_im.reshape(1, S5_COLS), (BATCH, S5_COLS))
    return (in_map(bbar_re), in_map(bbar_im), out_map(c_re.astype(F32)), out_map(-c_im.astype(F32)),
            are, aim, d_skip.astype(F32).reshape(1, S5_WIDTH))


def _gla_kernel(q_ref, k_ref, la_ref, v_ref, rs_ref, ng_ref, tt_ref, o_ref, st_ref):
    @pl.when(pl.program_id(1) == 0)
    def _():
        st_ref[...] = jnp.zeros_like(st_ref)

    c = GLA_CHUNK
    tt = tt_ref[...]
    row = lax.broadcasted_iota(I32, (c, GLA_KEY), 0)
    ri = lax.broadcasted_iota(I32, (c, c), 0)
    ci = lax.broadcasted_iota(I32, (c, c), 1)
    causal = ci <= ri
    nsub = c // GLA_SUB

    heads = range(GLA_HEADS)
    hsl = [slice(h * GLA_DK, (h + 1) * GLA_DK) for h in heads]
    vsl = [slice(h * GLA_DV, (h + 1) * GLA_DV) for h in heads]
    chunks = range(GLA_TG // c)
    intra, kv, qdec, dec = [], [], [], []

    sls = [slice(ch * c, (ch + 1) * c) for ch in chunks]
    csum = []
    for ch in chunks:
        g = la_ref[0, sls[ch], :]
        g1 = g.astype(BF16)
        r1 = g - g1.astype(F32)
        g2 = r1.astype(BF16)
        g3 = (r1 - g2.astype(F32)).astype(BF16)
        csum.append(_dot(tt, g1) + _dot(tt, g2) + _dot(tt, g3))
    qbigs, kbigs, kdec = [], [], []
    for ch in chunks:
        sl = sls[ch]
        cs = csum[ch]
        b = cs[0:c]
        cl = cs[c:2 * c]
        ref_pt = b - cl
        q = q_ref[0, sl, :]
        k = k_ref[0, sl, :]
        qt = q * jnp.exp(cl)
        qe = (q * jnp.exp(b)).astype(BF16)
        blast = b[c - 1:c, :]
        ks = (k * jnp.exp(blast - b)).astype(BF16)
        k_sub = []
        q_sub = []
        for s in range(nsub):
            rs_ = ref_pt[s * GLA_SUB:s * GLA_SUB + 1, :]
            k_sub.append(k * jnp.exp(jnp.minimum(rs_ - b, EXP_CLAMP)))
            q_sub.append(jnp.where((row >= s * GLA_SUB) & (row < (s + 1) * GLA_SUB), qt, 0.0))
        qbigs.append([jnp.concatenate([x[:, hsl[h]] for x in q_sub], axis=1).astype(BF16)
                      for h in heads])
        kbigs.append([jnp.concatenate([x[:, hsl[h]] for x in k_sub], axis=1).astype(BF16)
                      for h in heads])
        kdec.append(ks)
        qdec.append(qe)
        dec.append(jnp.exp(blast))
    scores = [[_dot_nt(qbigs[ch][h], kbigs[ch][h]) for h in heads] for ch in chunks]
    for ch in chunks:
        kv.append([_dot(v_ref[0, sls[ch], vsl[h]].T.astype(BF16), kdec[ch][:, hsl[h]])
                   for h in heads])
    for ch in chunks:
        intra.append([_dot(jnp.where(causal, scores[ch][h], 0.0).astype(BF16),
                           v_ref[0, sls[ch], vsl[h]].astype(BF16)) for h in heads])

    st = [st_ref[h] for h in heads]
    for ch in chunks:
        sl = slice(ch * c, (ch + 1) * c)
        inter = [_dot_nt(qdec[ch][:, hsl[h]], st[h].astype(BF16)) for h in heads]
        for h in heads:
            o = inter[h] + intra[ch][h]
            st[h] = st[h] * dec[ch][:, hsl[h]] + kv[ch][h]
            on = o * lax.rsqrt(jnp.mean(o * o, axis=-1, keepdims=True) + EPS) * ng_ref[...]
            o_ref[0, sl, vsl[h]] = on * rs_ref[0, sl, vsl[h]]
    for h in heads:
        st_ref[h] = st[h]


def _gla(q, k, la, v, rs, norm_g):
    tg = GLA_TG
    c = GLA_CHUNK
    r = jnp.arange(c)
    tri = (r[None, :] <= r[:, None])
    blk = tri & ((r[None, :] // GLA_SUB) == (r[:, None] // GLA_SUB))
    tt = jnp.concatenate([tri, blk], axis=0).astype(BF16)
    row3 = lambda w: pl.BlockSpec((1, tg, w), lambda b, i: (b, i, 0))
    const = lambda shape: pl.BlockSpec(shape, lambda b, i: tuple(0 for _ in shape))
    return pl.pallas_call(
        _gla_kernel,
        grid=(BATCH, SEQ // tg),
        in_specs=[row3(GLA_KEY), row3(GLA_KEY), row3(GLA_KEY), row3(GLA_VAL), row3(GLA_VAL),
                  const((1, GLA_DV)), const((2 * c, c))],
        out_specs=row3(GLA_VAL),
        out_shape=jax.ShapeDtypeStruct((BATCH, SEQ, GLA_VAL), F32),
        scratch_shapes=[pltpu.VMEM((GLA_HEADS, GLA_DV, GLA_DK), F32)],
        compiler_params=_params(("parallel", "arbitrary")),
        name="gla",
    )(q, k, la, v, rs, norm_g.reshape(1, GLA_DV), tt)


def _merge_kernel(ya_ref, yb_ref, ga_ref, gb_ref, x_ref, g1_ref, sc2_ref, sh2_ref, g2_ref, n2_ref,
                  wglu_ref, bglu_ref, wpa_ref, wpb_ref, wout_ref, wr_ref, wsgu_ref, wsd_ref,
                  base_ref, hp_ref, sc_ref):
    rows = MERGE_TM // MERGE_SPLIT
    subs = [slice(s * rows, (s + 1) * rows) for s in range(MERGE_SPLIT)]
    ya = [ya_ref[sl, :] for sl in subs]
    z = [_dot(a.astype(BF16), wglu_ref[...]) for a in ya]
    pb = [_dot(yb_ref[0, sl, :].astype(BF16), wpb_ref[...]) for sl in subs]
    ya2 = [(a * _sigmoid(zz + bglu_ref[...])).astype(BF16) for a, zz in zip(ya, z)]
    pa = [_dot(a, wpa_ref[...]) for a in ya2]
    mixed = [(ga_ref[0, sl, :] * p + gb_ref[0, sl, :] * q).astype(BF16)
             for sl, p, q in zip(subs, pa, pb)]
    mo = [_dot(m, wout_ref[...]) for m in mixed]
    x1 = [x_ref[0, sl, :] + g1_ref[0] * m for sl, m in zip(subs, mo)]
    h = [(x * lax.rsqrt(jnp.mean(x * x, axis=-1, keepdims=True) + EPS)) * n2_ref[...]
         * (1.0 + sc2_ref[0]) + sh2_ref[0] for x in x1]
    hb = [v.astype(BF16) for v in h]
    logits = [_dot(v, wr_ref[...]) for v in hb]
    gu = [_dot(v, wsgu_ref[...]) for v in hb]
    mid = [(g[:, :EXPERT_FF] * _sigmoid(g[:, :EXPERT_FF]) * g[:, EXPERT_FF:]).astype(BF16) for g in gu]
    shared = [_dot(m, wsd_ref[...]) for m in mid]
    for s, sl in enumerate(subs):
        sc_ref[sl, :] = _sigmoid(logits[s])
        base_ref[sl] = (x1[s] + g2_ref[0] * shared[s]).reshape(rows, *TOKEN_TILE)
        hp_ref[sl] = h[s].reshape(rows, *TOKEN_TILE)


def _merge(ya_tb, yb, ga, gb, x, g1, sc2, sh2, g2, n2, wglu, bglu, wpa, wpb, wout, wr, wsgu, wsd):
    tm = MERGE_TM
    row3 = lambda w: pl.BlockSpec((1, tm, w), lambda b, i: (b, i, 0))
    const = lambda a: pl.BlockSpec(a.shape, lambda b, i: tuple(0 for _ in a.shape))
    mod = pl.BlockSpec((1, 1, D_MODEL), lambda b, i: (b, 0, 0))
    nt = SEQ // tm
    flat = lambda w: pl.BlockSpec((tm, w), lambda b, i: (b * nt + i, 0))
    tiles = pl.BlockSpec((tm,) + TOKEN_TILE, lambda b, i: (b * nt + i, 0, 0))
    return pl.pallas_call(
        _merge_kernel,
        grid=(BATCH, nt),
        in_specs=[pl.BlockSpec((tm, S5_WIDTH), lambda b, i: (i, b)),
                  row3(GLA_VAL), row3(D_MODEL), row3(D_MODEL), row3(D_MODEL),
                  mod, mod, mod, mod, const(n2),
                  const(wglu), const(bglu), const(wpa), const(wpb), const(wout), const(wr),
                  const(wsgu), const(wsd)],
        out_specs=[tiles, tiles, flat(N_EXPERTS)],
        out_shape=[jax.ShapeDtypeStruct((TOKENS,) + TOKEN_TILE, F32),
                   jax.ShapeDtypeStruct((TOKENS,) + TOKEN_TILE, F32),
                   jax.ShapeDtypeStruct((TOKENS, N_EXPERTS), F32)],
        compiler_params=_params(("parallel", "parallel")),
        name="merge",
    )(ya_tb, yb, ga, gb, x, g1, sc2, sh2, g2, n2, wglu, bglu, wpa, wpb, wout, wr, wsgu, wsd)


def _first_argmax(x, iota, size):
    m = jnp.max(x, axis=0, keepdims=True)
    idx = jnp.min(jnp.where(x == m, iota, size), axis=0, keepdims=True)
    return m, idx


def _route_kernel(s_ref, bias_ref, tri_ref, e_ref, r_ref, w_ref, cnt_ref, carry_ref):
    @pl.when(pl.program_id(0) == 0)
    def _():
        carry_ref[...] = jnp.zeros_like(carry_ref)

    tr = ROUTE_TR
    neg = -jnp.inf
    s_t = s_ref[...].T
    biased = s_t + bias_ref[...]
    io_g = lax.broadcasted_iota(I32, (GROUP_SIZE, tr), 0)
    rows = []
    for g in range(N_GROUPS):
        xg = biased[g * GROUP_SIZE:(g + 1) * GROUP_SIZE, :]
        m1, i1 = _first_argmax(xg, io_g, GROUP_SIZE)
        m2 = jnp.max(jnp.where(io_g == i1, neg, xg), axis=0, keepdims=True)
        rows.append(m1 + m2)
    gs = jnp.concatenate(rows, axis=0)
    io_n = lax.broadcasted_iota(I32, (N_GROUPS, tr), 0)
    gsel = jnp.zeros((N_GROUPS, tr), F32)
    for _ in range(TOPK_GROUPS):
        _, gi = _first_argmax(gs, io_n, N_GROUPS)
        hit = io_n == gi
        gsel = jnp.where(hit, 1.0, gsel)
        gs = jnp.where(hit, neg, gs)
    masked = jnp.concatenate(
        [jnp.where(gsel[g:g + 1, :] > 0.0, biased[g * GROUP_SIZE:(g + 1) * GROUP_SIZE, :], neg)
         for g in range(N_GROUPS)], axis=0)
    io_e = lax.broadcasted_iota(I32, (N_EXPERTS, tr), 0)
    sel = jnp.zeros((N_EXPERTS, tr), F32)
    idxs = []
    for _ in range(TOP_K):
        _, ei = _first_argmax(masked, io_e, N_EXPERTS)
        hit = io_e == ei
        sel = jnp.where(hit, 1.0, sel)
        masked = jnp.where(hit, neg, masked)
        idxs.append(ei)
    den = jnp.sum(sel * s_t, axis=0, keepdims=True)
    rank = _dot(sel.astype(BF16), tri_ref[...]) + carry_ref[...]
    rks, sks = [], []
    for kk in range(TOP_K):
        hit = io_e == idxs[kk]
        rks.append(jnp.sum(jnp.where(hit, rank, 0.0), axis=0, keepdims=True))
        sks.append(jnp.sum(jnp.where(hit, s_t, 0.0), axis=0, keepdims=True))
    e_ref[...] = jnp.concatenate(idxs, axis=0)
    r_ref[...] = jnp.concatenate(rks, axis=0).astype(I32)
    w_ref[...] = jnp.concatenate(sks, axis=0) / den * ROUTE_SCALE
    carry_ref[...] += jnp.sum(sel, axis=1, keepdims=True)
    cnt_ref[...] = carry_ref[...]


def _route(scores, bias):
    tr = ROUTE_TR
    r = jnp.arange(tr)
    tri = (r[:, None] < r[None, :]).astype(BF16)
    kt = lambda dt: jax.ShapeDtypeStruct((TOP_K, TOKENS), dt)
    blk = pl.BlockSpec((TOP_K, tr), lambda i: (0, i))
    return pl.pallas_call(
        _route_kernel,
        grid=(TOKENS // tr,),
        in_specs=[pl.BlockSpec((tr, N_EXPERTS), lambda i: (i, 0)),
                  pl.BlockSpec((N_EXPERTS, 1), lambda i: (0, 0)),
                  pl.BlockSpec((tr, tr), lambda i: (0, 0))],
        out_specs=[blk, blk, blk, pl.BlockSpec((N_EXPERTS, 1), lambda i: (0, 0))],
        out_shape=[kt(I32), kt(I32), kt(F32), jax.ShapeDtypeStruct((N_EXPERTS, 1), F32)],
        scratch_shapes=[pltpu.VMEM((N_EXPERTS, 1), F32)],
        compiler_params=_params(("arbitrary",)),
        name="route",
    )(scores, bias.reshape(N_EXPERTS, 1), tri)


def _slots_kernel(e_ref, r_ref, ps_ref, d_ref):
    tr = ROUTE_TR
    io_e = lax.broadcasted_iota(I32, (N_EXPERTS, tr), 0)
    ps = ps_ref[...]
    rows = []
    for kk in range(TOP_K):
        hit = io_e == e_ref[kk:kk + 1, :]
        rows.append(jnp.sum(jnp.where(hit, ps, 0.0), axis=0, keepdims=True))
    d_ref[...] = jnp.concatenate(rows, axis=0).astype(I32) + r_ref[...]


def _slots(e_idx, rank, pad_start):
    tr = ROUTE_TR
    blk = pl.BlockSpec((TOP_K, tr), lambda i: (0, i))
    return pl.pallas_call(
        _slots_kernel,
        grid=(TOKENS // tr,),
        in_specs=[blk, blk, pl.BlockSpec((N_EXPERTS, 1), lambda i: (0, 0))],
        out_specs=blk,
        out_shape=jax.ShapeDtypeStruct((TOP_K, TOKENS), I32),
        compiler_params=_params(("parallel",)),
        name="slots",
    )(e_idx, rank, pad_start.astype(F32).reshape(N_EXPERTS, 1))


def _dispatch_kernel(ps_ref, cnt_ref, nb_ref, dest_ref, h_ref, xs_ref, zbuf, sem, zsem):
    tq = DISPATCH_TQ
    step = pl.program_id(0)
    n_used = ps_ref[N_EXPERTS - 1] // MOE_BLOCK + nb_ref[N_EXPERTS - 1]

    def zero_fill(act):
        def block(j, carry):
            r = pl.ds(pl.multiple_of(j * MOE_BLOCK, MOE_BLOCK), MOE_BLOCK)
            act(pltpu.make_async_copy(zbuf, xs_ref.at[r], zsem))
            return carry

        def expert(e, carry):
            @pl.when(cnt_ref[e] < nb_ref[e] * MOE_BLOCK)
            def _():
                block(ps_ref[e] // MOE_BLOCK + nb_ref[e] - 1, 0)
            return carry

        lax.fori_loop(0, N_EXPERTS, expert, 0)
        lax.fori_loop(n_used, N_BLOCKS, block, 0)

    @pl.when(step == 0)
    def _():
        zbuf[...] = jnp.zeros_like(zbuf)
        zero_fill(lambda cp: cp.start())
        zero_fill(lambda cp: cp.wait())

    def start(i, carry):
        for kk in range(TOP_K):
            d = dest_ref[i * TOP_K + kk]
            pltpu.make_async_copy(h_ref.at[i], xs_ref.at[d],
                                  sem).start(priority=kk % 2)
        return carry

    lax.fori_loop(0, tq, start, 0, unroll=2)
    for _ in range(TOP_K):
        pltpu.make_async_copy(h_ref, xs_ref.at[pl.ds(0, tq)], sem).wait()


def _dispatch(pad_start, counts, n_blk, dest, hp):
    tq = DISPATCH_TQ
    grid_spec = pltpu.PrefetchScalarGridSpec(
        num_scalar_prefetch=3,
        grid=(TOKENS // tq,),
        in_specs=[pl.BlockSpec((tq * TOP_K,), lambda i, *_: (i,), memory_space=pltpu.SMEM),
                  pl.BlockSpec((tq,) + TOKEN_TILE, lambda i, *_: (i, 0, 0))],
        out_specs=pl.BlockSpec(memory_space=pl.ANY),
        scratch_shapes=[pltpu.VMEM((MOE_BLOCK,) + TOKEN_TILE, F32),
                        pltpu.SemaphoreType.DMA, pltpu.SemaphoreType.DMA],
    )
    return pl.pallas_call(
        _dispatch_kernel,
        grid_spec=grid_spec,
        out_shape=jax.ShapeDtypeStruct((N_SLOTS,) + TOKEN_TILE, F32),
        compiler_params=_params(("arbitrary",)),
        name="dispatch",
    )(pad_start, counts, n_blk, dest, hp)


def _expert_kernel(ps_ref, nb_ref, wg_ref, wu_ref, wd_ref, xs_ref, ys_ref,
                   xbuf, ybuf, wgus, wds, xsem, ysem):
    e = pl.program_id(0)
    nb = nb_ref[e]
    first = ps_ref[e] // MOE_BLOCK
    n_used = ps_ref[N_EXPERTS - 1] // MOE_BLOCK + nb_ref[N_EXPERTS - 1]
    nbuf = EXPERT_NBUF

    def rows(g):
        return pl.ds(pl.multiple_of(g * MOE_BLOCK, MOE_BLOCK), MOE_BLOCK)

    def x_copy(g):
        slot = g % nbuf
        return pltpu.make_async_copy(xs_ref.at[rows(g)], xbuf.at[slot], xsem.at[slot])

    def y_copy(g):
        slot = g % nbuf
        return pltpu.make_async_copy(ybuf.at[slot], ys_ref.at[rows(g)], ysem.at[slot])

    look = nbuf - EXPERT_UNIT

    @pl.when(e == 0)
    def _():
        for g in range(look):
            @pl.when(g < n_used)
            def _():
                x_copy(g).start()

    def run(g, u):
        for d in range(u):
            @pl.when(g + look + d < n_used)
            def _():
                x_copy(g + look + d).start()

        for d in range(u):
            b = g + d

            @pl.when(b >= nbuf)
            def _():
                y_copy(b - nbuf).wait()

        for d in range(u):
            x_copy(g + d).wait()
        gus = [_dot(xbuf[(g + d) % nbuf].reshape(MOE_BLOCK, D_MODEL).astype(BF16), wgus[...])
               for d in range(u)]
        for d in range(u):
            gate = gus[d][:, :EXPERT_FF]
            mid = (gate * _sigmoid(gate) * gus[d][:, EXPERT_FF:]).astype(BF16)
            ybuf[(g + d) % nbuf] = _dot(mid, wds[...]).reshape(MOE_BLOCK, *TOKEN_TILE)
        for d in range(u):
            y_copy(g + d).start()

    @pl.when(nb > 0)
    def _():
        wgus[:, :EXPERT_FF] = wg_ref[0].astype(BF16)
        wgus[:, EXPERT_FF:] = wu_ref[0].astype(BF16)
        wds[...] = wd_ref[0].astype(BF16)

        whole = nb == EXPERT_UNIT
        triple = (nb % 2 == 1) & (nb >= 3)
        pairs = jnp.where(whole, 0, jnp.where(triple, (nb - 3) // 2, nb // 2))

        def pair(j, carry):
            run(first + j * 2, 2)
            return carry

        lax.fori_loop(0, pairs, pair, 0)

        @pl.when(whole)
        def _():
            run(first, EXPERT_UNIT)

        @pl.when(triple)
        def _():
            run(first + nb - 3, 3)

        @pl.when(nb == 1)
        def _():
            run(first, 1)

    @pl.when(e == N_EXPERTS - 1)
    def _():
        for d in range(nbuf):
            g = n_used - nbuf + d

            @pl.when(g >= 0)
            def _():
                y_copy(g).wait()

        ybuf[0] = jnp.zeros((MOE_BLOCK,) + TOKEN_TILE, F32)

        def z_copy(j):
            return pltpu.make_async_copy(ybuf.at[0], ys_ref.at[rows(j)], ysem.at[0])

        def z_start(j, carry):
            z_copy(j).start()
            return carry

        def z_wait(j, carry):
            z_copy(j).wait()
            return carry

        lax.fori_loop(n_used, N_BLOCKS, z_start, 0)
        lax.fori_loop(n_used, N_BLOCKS, z_wait, 0)


def _experts(pad_start, n_blk, xs, wg, wu, wd):
    wspec = lambda shape: pl.BlockSpec((1,) + shape, lambda e, ps, nb: (e, 0, 0))
    grid_spec = pltpu.PrefetchScalarGridSpec(
        num_scalar_prefetch=2,
        grid=(N_EXPERTS,),
        in_specs=[wspec((D_MODEL, EXPERT_FF)), wspec((D_MODEL, EXPERT_FF)),
                  wspec((EXPERT_FF, D_MODEL)), pl.BlockSpec(memory_space=pl.ANY)],
        out_specs=pl.BlockSpec(memory_space=pl.ANY),
        scratch_shapes=[pltpu.VMEM((EXPERT_NBUF, MOE_BLOCK) + TOKEN_TILE, F32),
                        pltpu.VMEM((EXPERT_NBUF, MOE_BLOCK) + TOKEN_TILE, F32),
                        pltpu.VMEM((D_MODEL, 2 * EXPERT_FF), BF16),
                        pltpu.VMEM((EXPERT_FF, D_MODEL), BF16),
                        pltpu.SemaphoreType.DMA((EXPERT_NBUF,)),
                        pltpu.SemaphoreType.DMA((EXPERT_NBUF,))],
    )
    return pl.pallas_call(
        _expert_kernel,
        grid_spec=grid_spec,
        out_shape=jax.ShapeDtypeStruct((N_SLOTS,) + TOKEN_TILE, F32),
        compiler_params=_params(("arbitrary",)),
        name="experts",
    )(pad_start, n_blk, wg, wu, wd, xs)


def _combine_kernel(dcur_ref, dnext_ref, w_ref, base_ref, g2_ref, fg_ref, ys_ref, o_ref,
                    buf, acc_ref, sem):
    tq = COMBINE_TQ
    i = pl.program_id(0)
    n = pl.num_programs(0)

    def start(dest_ref, to_slot, t):
        for kk in range(TOP_K):
            d = dest_ref[t * TOP_K + kk]
            pltpu.make_async_copy(ys_ref.at[d], buf.at[to_slot, kk, t],
                                  sem.at[to_slot]).start(priority=kk % 2)

    def reduce(slot, t):
        a = buf[slot, 0, t] * w_ref[t * TOP_K]
        for kk in range(1, TOP_K):
            a = a + buf[slot, kk, t] * w_ref[t * TOP_K + kk]
        acc_ref[t] = a

    @pl.when(i == 0)
    def _():
        lax.fori_loop(0, tq, lambda t, c: start(dcur_ref, 0, t) or c, 0, unroll=4)

    for slot in range(2):
        @pl.when(i % 2 == slot)
        def _(slot=slot):
            @pl.when(i + 1 < n)
            def _():
                lax.fori_loop(0, tq, lambda t, c: start(dnext_ref, 1 - slot, t) or c, 0, unroll=4)

            for kk in range(TOP_K):
                pltpu.make_async_copy(ys_ref.at[pl.ds(0, tq)], buf.at[slot, kk],
                                      sem.at[slot]).wait()
            lax.fori_loop(0, tq, lambda t, c: reduce(slot, t) or c, 0, unroll=8)

    xo = base_ref[...] + g2_ref[0] * acc_ref[...]
    ms = jnp.sum(jnp.sum(xo * xo, axis=2, keepdims=True), axis=1, keepdims=True) * (1.0 / D_MODEL)
    o_ref[0] = ((xo * lax.rsqrt(ms + EPS)) * fg_ref[...]).reshape(tq, D_MODEL)


def _combine(dest, w_flat, base, g2, final_g, ys):
    tq = COMBINE_TQ
    nt = SEQ // tq
    n = TOKENS // tq
    smem = lambda imap: pl.BlockSpec((tq * TOP_K,), imap, memory_space=pltpu.SMEM)
    return pl.pallas_call(
        _combine_kernel,
        grid=(n,),
        in_specs=[smem(lambda i: (i,)), smem(lambda i: (jnp.minimum(i + 1, n - 1),)),
                  smem(lambda i: (i,)),
                  pl.BlockSpec((tq,) + TOKEN_TILE, lambda i: (i, 0, 0)),
                  pl.BlockSpec((1,) + TOKEN_TILE, lambda i: (i // nt, 0, 0)),
                  pl.BlockSpec(TOKEN_TILE, lambda i: (0, 0)),
                  pl.BlockSpec(memory_space=pl.ANY)],
        out_specs=pl.BlockSpec((1, tq, D_MODEL), lambda i: (i // nt, i % nt, 0)),
        out_shape=jax.ShapeDtypeStruct((BATCH, SEQ, D_MODEL), F32),
        scratch_shapes=[pltpu.VMEM((2, TOP_K, tq) + TOKEN_TILE, F32),
                        pltpu.VMEM((tq,) + TOKEN_TILE, F32),
                        pltpu.SemaphoreType.DMA((2,))],
        compiler_params=_params(("arbitrary",)),
        name="combine",
    )(dest, dest, w_flat, base, g2.reshape((BATCH,) + TOKEN_TILE), final_g.reshape(TOKEN_TILE), ys)


def kernel(x, c, ada_w, ada_b, norm1_g, w_in, s5_lam_re, s5_lam_im, s5_log_dt, s5_b_re, s5_b_im,
           s5_c_re, s5_c_im, s5_d, s5_w_glu, s5_b_glu, w_proj_a, gla_w_gk2, gla_b_gk2, gla_norm_g,
           w_proj_b, w_out, norm2_g, router_w, router_bias, exp_w_gate, exp_w_up, exp_w_down,
           sh_w_gate, sh_w_up, sh_w_down, final_g):
    l = 0
    mod = _ada(c, ada_w[l], ada_b[l])
    sh1, sc1, g1, sh2, sc2, g2 = [m.reshape(BATCH, 1, D_MODEL) for m in jnp.split(mod, 6, axis=-1)]

    w = w_in[l]
    gk0 = _V0 + GLA_VAL
    w_main = jnp.concatenate([w[:, :gk0], w[:, gk0 + GLA_GATE_RANK:]], axis=1).astype(BF16)
    w_gk = jnp.pad(w[:, gk0:gk0 + GLA_GATE_RANK], ((0, 0), (0, LANES - GLA_GATE_RANK))).astype(BF16)
    w_gk2 = jnp.pad(gla_w_gk2[l], ((0, LANES - GLA_GATE_RANK), (0, 0))).astype(BF16)
    u_tb, q, k, v, la, rs, ga, gb = _inproj(
        x, norm1_g[l].reshape(1, D_MODEL), sc1, sh1, w_main, w_gk, w_gk2,
        gla_b_gk2[l].reshape(1, GLA_KEY))
    tables = _s5_tables(s5_lam_re[l], s5_lam_im[l], s5_log_dt[l], s5_b_re[l], s5_b_im[l],
                        s5_c_re[l], s5_c_im[l], s5_d[l])
    ya_tb = _s5(u_tb, *tables)
    yb = _gla(q, k, la, v, rs, gla_norm_g[l])

    wsgu = jnp.concatenate([sh_w_gate[l], sh_w_up[l]], axis=1).astype(BF16)
    base, hp, scores = _merge(
        ya_tb, yb, ga, gb, x, g1, sc2, sh2, g2,
        norm2_g[l].reshape(1, D_MODEL), s5_w_glu[l].astype(BF16), s5_b_glu[l].reshape(1, S5_WIDTH),
        w_proj_a[l].astype(BF16), w_proj_b[l].astype(BF16), w_out[l].astype(BF16),
        router_w[l].astype(BF16), wsgu, sh_w_down[l].astype(BF16))

    e_idx, rank, w_k, cnt = _route(scores, router_bias[l])
    counts = cnt[:, 0].astype(I32)
    padded = (counts + MOE_BLOCK - 1) // MOE_BLOCK * MOE_BLOCK
    pad_end = jnp.cumsum(padded)
    pad_start = pad_end - padded
    dest = _slots(e_idx, rank, pad_start).T.reshape(-1)
    pad_start = pad_start.astype(I32)
    n_blk = (padded // MOE_BLOCK).astype(I32)
    xs = _dispatch(pad_start, counts, n_blk, dest, hp)
    ys = _experts(pad_start, n_blk, xs, exp_w_gate[l], exp_w_up[l], exp_w_down[l])
    return _combine(dest, w_k.T.reshape(-1), base, g2, final_g, ys)
```

```python
import jax
import jax.numpy as jnp
from jax import lax
from jax.experimental import pallas as pl
from jax.experimental.pallas import tpu as pltpu

F32 = jnp.float32
BF16 = jnp.bfloat16
I32 = jnp.int32

D_MODEL = 1024
BATCH = 8
SEQ = 2048
TOKENS = BATCH * SEQ
S5_WIDTH = 512
S5_GROUP = 16
S5_GROUPS = 32
S5_STATE = 64
S5_COLS = S5_GROUPS * S5_STATE
GLA_HEADS = 4
GLA_DK = 64
GLA_DV = 128
GLA_KEY = GLA_HEADS * GLA_DK
GLA_VAL = GLA_HEADS * GLA_DV
GLA_GATE_RANK = 16
GLA_GATE_TAU = 16.0
GLA_CHUNK = 64
GLA_SUB = 16
N_EXPERTS = 256
TOP_K = 8
N_GROUPS = 8
GROUP_SIZE = N_EXPERTS // N_GROUPS
TOPK_GROUPS = 4
EXPERT_FF = 256
ROUTE_SCALE = 2.5
MOE_BLOCK = 128
EPS = 1e-6
N_SLOTS = -(-(TOKENS * TOP_K + N_EXPERTS * (MOE_BLOCK - 1)) // MOE_BLOCK) * MOE_BLOCK
N_BLOCKS = N_SLOTS // MOE_BLOCK

LANES = 128
SUBLANES = 8
TOKEN_TILE = (SUBLANES, LANES)
assert D_MODEL == SUBLANES * LANES
VMEM_LIMIT = 56 * 1024 * 1024

ADA_TN = 1536
INPROJ_TM = 512
S5_TC = 128
S5_RB = 32
S5_COL_CHUNK = 512
GLA_TG = 512
MERGE_TM = 512
MERGE_SPLIT = 2
ROUTE_TR = 512
DISPATCH_TQ = 1024
COMBINE_TQ = 256
EXPERT_UNIT = 4
EXPERT_NBUF = 10
EXP_CLAMP = 60.0


def _params(sem, vmem=VMEM_LIMIT):
    return pltpu.CompilerParams(dimension_semantics=sem, vmem_limit_bytes=vmem)


def _dot(a, b):
    return jnp.dot(a, b, preferred_element_type=F32)


def _dot_nt(a, b):
    return lax.dot_general(a, b, (((1,), (1,)), ((), ())), preferred_element_type=F32)


def _sigmoid(x):
    return jax.nn.sigmoid(x)


def _ada_kernel(c_ref, w_ref, b_ref, o_ref):
    c = c_ref[...]
    s = (c * _sigmoid(c)).astype(BF16)
    o_ref[...] = _dot(s, w_ref[...].astype(BF16)) + b_ref[...]


def _ada(c, w, b):
    n = w.shape[1]
    return pl.pallas_call(
        _ada_kernel,
        grid=(n // ADA_TN,),
        in_specs=[
            pl.BlockSpec((BATCH, D_MODEL), lambda j: (0, 0)),
            pl.BlockSpec((D_MODEL, ADA_TN), lambda j: (0, j)),
            pl.BlockSpec((1, ADA_TN), lambda j: (0, j)),
        ],
        out_specs=pl.BlockSpec((BATCH, ADA_TN), lambda j: (0, j)),
        out_shape=jax.ShapeDtypeStruct((BATCH, n), F32),
        compiler_params=_params(("arbitrary",)),
        name="ada",
    )(c, w, b.reshape(1, n))


_U0, _Q0, _K0, _V0, _R0, _GA0, _GB0, _END = 0, 512, 768, 1024, 1536, 2048, 3072, 4096


def _inproj_kernel(x_ref, g_ref, sc_ref, sh_ref, wm_ref, wgk_ref, wgk2_ref, bgk_ref,
                   u_ref, q_ref, k_ref, v_ref, la_ref, rs_ref, ga_ref, gb_ref):
    x = x_ref[0]
    r = lax.rsqrt(jnp.mean(x * x, axis=-1, keepdims=True) + EPS)
    h = (x * r) * g_ref[...] * (1.0 + sc_ref[0]) + sh_ref[0]
    hb = h.astype(BF16)

    def seg(lo, hi):
        return _dot(hb, wm_ref[:, lo:hi])

    u_ref[...] = seg(_U0, _Q0)
    q_ref[0] = seg(_Q0, _K0) * (GLA_DK ** -0.5)
    k_ref[0] = seg(_K0, _V0)
    v_ref[0] = seg(_V0, _R0)
    rr = seg(_R0, _GA0)
    rs_ref[0] = rr * _sigmoid(rr)
    ga_ref[0] = _sigmoid(seg(_GA0, _GB0))
    gb_ref[0] = _sigmoid(seg(_GB0, _END))
    gk = _dot(hb, wgk_ref[...])
    z = _dot(gk.astype(BF16), wgk2_ref[...]) + bgk_ref[...]
    la_ref[0] = -(jnp.maximum(-z, 0.0) + jnp.log1p(jnp.exp(-jnp.abs(z)))) * (1.0 / GLA_GATE_TAU)


def _inproj(x, g, sc, sh, w_main, w_gk, w_gk2, b_gk2):
    tm = INPROJ_TM
    nt = SEQ // tm
    row3 = lambda w: pl.BlockSpec((1, tm, w), lambda b, i: (b, i, 0))
    const = lambda shape: pl.BlockSpec(shape, lambda b, i: tuple(0 for _ in shape))
    mod = pl.BlockSpec((1, 1, D_MODEL), lambda b, i: (b, 0, 0))
    bld = lambda w: jax.ShapeDtypeStruct((BATCH, SEQ, w), F32)
    return pl.pallas_call(
        _inproj_kernel,
        grid=(BATCH, nt),
        in_specs=[row3(D_MODEL), const((1, D_MODEL)), mod, mod,
                  const((D_MODEL, _END)), const((D_MODEL, LANES)), const((LANES, GLA_KEY)),
                  const((1, GLA_KEY))],
        out_specs=[pl.BlockSpec((tm, S5_WIDTH), lambda b, i: (i, b)),
                   row3(GLA_KEY), row3(GLA_KEY), row3(GLA_VAL), row3(GLA_KEY), row3(GLA_VAL),
                   row3(D_MODEL), row3(D_MODEL)],
        out_shape=[jax.ShapeDtypeStruct((SEQ, BATCH * S5_WIDTH), F32),
                   bld(GLA_KEY), bld(GLA_KEY), bld(GLA_VAL), bld(GLA_KEY), bld(GLA_VAL),
                   bld(D_MODEL), bld(D_MODEL)],
        compiler_params=_params(("parallel", "parallel")),
        name="inproj",
    )(x, g, sc, sh, w_main, w_gk, w_gk2, b_gk2)


def _s5_kernel(u_ref, bre_ref, bim_ref, cre_ref, cim_ref, are_ref, aim_ref, d_ref,
               y_ref, s_ref, st_ref):
    half = S5_COLS // 2

    @pl.when(pl.program_id(0) == 0)
    def _():
        st_ref[...] = jnp.zeros_like(st_ref)

    u = u_ref[...].reshape(S5_TC, BATCH, S5_WIDTH).reshape(S5_TC * BATCH, S5_WIDTH)
    ub = u.astype(BF16)
    nblk = S5_TC // S5_RB
    brows = [slice(r * S5_RB * BATCH, (r + 1) * S5_RB * BATCH) for r in range(nblk)]
    for r in range(nblk):
        for j in range(2):
            uj = ub[brows[r], j * 256:(j + 1) * 256]
            s_ref[brows[r], j * half:(j + 1) * half] = _dot(uj, bre_ref[j])
            s_ref[brows[r], S5_COLS + j * half:S5_COLS + (j + 1) * half] = _dot(uj, bim_ref[j])

    ncc = S5_COLS // S5_COL_CHUNK
    re_sl = [slice(cc * S5_COL_CHUNK, (cc + 1) * S5_COL_CHUNK) for cc in range(ncc)]
    im_sl = [slice(S5_COLS + cc * S5_COL_CHUNK, S5_COLS + (cc + 1) * S5_COL_CHUNK) for cc in range(ncc)]
    sr = [st_ref[:, sl] for sl in re_sl]
    si = [st_ref[:, sl] for sl in im_sl]
    for r in range(nblk):
        for t in range(r * S5_RB, (r + 1) * S5_RB):
            rows = slice(t * BATCH, (t + 1) * BATCH)
            for cc in range(ncc):
                ar = are_ref[:, re_sl[cc]]
                ai = aim_ref[:, re_sl[cc]]
                nr = ar * sr[cc] - ai * si[cc] + s_ref[rows, re_sl[cc]]
                ni = ar * si[cc] + ai * sr[cc] + s_ref[rows, im_sl[cc]]
                s_ref[rows, re_sl[cc]] = nr
                s_ref[rows, im_sl[cc]] = ni
                sr[cc], si[cc] = nr, ni
    for cc in range(ncc):
        st_ref[:, re_sl[cc]] = sr[cc]
        st_ref[:, im_sl[cc]] = si[cc]

    for r in range(nblk):
        ys = []
        for j in range(2):
            sre = s_ref[brows[r], j * half:(j + 1) * half].astype(BF16)
            sim = s_ref[brows[r], S5_COLS + j * half:S5_COLS + (j + 1) * half].astype(BF16)
            ys.append(_dot(sre, cre_ref[j]) + _dot(sim, cim_ref[j]))
        y = jnp.concatenate(ys, axis=1) + d_ref[...] * u[brows[r], :]
        y_ref[r * S5_RB:(r + 1) * S5_RB, :] = (
            jax.nn.gelu(y).reshape(S5_RB, BATCH, S5_WIDTH).reshape(S5_RB, BATCH * S5_WIDTH))


def _s5(u_tb, bre, bim, cre, cimn, are, aim, dflat):
    rows = S5_TC * BATCH
    const = lambda shape: pl.BlockSpec(shape, lambda i: tuple(0 for _ in shape))
    return pl.pallas_call(
        _s5_kernel,
        grid=(SEQ // S5_TC,),
        in_specs=[pl.BlockSpec((S5_TC, BATCH * S5_WIDTH), lambda i: (i, 0)),
                  const(bre.shape), const(bim.shape), const(cre.shape), const(cimn.shape),
                  const(are.shape), const(aim.shape), const(dflat.shape)],
        out_specs=pl.BlockSpec((S5_TC, BATCH * S5_WIDTH), lambda i: (i, 0)),
        out_shape=jax.ShapeDtypeStruct((SEQ, BATCH * S5_WIDTH), F32),
        scratch_shapes=[pltpu.VMEM((rows, 2 * S5_COLS), F32),
                        pltpu.VMEM((BATCH, 2 * S5_COLS), F32)],
        compiler_params=_params(("arbitrary",)),
        name="s5",
    )(u_tb, bre, bim, cre, cimn, are, aim, dflat)


def _s5_tables(lam_re, lam_im, log_dt, b_re, b_im, c_re, c_im, d_skip):
    lr = lam_re.astype(F32)
    li = lam_im.astype(F32)
    dt = jnp.exp(log_dt.astype(F32))[:, None]
    mag = jnp.exp(lr * dt)
    abar_re = mag * jnp.cos(li * dt)
    abar_im = mag * jnp.sin(li * dt)
    den = lr * lr + li * li
    num_re = abar_re - 1.0
    coef_re = (num_re * lr + abar_im * li) / den
    coef_im = (abar_im * lr - num_re * li) / den
    cr, ci = coef_re[..., None], coef_im[..., None]
    br, bi = b_re.astype(F32), b_im.astype(F32)
    bbar_re = cr * br - ci * bi
    bbar_im = cr * bi + ci * br
    eye = jnp.eye(S5_GROUPS // 2, dtype=F32)

    def in_map(bb):
        bb = bb.reshape(2, S5_GROUPS // 2, S5_STATE, S5_GROUP)
        return jnp.einsum('jgph,gk->jghkp', bb, eye).reshape(2, 256, S5_COLS // 2).astype(BF16)

    def out_map(cc):
        cc = cc.reshape(2, S5_GROUPS // 2, S5_GROUP, S5_STATE)
        return jnp.einsum('jghp,gk->jgpkh', cc, eye).reshape(2, S5_COLS // 2, 256).astype(BF16)

    are = jnp.broadcast_to(abar_re.reshape(1, S5_COLS), (BATCH, S5_COLS))
    aim = jnp.broadcast_to(abar_im.reshape(1, S5_COLS), (BATCH, S5_COLS))
    return (in_map(bbar_re), in_map(bbar_im), out_map(c_re.astype(F32)), out_map(-c_im.astype(F32)),
            are, aim, d_skip.astype(F32).reshape(1, S5_WIDTH))


def _gla_kernel(q_ref, k_ref, la_ref, v_ref, rs_ref, ng_ref, tt_ref, o_ref, st_ref):
    @pl.when(pl.program_id(1) == 0)
    def _():
        st_ref[...] = jnp.zeros_like(st_ref)

    c = GLA_CHUNK
    tt = tt_ref[...]
    row = lax.broadcasted_iota(I32, (c, GLA_KEY), 0)
    ri = lax.broadcasted_iota(I32, (c, c), 0)
    ci = lax.broadcasted_iota(I32, (c, c), 1)
    causal = ci <= ri
    nsub = c // GLA_SUB

    heads = range(GLA_HEADS)
    hsl = [slice(h * GLA_DK, (h + 1) * GLA_DK) for h in heads]
    vsl = [slice(h * GLA_DV, (h + 1) * GLA_DV) for h in heads]
    chunks = range(GLA_TG // c)
    intra, kv, qdec, dec = [], [], [], []

    sls = [slice(ch * c, (ch + 1) * c) for ch in chunks]
    csum = []
    for ch in chunks:
        g = la_ref[0, sls[ch], :]
        g1 = g.astype(BF16)
        r1 = g - g1.astype(F32)
        g2 = r1.astype(BF16)
        g3 = (r1 - g2.astype(F32)).astype(BF16)
        csum.append(_dot(tt, g1) + _dot(tt, g2) + _dot(tt, g3))
    qbigs, kbigs, kdec = [], [], []
    for ch in chunks:
        sl = sls[ch]
        cs = csum[ch]
        b = cs[0:c]
        cl = cs[c:2 * c]
        ref_pt = b - cl
        q = q_ref[0, sl, :]
        k = k_ref[0, sl, :]
        qt = q * jnp.exp(cl)
        qe = (q * jnp.exp(b)).astype(BF16)
        blast = b[c - 1:c, :]
        ks = (k * jnp.exp(blast - b)).astype(BF16)
        k_sub = []
        q_sub = []
        for s in range(nsub):
            rs_ = ref_pt[s * GLA_SUB:s * GLA_SUB + 1, :]
            k_sub.append(k * jnp.exp(jnp.minimum(rs_ - b, EXP_CLAMP)))
            q_sub.append(jnp.where((row >= s * GLA_SUB) & (row < (s + 1) * GLA_SUB), qt, 0.0))
        qbigs.append([jnp.concatenate([x[:, hsl[h]] for x in q_sub], axis=1).astype(BF16)
                      for h in heads])
        kbigs.append([jnp.concatenate([x[:, hsl[h]] for x in k_sub], axis=1).astype(BF16)
                      for h in heads])
        kdec.append(ks)
        qdec.append(qe)
        dec.append(jnp.exp(blast))
    scores = [[_dot_nt(qbigs[ch][h], kbigs[ch][h]) for h in heads] for ch in chunks]
    for ch in chunks:
        kv.append([_dot(v_ref[0, sls[ch], vsl[h]].T.astype(BF16), kdec[ch][:, hsl[h]])
                   for h in heads])
    for ch in chunks:
        intra.append([_dot(jnp.where(causal, scores[ch][h], 0.0).astype(BF16),
                           v_ref[0, sls[ch], vsl[h]].astype(BF16)) for h in heads])

    st = [st_ref[h] for h in heads]
    for ch in chunks:
        sl = slice(ch * c, (ch + 1) * c)
        inter = [_dot_nt(qdec[ch][:, hsl[h]], st[h].astype(BF16)) for h in heads]
        for h in heads:
            o = inter[h] + intra[ch][h]
            st[h] = st[h] * dec[ch][:, hsl[h]] + kv[ch][h]
            on = o * lax.rsqrt(jnp.mean(o * o, axis=-1, keepdims=True) + EPS) * ng_ref[...]
            o_ref[0, sl, vsl[h]] = on * rs_ref[0, sl, vsl[h]]
    for h in heads:
        st_ref[h] = st[h]


def _gla(q, k, la, v, rs, norm_g):
    tg = GLA_TG
    c = GLA_CHUNK
    r = jnp.arange(c)
    tri = (r[None, :] <= r[:, None])
    blk = tri & ((r[None, :] // GLA_SUB) == (r[:, None] // GLA_SUB))
    tt = jnp.concatenate([tri, blk], axis=0).astype(BF16)
    row3 = lambda w: pl.BlockSpec((1, tg, w), lambda b, i: (b, i, 0))
    const = lambda shape: pl.BlockSpec(shape, lambda b, i: tuple(0 for _ in shape))
    return pl.pallas_call(
        _gla_kernel,
        grid=(BATCH, SEQ // tg),
        in_specs=[row3(GLA_KEY), row3(GLA_KEY), row3(GLA_KEY), row3(GLA_VAL), row3(GLA_VAL),
                  const((1, GLA_DV)), const((2 * c, c))],
        out_specs=row3(GLA_VAL),
        out_shape=jax.ShapeDtypeStruct((BATCH, SEQ, GLA_VAL), F32),
        scratch_shapes=[pltpu.VMEM((GLA_HEADS, GLA_DV, GLA_DK), F32)],
        compiler_params=_params(("parallel", "arbitrary")),
        name="gla",
    )(q, k, la, v, rs, norm_g.reshape(1, GLA_DV), tt)


def _merge_kernel(ya_ref, yb_ref, ga_ref, gb_ref, x_ref, g1_ref, sc2_ref, sh2_ref, g2_ref, n2_ref,
                  wglu_ref, bglu_ref, wpa_ref, wpb_ref, wout_ref, wr_ref, wsgu_ref, wsd_ref,
                  base_ref, hp_ref, sc_ref):
    rows = MERGE_TM // MERGE_SPLIT
    subs = [slice(s * rows, (s + 1) * rows) for s in range(MERGE_SPLIT)]
    ya = [ya_ref[sl, :] for sl in subs]
    z = [_dot(a.astype(BF16), wglu_ref[...]) for a in ya]
    pb = [_dot(yb_ref[0, sl, :].astype(BF16), wpb_ref[...]) for sl in subs]
    ya2 = [(a * _sigmoid(zz + bglu_ref[...])).astype(BF16) for a, zz in zip(ya, z)]
    pa = [_dot(a, wpa_ref[...]) for a in ya2]
    mixed = [(ga_ref[0, sl, :] * p + gb_ref[0, sl, :] * q).astype(BF16)
             for sl, p, q in zip(subs, pa, pb)]
    mo = [_dot(m, wout_ref[...]) for m in mixed]
    x1 = [x_ref[0, sl, :] + g1_ref[0] * m for sl, m in zip(subs, mo)]
    h = [(x * lax.rsqrt(jnp.mean(x * x, axis=-1, keepdims=True) + EPS)) * n2_ref[...]
         * (1.0 + sc2_ref[0]) + sh2_ref[0] for x in x1]
    hb = [v.astype(BF16) for v in h]
    logits = [_dot(v, wr_ref[...]) for v in hb]
    gu = [_dot(v, wsgu_ref[...]) for v in hb]
    mid = [(g[:, :EXPERT_FF] * _sigmoid(g[:, :EXPERT_FF]) * g[:, EXPERT_FF:]).astype(BF16) for g in gu]
    shared = [_dot(m, wsd_ref[...]) for m in mid]
    for s, sl in enumerate(subs):
        sc_ref[sl, :] = _sigmoid(logits[s])
        base_ref[sl] = (x1[s] + g2_ref[0] * shared[s]).reshape(rows, *TOKEN_TILE)
        hp_ref[sl] = h[s].reshape(rows, *TOKEN_TILE).astype(BF16)


def _merge(ya_tb, yb, ga, gb, x, g1, sc2, sh2, g2, n2, wglu, bglu, wpa, wpb, wout, wr, wsgu, wsd):
    tm = MERGE_TM
    row3 = lambda w: pl.BlockSpec((1, tm, w), lambda b, i: (b, i, 0))
    const = lambda a: pl.BlockSpec(a.shape, lambda b, i: tuple(0 for _ in a.shape))
    mod = pl.BlockSpec((1, 1, D_MODEL), lambda b, i: (b, 0, 0))
    nt = SEQ // tm
    flat = lambda w: pl.BlockSpec((tm, w), lambda b, i: (b * nt + i, 0))
    tiles = pl.BlockSpec((tm,) + TOKEN_TILE, lambda b, i: (b * nt + i, 0, 0))
    return pl.pallas_call(
        _merge_kernel,
        grid=(BATCH, nt),
        in_specs=[pl.BlockSpec((tm, S5_WIDTH), lambda b, i: (i, b)),
                  row3(GLA_VAL), row3(D_MODEL), row3(D_MODEL), row3(D_MODEL),
                  mod, mod, mod, mod, const(n2),
                  const(wglu), const(bglu), const(wpa), const(wpb), const(wout), const(wr),
                  const(wsgu), const(wsd)],
        out_specs=[tiles, tiles, flat(N_EXPERTS)],
        out_shape=[jax.ShapeDtypeStruct((TOKENS,) + TOKEN_TILE, F32),
                   jax.ShapeDtypeStruct((TOKENS,) + TOKEN_TILE, BF16),
                   jax.ShapeDtypeStruct((TOKENS, N_EXPERTS), F32)],
        compiler_params=_params(("parallel", "parallel")),
        name="merge",
    )(ya_tb, yb, ga, gb, x, g1, sc2, sh2, g2, n2, wglu, bglu, wpa, wpb, wout, wr, wsgu, wsd)


def _first_argmax(x, iota, size):
    m = jnp.max(x, axis=0, keepdims=True)
    idx = jnp.min(jnp.where(x == m, iota, size), axis=0, keepdims=True)
    return m, idx


def _route_kernel(s_ref, bias_ref, tri_ref, e_ref, r_ref, w_ref, cnt_ref, carry_ref):
    @pl.when(pl.program_id(0) == 0)
    def _():
        carry_ref[...] = jnp.zeros_like(carry_ref)

    tr = ROUTE_TR
    neg = -jnp.inf
    s_t = s_ref[...].T
    biased = s_t + bias_ref[...]
    io_g = lax.broadcasted_iota(I32, (GROUP_SIZE, tr), 0)
    rows = []
    for g in range(N_GROUPS):
        xg = biased[g * GROUP_SIZE:(g + 1) * GROUP_SIZE, :]
        m1, i1 = _first_argmax(xg, io_g, GROUP_SIZE)
        m2 = jnp.max(jnp.where(io_g == i1, neg, xg), axis=0, keepdims=True)
        rows.append(m1 + m2)
    gs = jnp.concatenate(rows, axis=0)
    io_n = lax.broadcasted_iota(I32, (N_GROUPS, tr), 0)
    gsel = jnp.zeros((N_GROUPS, tr), F32)
    for _ in range(TOPK_GROUPS):
        _, gi = _first_argmax(gs, io_n, N_GROUPS)
        hit = io_n == gi
        gsel = jnp.where(hit, 1.0, gsel)
        gs = jnp.where(hit, neg, gs)
    masked = jnp.concatenate(
        [jnp.where(gsel[g:g + 1, :] > 0.0, biased[g * GROUP_SIZE:(g + 1) * GROUP_SIZE, :], neg)
         for g in range(N_GROUPS)], axis=0)
    io_e = lax.broadcasted_iota(I32, (N_EXPERTS, tr), 0)
    sel = jnp.zeros((N_EXPERTS, tr), F32)
    idxs = []
    for _ in range(TOP_K):
        _, ei = _first_argmax(masked, io_e, N_EXPERTS)
        hit = io_e == ei
        sel = jnp.where(hit, 1.0, sel)
        masked = jnp.where(hit, neg, masked)
        idxs.append(ei)
    den = jnp.sum(sel * s_t, axis=0, keepdims=True)
    rank = _dot(sel.astype(BF16), tri_ref[...]) + carry_ref[...]
    rks, sks = [], []
    for kk in range(TOP_K):
        hit = io_e == idxs[kk]
        rks.append(jnp.sum(jnp.where(hit, rank, 0.0), axis=0, keepdims=True))
        sks.append(jnp.sum(jnp.where(hit, s_t, 0.0), axis=0, keepdims=True))
    e_ref[...] = jnp.concatenate(idxs, axis=0)
    r_ref[...] = jnp.concatenate(rks, axis=0).astype(I32)
    w_ref[...] = jnp.concatenate(sks, axis=0) / den * ROUTE_SCALE
    carry_ref[...] += jnp.sum(sel, axis=1, keepdims=True)
    cnt_ref[...] = carry_ref[...]


def _route(scores, bias):
    tr = ROUTE_TR
    r = jnp.arange(tr)
    tri = (r[:, None] < r[None, :]).astype(BF16)
    kt = lambda dt: jax.ShapeDtypeStruct((TOP_K, TOKENS), dt)
    blk = pl.BlockSpec((TOP_K, tr), lambda i: (0, i))
    return pl.pallas_call(
        _route_kernel,
        grid=(TOKENS // tr,),
        in_specs=[pl.BlockSpec((tr, N_EXPERTS), lambda i: (i, 0)),
                  pl.BlockSpec((N_EXPERTS, 1), lambda i: (0, 0)),
                  pl.BlockSpec((tr, tr), lambda i: (0, 0))],
        out_specs=[blk, blk, blk, pl.BlockSpec((N_EXPERTS, 1), lambda i: (0, 0))],
        out_shape=[kt(I32), kt(I32), kt(F32), jax.ShapeDtypeStruct((N_EXPERTS, 1), F32)],
        scratch_shapes=[pltpu.VMEM((N_EXPERTS, 1), F32)],
        compiler_params=_params(("arbitrary",)),
        name="route",
    )(scores, bias.reshape(N_EXPERTS, 1), tri)


def _slots_kernel(e_ref, r_ref, ps_ref, d_ref):
    tr = ROUTE_TR
    io_e = lax.broadcasted_iota(I32, (N_EXPERTS, tr), 0)
    ps = ps_ref[...]
    rows = []
    for kk in range(TOP_K):
        hit = io_e == e_ref[kk:kk + 1, :]
        rows.append(jnp.sum(jnp.where(hit, ps, 0.0), axis=0, keepdims=True))
    d_ref[...] = jnp.concatenate(rows, axis=0).astype(I32) + r_ref[...]


def _slots(e_idx, rank, pad_start):
    tr = ROUTE_TR
    blk = pl.BlockSpec((TOP_K, tr), lambda i: (0, i))
    return pl.pallas_call(
        _slots_kernel,
        grid=(TOKENS // tr,),
        in_specs=[blk, blk, pl.BlockSpec((N_EXPERTS, 1), lambda i: (0, 0))],
        out_specs=blk,
        out_shape=jax.ShapeDtypeStruct((TOP_K, TOKENS), I32),
        compiler_params=_params(("parallel",)),
        name="slots",
    )(e_idx, rank, pad_start.astype(F32).reshape(N_EXPERTS, 1))


def _dispatch_kernel(ps_ref, cnt_ref, nb_ref, dest_ref, h_ref, xs_ref, zbuf, sem, zsem):
    tq = DISPATCH_TQ
    step = pl.program_id(0)
    n_used = ps_ref[N_EXPERTS - 1] // MOE_BLOCK + nb_ref[N_EXPERTS - 1]

    def zero_fill(act):
        def block(j, carry):
            r = pl.ds(pl.multiple_of(j * MOE_BLOCK, MOE_BLOCK), MOE_BLOCK)
            act(pltpu.make_async_copy(zbuf, xs_ref.at[r], zsem))
            return carry

        def expert(e, carry):
            @pl.when(cnt_ref[e] < nb_ref[e] * MOE_BLOCK)
            def _():
                block(ps_ref[e] // MOE_BLOCK + nb_ref[e] - 1, 0)
            return carry

        lax.fori_loop(0, N_EXPERTS, expert, 0)
        lax.fori_loop(n_used, N_BLOCKS, block, 0)

    @pl.when(step == 0)
    def _():
        zbuf[...] = jnp.zeros_like(zbuf)
        zero_fill(lambda cp: cp.start())
        zero_fill(lambda cp: cp.wait())

    def start(i, carry):
        for kk in range(TOP_K):
            d = dest_ref[i * TOP_K + kk]
            pltpu.make_async_copy(h_ref.at[i], xs_ref.at[d],
                                  sem).start(priority=kk % 2)
        return carry

    lax.fori_loop(0, tq, start, 0, unroll=2)
    for _ in range(TOP_K):
        pltpu.make_async_copy(h_ref, xs_ref.at[pl.ds(0, tq)], sem).wait()


def _dispatch(pad_start, counts, n_blk, dest, hp):
    tq = DISPATCH_TQ
    grid_spec = pltpu.PrefetchScalarGridSpec(
        num_scalar_prefetch=3,
        grid=(TOKENS // tq,),
        in_specs=[pl.BlockSpec((tq * TOP_K,), lambda i, *_: (i,), memory_space=pltpu.SMEM),
                  pl.BlockSpec((tq,) + TOKEN_TILE, lambda i, *_: (i, 0, 0))],
        out_specs=pl.BlockSpec(memory_space=pl.ANY),
        scratch_shapes=[pltpu.VMEM((MOE_BLOCK,) + TOKEN_TILE, hp.dtype),
                        pltpu.SemaphoreType.DMA, pltpu.SemaphoreType.DMA],
    )
    return pl.pallas_call(
        _dispatch_kernel,
        grid_spec=grid_spec,
        out_shape=jax.ShapeDtypeStruct((N_SLOTS,) + TOKEN_TILE, hp.dtype),
        compiler_params=_params(("arbitrary",)),
        name="dispatch",
    )(pad_start, counts, n_blk, dest, hp)


def _expert_kernel(ps_ref, nb_ref, wg_ref, wu_ref, wd_ref, xs_ref, ys_ref,
                   xbuf, ybuf, wgus, wds, xsem, ysem):
    e = pl.program_id(0)
    nb = nb_ref[e]
    first = ps_ref[e] // MOE_BLOCK
    n_used = ps_ref[N_EXPERTS - 1] // MOE_BLOCK + nb_ref[N_EXPERTS - 1]
    nbuf = EXPERT_NBUF

    def rows(g):
        return pl.ds(pl.multiple_of(g * MOE_BLOCK, MOE_BLOCK), MOE_BLOCK)

    def x_copy(g):
        slot = g % nbuf
        return pltpu.make_async_copy(xs_ref.at[rows(g)], xbuf.at[slot], xsem.at[slot])

    def y_copy(g):
        slot = g % nbuf
        return pltpu.make_async_copy(ybuf.at[slot], ys_ref.at[rows(g)], ysem.at[slot])

    look = nbuf - EXPERT_UNIT

    @pl.when(e == 0)
    def _():
        for g in range(look):
            @pl.when(g < n_used)
            def _():
                x_copy(g).start()

    def run(g, u):
        for d in range(u):
            @pl.when(g + look + d < n_used)
            def _():
                x_copy(g + look + d).start()

        for d in range(u):
            b = g + d

            @pl.when(b >= nbuf)
            def _():
                y_copy(b - nbuf).wait()

        for d in range(u):
            x_copy(g + d).wait()
        gus = [_dot(xbuf[(g + d) % nbuf].astype(F32).reshape(MOE_BLOCK, D_MODEL).astype(BF16), wgus[...])
               for d in range(u)]
        for d in range(u):
            gate = gus[d][:, :EXPERT_FF]
            mid = (gate * _sigmoid(gate) * gus[d][:, EXPERT_FF:]).astype(BF16)
            ybuf[(g + d) % nbuf] = _dot(mid, wds[...]).reshape(MOE_BLOCK, *TOKEN_TILE)
        for d in range(u):
            y_copy(g + d).start()

    @pl.when(nb > 0)
    def _():
        wgus[:, :EXPERT_FF] = wg_ref[0].astype(BF16)
        wgus[:, EXPERT_FF:] = wu_ref[0].astype(BF16)
        wds[...] = wd_ref[0].astype(BF16)

        whole = nb == EXPERT_UNIT
        triple = (nb % 2 == 1) & (nb >= 3)
        pairs = jnp.where(whole, 0, jnp.where(triple, (nb - 3) // 2, nb // 2))

        def pair(j, carry):
            run(first + j * 2, 2)
            return carry

        lax.fori_loop(0, pairs, pair, 0)

        @pl.when(whole)
        def _():
            run(first, EXPERT_UNIT)

        @pl.when(triple)
        def _():
            run(first + nb - 3, 3)

        @pl.when(nb == 1)
        def _():
            run(first, 1)

    @pl.when(e == N_EXPERTS - 1)
    def _():
        for d in range(nbuf):
            g = n_used - nbuf + d

            @pl.when(g >= 0)
            def _():
                y_copy(g).wait()

        ybuf[0] = jnp.zeros((MOE_BLOCK,) + TOKEN_TILE, F32)

        def z_copy(j):
            return pltpu.make_async_copy(ybuf.at[0], ys_ref.at[rows(j)], ysem.at[0])

        def z_start(j, carry):
            z_copy(j).start()
            return carry

        def z_wait(j, carry):
            z_copy(j).wait()
            return carry

        lax.fori_loop(n_used, N_BLOCKS, z_start, 0)
        lax.fori_loop(n_used, N_BLOCKS, z_wait, 0)


def _experts(pad_start, n_blk, xs, wg, wu, wd):
    wspec = lambda shape: pl.BlockSpec((1,) + shape, lambda e, ps, nb: (e, 0, 0))
    grid_spec = pltpu.PrefetchScalarGridSpec(
        num_scalar_prefetch=2,
        grid=(N_EXPERTS,),
        in_specs=[wspec((D_MODEL, EXPERT_FF)), wspec((D_MODEL, EXPERT_FF)),
                  wspec((EXPERT_FF, D_MODEL)), pl.BlockSpec(memory_space=pl.ANY)],
        out_specs=pl.BlockSpec(memory_space=pl.ANY),
        scratch_shapes=[pltpu.VMEM((EXPERT_NBUF, MOE_BLOCK) + TOKEN_TILE, xs.dtype),
                        pltpu.VMEM((EXPERT_NBUF, MOE_BLOCK) + TOKEN_TILE, F32),
                        pltpu.VMEM((D_MODEL, 2 * EXPERT_FF), BF16),
                        pltpu.VMEM((EXPERT_FF, D_MODEL), BF16),
                        pltpu.SemaphoreType.DMA((EXPERT_NBUF,)),
                        pltpu.SemaphoreType.DMA((EXPERT_NBUF,))],
    )
    return pl.pallas_call(
        _expert_kernel,
        grid_spec=grid_spec,
        out_shape=jax.ShapeDtypeStruct((N_SLOTS,) + TOKEN_TILE, F32),
        compiler_params=_params(("arbitrary",)),
        name="experts",
    )(pad_start, n_blk, wg, wu, wd, xs)


def _combine_kernel(dcur_ref, dnext_ref, w_ref, base_ref, g2_ref, fg_ref, ys_ref, o_ref,
                    buf, acc_ref, sem):
    tq = COMBINE_TQ
    i = pl.program_id(0)
    n = pl.num_programs(0)

    def start(dest_ref, to_slot, t):
        for kk in range(TOP_K):
            d = dest_ref[t * TOP_K + kk]
            pltpu.make_async_copy(ys_ref.at[d], buf.at[to_slot, kk, t],
                                  sem.at[to_slot]).start(priority=kk % 2)

    def reduce(slot, t):
        a = buf[slot, 0, t] * w_ref[t * TOP_K]
        for kk in range(1, TOP_K):
            a = a + buf[slot, kk, t] * w_ref[t * TOP_K + kk]
        acc_ref[t] = a

    @pl.when(i == 0)
    def _():
        lax.fori_loop(0, tq, lambda t, c: start(dcur_ref, 0, t) or c, 0, unroll=4)

    for slot in range(2):
        @pl.when(i % 2 == slot)
        def _(slot=slot):
            @pl.when(i + 1 < n)
            def _():
                lax.fori_loop(0, tq, lambda t, c: start(dnext_ref, 1 - slot, t) or c, 0, unroll=4)

            for kk in range(TOP_K):
                pltpu.make_async_copy(ys_ref.at[pl.ds(0, tq)], buf.at[slot, kk],
                                      sem.at[slot]).wait()
            lax.fori_loop(0, tq, lambda t, c: reduce(slot, t) or c, 0, unroll=8)

    xo = base_ref[...] + g2_ref[0] * acc_ref[...]
    ms = jnp.sum(jnp.sum(xo * xo, axis=2, keepdims=True), axis=1, keepdims=True) * (1.0 / D_MODEL)
    o_ref[0] = ((xo * lax.rsqrt(ms + EPS)) * fg_ref[...]).reshape(tq, D_MODEL)


def _combine(dest, w_flat, base, g2, final_g, ys):
    tq = COMBINE_TQ
    nt = SEQ // tq
    n = TOKENS // tq
    smem = lambda imap: pl.BlockSpec((tq * TOP_K,), imap, memory_space=pltpu.SMEM)
    return pl.pallas_call(
        _combine_kernel,
        grid=(n,),
        in_specs=[smem(lambda i: (i,)), smem(lambda i: (jnp.minimum(i + 1, n - 1),)),
                  smem(lambda i: (i,)),
                  pl.BlockSpec((tq,) + TOKEN_TILE, lambda i: (i, 0, 0)),
                  pl.BlockSpec((1,) + TOKEN_TILE, lambda i: (i // nt, 0, 0)),
                  pl.BlockSpec(TOKEN_TILE, lambda i: (0, 0)),
                  pl.BlockSpec(memory_space=pl.ANY)],
        out_specs=pl.BlockSpec((1, tq, D_MODEL), lambda i: (i // nt, i % nt, 0)),
        out_shape=jax.ShapeDtypeStruct((BATCH, SEQ, D_MODEL), F32),
        scratch_shapes=[pltpu.VMEM((2, TOP_K, tq) + TOKEN_TILE, F32),
                        pltpu.VMEM((tq,) + TOKEN_TILE, F32),
                        pltpu.SemaphoreType.DMA((2,))],
        compiler_params=_params(("arbitrary",)),
        name="combine",
    )(dest, dest, w_flat, base, g2.reshape((BATCH,) + TOKEN_TILE), final_g.reshape(TOKEN_TILE), ys)


def kernel(x, c, ada_w, ada_b, norm1_g, w_in, s5_lam_re, s5_lam_im, s5_log_dt, s5_b_re, s5_b_im,
           s5_c_re, s5_c_im, s5_d, s5_w_glu, s5_b_glu, w_proj_a, gla_w_gk2, gla_b_gk2, gla_norm_g,
           w_proj_b, w_out, norm2_g, router_w, router_bias, exp_w_gate, exp_w_up, exp_w_down,
           sh_w_gate, sh_w_up, sh_w_down, final_g):
    l = 0
    mod = _ada(c, ada_w[l], ada_b[l])
    sh1, sc1, g1, sh2, sc2, g2 = [m.reshape(BATCH, 1, D_MODEL) for m in jnp.split(mod, 6, axis=-1)]

    w = w_in[l]
    gk0 = _V0 + GLA_VAL
    w_main = jnp.concatenate([w[:, :gk0], w[:, gk0 + GLA_GATE_RANK:]], axis=1).astype(BF16)
    w_gk = jnp.pad(w[:, gk0:gk0 + GLA_GATE_RANK], ((0, 0), (0, LANES - GLA_GATE_RANK))).astype(BF16)
    w_gk2 = jnp.pad(gla_w_gk2[l], ((0, LANES - GLA_GATE_RANK), (0, 0))).astype(BF16)
    u_tb, q, k, v, la, rs, ga, gb = _inproj(
        x, norm1_g[l].reshape(1, D_MODEL), sc1, sh1, w_main, w_gk, w_gk2,
        gla_b_gk2[l].reshape(1, GLA_KEY))
    tables = _s5_tables(s5_lam_re[l], s5_lam_im[l], s5_log_dt[l], s5_b_re[l], s5_b_im[l],
                        s5_c_re[l], s5_c_im[l], s5_d[l])
    ya_tb = _s5(u_tb, *tables)
    yb = _gla(q, k, la, v, rs, gla_norm_g[l])

    wsgu = jnp.concatenate([sh_w_gate[l], sh_w_up[l]], axis=1).astype(BF16)
    base, hp, scores = _merge(
        ya_tb, yb, ga, gb, x, g1, sc2, sh2, g2,
        norm2_g[l].reshape(1, D_MODEL), s5_w_glu[l].astype(BF16), s5_b_glu[l].reshape(1, S5_WIDTH),
        w_proj_a[l].astype(BF16), w_proj_b[l].astype(BF16), w_out[l].astype(BF16),
        router_w[l].astype(BF16), wsgu, sh_w_down[l].astype(BF16))

    e_idx, rank, w_k, cnt = _route(scores, router_bias[l])
    counts = cnt[:, 0].astype(I32)
    padded = (counts + MOE_BLOCK - 1) // MOE_BLOCK * MOE_BLOCK
    pad_end = jnp.cumsum(padded)
    pad_start = pad_end - padded
    dest = _slots(e_idx, rank, pad_start).T.reshape(-1)
    pad_start = pad_start.astype(I32)
    n_blk = (padded // MOE_BLOCK).astype(I32)
    xs = _dispatch(pad_start, counts, n_blk, dest, hp)
    ys = _experts(pad_start, n_blk, xs, exp_w_gate[l], exp_w_up[l], exp_w_down[l])
    return _combine(dest, w_k.T.reshape(-1), base, g2, final_g, ys)
```

```python
import jax
import jax.numpy as jnp
from jax import lax
from jax.experimental import pallas as pl
from jax.experimental.pallas import tpu as pltpu

F32 = jnp.float32
BF16 = jnp.bfloat16
I32 = jnp.int32

D_MODEL = 1024
BATCH = 8
SEQ = 2048
TOKENS = BATCH * SEQ
S5_WIDTH = 512
S5_GROUP = 16
S5_GROUPS = 32
S5_STATE = 64
S5_COLS = S5_GROUPS * S5_STATE
GLA_HEADS = 4
GLA_DK = 64
GLA_DV = 128
GLA_KEY = GLA_HEADS * GLA_DK
GLA_VAL = GLA_HEADS * GLA_DV
GLA_GATE_RANK = 16
GLA_GATE_TAU = 16.0
GLA_CHUNK = 64
GLA_SUB = 16
N_EXPERTS = 256
TOP_K = 8
N_GROUPS = 8
GROUP_SIZE = N_EXPERTS // N_GROUPS
TOPK_GROUPS = 4
EXPERT_FF = 256
ROUTE_SCALE = 2.5
MOE_BLOCK = 128
EPS = 1e-6
N_SLOTS = -(-(TOKENS * TOP_K + N_EXPERTS * (MOE_BLOCK - 1)) // MOE_BLOCK) * MOE_BLOCK
N_BLOCKS = N_SLOTS // MOE_BLOCK

LANES = 128
SUBLANES = 8
TOKEN_TILE = (SUBLANES, LANES)
assert D_MODEL == SUBLANES * LANES
VMEM_LIMIT = 56 * 1024 * 1024

ADA_TN = 1536
INPROJ_TM = 512
S5_TC = 128
S5_RB = 32
S5_COL_CHUNK = 512
GLA_TG = 512
MERGE_TM = 512
MERGE_SPLIT = 2
ROUTE_TR = 512
DISPATCH_TQ = 1024
COMBINE_TQ = 256
EXPERT_UNIT = 4
EXPERT_NBUF = 10
EXPERT_OUT_DTYPE = BF16
EXP_CLAMP = 60.0


def _params(sem, vmem=VMEM_LIMIT):
    return pltpu.CompilerParams(dimension_semantics=sem, vmem_limit_bytes=vmem)


def _dot(a, b):
    return jnp.dot(a, b, preferred_element_type=F32)


def _dot_nt(a, b):
    return lax.dot_general(a, b, (((1,), (1,)), ((), ())), preferred_element_type=F32)


def _sigmoid(x):
    return jax.nn.sigmoid(x)


def _ada_kernel(c_ref, w_ref, b_ref, o_ref):
    c = c_ref[...]
    s = (c * _sigmoid(c)).astype(BF16)
    o_ref[...] = _dot(s, w_ref[...].astype(BF16)) + b_ref[...]


def _ada(c, w, b):
    n = w.shape[1]
    return pl.pallas_call(
        _ada_kernel,
        grid=(n // ADA_TN,),
        in_specs=[
            pl.BlockSpec((BATCH, D_MODEL), lambda j: (0, 0)),
            pl.BlockSpec((D_MODEL, ADA_TN), lambda j: (0, j)),
            pl.BlockSpec((1, ADA_TN), lambda j: (0, j)),
        ],
        out_specs=pl.BlockSpec((BATCH, ADA_TN), lambda j: (0, j)),
        out_shape=jax.ShapeDtypeStruct((BATCH, n), F32),
        compiler_params=_params(("arbitrary",)),
        name="ada",
    )(c, w, b.reshape(1, n))


_U0, _Q0, _K0, _V0, _R0, _GA0, _GB0, _END = 0, 512, 768, 1024, 1536, 2048, 3072, 4096


def _inproj_kernel(x_ref, g_ref, sc_ref, sh_ref, wm_ref, wgk_ref, wgk2_ref, bgk_ref,
                   u_ref, q_ref, k_ref, v_ref, la_ref, rs_ref, ga_ref, gb_ref):
    x = x_ref[0]
    r = lax.rsqrt(jnp.mean(x * x, axis=-1, keepdims=True) + EPS)
    h = (x * r) * g_ref[...] * (1.0 + sc_ref[0]) + sh_ref[0]
    hb = h.astype(BF16)

    def seg(lo, hi):
        return _dot(hb, wm_ref[:, lo:hi])

    u_ref[...] = seg(_U0, _Q0)
    q_ref[0] = seg(_Q0, _K0) * (GLA_DK ** -0.5)
    k_ref[0] = seg(_K0, _V0)
    v_ref[0] = seg(_V0, _R0)
    rr = seg(_R0, _GA0)
    rs_ref[0] = rr * _sigmoid(rr)
    ga_ref[0] = _sigmoid(seg(_GA0, _GB0))
    gb_ref[0] = _sigmoid(seg(_GB0, _END))
    gk = _dot(hb, wgk_ref[...])
    z = _dot(gk.astype(BF16), wgk2_ref[...]) + bgk_ref[...]
    la_ref[0] = -(jnp.maximum(-z, 0.0) + jnp.log1p(jnp.exp(-jnp.abs(z)))) * (1.0 / GLA_GATE_TAU)


def _inproj(x, g, sc, sh, w_main, w_gk, w_gk2, b_gk2):
    tm = INPROJ_TM
    nt = SEQ // tm
    row3 = lambda w: pl.BlockSpec((1, tm, w), lambda b, i: (b, i, 0))
    const = lambda shape: pl.BlockSpec(shape, lambda b, i: tuple(0 for _ in shape))
    mod = pl.BlockSpec((1, 1, D_MODEL), lambda b, i: (b, 0, 0))
    bld = lambda w: jax.ShapeDtypeStruct((BATCH, SEQ, w), F32)
    return pl.pallas_call(
        _inproj_kernel,
        grid=(BATCH, nt),
        in_specs=[row3(D_MODEL), const((1, D_MODEL)), mod, mod,
                  const((D_MODEL, _END)), const((D_MODEL, LANES)), const((LANES, GLA_KEY)),
                  const((1, GLA_KEY))],
        out_specs=[pl.BlockSpec((tm, S5_WIDTH), lambda b, i: (i, b)),
                   row3(GLA_KEY), row3(GLA_KEY), row3(GLA_VAL), row3(GLA_KEY), row3(GLA_VAL),
                   row3(D_MODEL), row3(D_MODEL)],
        out_shape=[jax.ShapeDtypeStruct((SEQ, BATCH * S5_WIDTH), F32),
                   bld(GLA_KEY), bld(GLA_KEY), bld(GLA_VAL), bld(GLA_KEY), bld(GLA_VAL),
                   bld(D_MODEL), bld(D_MODEL)],
        compiler_params=_params(("parallel", "parallel")),
        name="inproj",
    )(x, g, sc, sh, w_main, w_gk, w_gk2, b_gk2)


def _s5_kernel(u_ref, bre_ref, bim_ref, cre_ref, cim_ref, are_ref, aim_ref, d_ref,
               y_ref, s_ref, st_ref):
    half = S5_COLS // 2

    @pl.when(pl.program_id(0) == 0)
    def _():
        st_ref[...] = jnp.zeros_like(st_ref)

    u = u_ref[...].reshape(S5_TC, BATCH, S5_WIDTH).reshape(S5_TC * BATCH, S5_WIDTH)
    ub = u.astype(BF16)
    nblk = S5_TC // S5_RB
    brows = [slice(r * S5_RB * BATCH, (r + 1) * S5_RB * BATCH) for r in range(nblk)]
    for r in range(nblk):
        for j in range(2):
            uj = ub[brows[r], j * 256:(j + 1) * 256]
            s_ref[brows[r], j * half:(j + 1) * half] = _dot(uj, bre_ref[j])
            s_ref[brows[r], S5_COLS + j * half:S5_COLS + (j + 1) * half] = _dot(uj, bim_ref[j])

    ncc = S5_COLS // S5_COL_CHUNK
    re_sl = [slice(cc * S5_COL_CHUNK, (cc + 1) * S5_COL_CHUNK) for cc in range(ncc)]
    im_sl = [slice(S5_COLS + cc * S5_COL_CHUNK, S5_COLS + (cc + 1) * S5_COL_CHUNK) for cc in range(ncc)]
    sr = [st_ref[:, sl] for sl in re_sl]
    si = [st_ref[:, sl] for sl in im_sl]
    for r in range(nblk):
        for t in range(r * S5_RB, (r + 1) * S5_RB):
            rows = slice(t * BATCH, (t + 1) * BATCH)
            for cc in range(ncc):
                ar = are_ref[:, re_sl[cc]]
                ai = aim_ref[:, re_sl[cc]]
                nr = ar * sr[cc] - ai * si[cc] + s_ref[rows, re_sl[cc]]
                ni = ar * si[cc] + ai * sr[cc] + s_ref[rows, im_sl[cc]]
                s_ref[rows, re_sl[cc]] = nr
                s_ref[rows, im_sl[cc]] = ni
                sr[cc], si[cc] = nr, ni
    for cc in range(ncc):
        st_ref[:, re_sl[cc]] = sr[cc]
        st_ref[:, im_sl[cc]] = si[cc]

    for r in range(nblk):
        ys = []
        for j in range(2):
            sre = s_ref[brows[r], j * half:(j + 1) * half].astype(BF16)
            sim = s_ref[brows[r], S5_COLS + j * half:S5_COLS + (j + 1) * half].astype(BF16)
            ys.append(_dot(sre, cre_ref[j]) + _dot(sim, cim_ref[j]))
        y = jnp.concatenate(ys, axis=1) + d_ref[...] * u[brows[r], :]
        y_ref[r * S5_RB:(r + 1) * S5_RB, :] = (
            jax.nn.gelu(y).reshape(S5_RB, BATCH, S5_WIDTH).reshape(S5_RB, BATCH * S5_WIDTH))


def _s5(u_tb, bre, bim, cre, cimn, are, aim, dflat):
    rows = S5_TC * BATCH
    const = lambda shape: pl.BlockSpec(shape, lambda i: tuple(0 for _ in shape))
    return pl.pallas_call(
        _s5_kernel,
        grid=(SEQ // S5_TC,),
        in_specs=[pl.BlockSpec((S5_TC, BATCH * S5_WIDTH), lambda i: (i, 0)),
                  const(bre.shape), const(bim.shape), const(cre.shape), const(cimn.shape),
                  const(are.shape), const(aim.shape), const(dflat.shape)],
        out_specs=pl.BlockSpec((S5_TC, BATCH * S5_WIDTH), lambda i: (i, 0)),
        out_shape=jax.ShapeDtypeStruct((SEQ, BATCH * S5_WIDTH), F32),
        scratch_shapes=[pltpu.VMEM((rows, 2 * S5_COLS), F32),
                        pltpu.VMEM((BATCH, 2 * S5_COLS), F32)],
        compiler_params=_params(("arbitrary",)),
        name="s5",
    )(u_tb, bre, bim, cre, cimn, are, aim, dflat)


def _s5_tables(lam_re, lam_im, log_dt, b_re, b_im, c_re, c_im, d_skip):
    lr = lam_re.astype(F32)
    li = lam_im.astype(F32)
    dt = jnp.exp(log_dt.astype(F32))[:, None]
    mag = jnp.exp(lr * dt)
    abar_re = mag * jnp.cos(li * dt)
    abar_im = mag * jnp.sin(li * dt)
    den = lr * lr + li * li
    num_re = abar_re - 1.0
    coef_re = (num_re * lr + abar_im * li) / den
    coef_im = (abar_im * lr - num_re * li) / den
    cr, ci = coef_re[..., None], coef_im[..., None]
    br, bi = b_re.astype(F32), b_im.astype(F32)
    bbar_re = cr * br - ci * bi
    bbar_im = cr * bi + ci * br
    eye = jnp.eye(S5_GROUPS // 2, dtype=F32)

    def in_map(bb):
        bb = bb.reshape(2, S5_GROUPS // 2, S5_STATE, S5_GROUP)
        return jnp.einsum('jgph,gk->jghkp', bb, eye).reshape(2, 256, S5_COLS // 2).astype(BF16)

    def out_map(cc):
        cc = cc.reshape(2, S5_GROUPS // 2, S5_GROUP, S5_STATE)
        return jnp.einsum('jghp,gk->jgpkh', cc, eye).reshape(2, S5_COLS // 2, 256).astype(BF16)

    are = jnp.broadcast_to(abar_re.reshape(1, S5_COLS), (BATCH, S5_COLS))
    aim = jnp.broadcast_to(abar_im.reshape(1, S5_COLS), (BATCH, S5_COLS))
    return (in_map(bbar_re), in_map(bbar_im), out_map(c_re.astype(F32)), out_map(-c_im.astype(F32)),
            are, aim, d_skip.astype(F32).reshape(1, S5_WIDTH))


def _gla_kernel(q_ref, k_ref, la_ref, v_ref, rs_ref, ng_ref, tt_ref, o_ref, st_ref):
    @pl.when(pl.program_id(1) == 0)
    def _():
        st_ref[...] = jnp.zeros_like(st_ref)

    c = GLA_CHUNK
    tt = tt_ref[...]
    row = lax.broadcasted_iota(I32, (c, GLA_KEY), 0)
    ri = lax.broadcasted_iota(I32, (c, c), 0)
    ci = lax.broadcasted_iota(I32, (c, c), 1)
    causal = ci <= ri
    nsub = c // GLA_SUB

    heads = range(GLA_HEADS)
    hsl = [slice(h * GLA_DK, (h + 1) * GLA_DK) for h in heads]
    vsl = [slice(h * GLA_DV, (h + 1) * GLA_DV) for h in heads]
    chunks = range(GLA_TG // c)
    intra, kv, qdec, dec = [], [], [], []

    sls = [slice(ch * c, (ch + 1) * c) for ch in chunks]
    csum = []
    for ch in chunks:
        g = la_ref[0, sls[ch], :]
        g1 = g.astype(BF16)
        r1 = g - g1.astype(F32)
        g2 = r1.astype(BF16)
        g3 = (r1 - g2.astype(F32)).astype(BF16)
        csum.append(_dot(tt, g1) + _dot(tt, g2) + _dot(tt, g3))
    qbigs, kbigs, kdec = [], [], []
    for ch in chunks:
        sl = sls[ch]
        cs = csum[ch]
        b = cs[0:c]
        cl = cs[c:2 * c]
        ref_pt = b - cl
        q = q_ref[0, sl, :]
        k = k_ref[0, sl, :]
        qt = q * jnp.exp(cl)
        qe = (q * jnp.exp(b)).astype(BF16)
        blast = b[c - 1:c, :]
        ks = (k * jnp.exp(blast - b)).astype(BF16)
        k_sub = []
        q_sub = []
        for s in range(nsub):
            rs_ = ref_pt[s * GLA_SUB:s * GLA_SUB + 1, :]
            k_sub.append(k * jnp.exp(jnp.minimum(rs_ - b, EXP_CLAMP)))
            q_sub.append(jnp.where((row >= s * GLA_SUB) & (row < (s + 1) * GLA_SUB), qt, 0.0))
        qbigs.append([jnp.concatenate([x[:, hsl[h]] for x in q_sub], axis=1).astype(BF16)
                      for h in heads])
        kbigs.append([jnp.concatenate([x[:, hsl[h]] for x in k_sub], axis=1).astype(BF16)
                      for h in heads])
        kdec.append(ks)
        qdec.append(qe)
        dec.append(jnp.exp(blast))
    scores = [[_dot_nt(qbigs[ch][h], kbigs[ch][h]) for h in heads] for ch in chunks]
    for ch in chunks:
        kv.append([_dot(v_ref[0, sls[ch], vsl[h]].T.astype(BF16), kdec[ch][:, hsl[h]])
                   for h in heads])
    for ch in chunks:
        intra.append([_dot(jnp.where(causal, scores[ch][h], 0.0).astype(BF16),
                           v_ref[0, sls[ch], vsl[h]].astype(BF16)) for h in heads])

    st = [st_ref[h] for h in heads]
    for ch in chunks:
        sl = slice(ch * c, (ch + 1) * c)
        inter = [_dot_nt(qdec[ch][:, hsl[h]], st[h].astype(BF16)) for h in heads]
        for h in heads:
            o = inter[h] + intra[ch][h]
            st[h] = st[h] * dec[ch][:, hsl[h]] + kv[ch][h]
            on = o * lax.rsqrt(jnp.mean(o * o, axis=-1, keepdims=True) + EPS) * ng_ref[...]
            o_ref[0, sl, vsl[h]] = on * rs_ref[0, sl, vsl[h]]
    for h in heads:
        st_ref[h] = st[h]


def _gla(q, k, la, v, rs, norm_g):
    tg = GLA_TG
    c = GLA_CHUNK
    r = jnp.arange(c)
    tri = (r[None, :] <= r[:, None])
    blk = tri & ((r[None, :] // GLA_SUB) == (r[:, None] // GLA_SUB))
    tt = jnp.concatenate([tri, blk], axis=0).astype(BF16)
    row3 = lambda w: pl.BlockSpec((1, tg, w), lambda b, i: (b, i, 0))
    const = lambda shape: pl.BlockSpec(shape, lambda b, i: tuple(0 for _ in shape))
    return pl.pallas_call(
        _gla_kernel,
        grid=(BATCH, SEQ // tg),
        in_specs=[row3(GLA_KEY), row3(GLA_KEY), row3(GLA_KEY), row3(GLA_VAL), row3(GLA_VAL),
                  const((1, GLA_DV)), const((2 * c, c))],
        out_specs=row3(GLA_VAL),
        out_shape=jax.ShapeDtypeStruct((BATCH, SEQ, GLA_VAL), F32),
        scratch_shapes=[pltpu.VMEM((GLA_HEADS, GLA_DV, GLA_DK), F32)],
        compiler_params=_params(("parallel", "arbitrary")),
        name="gla",
    )(q, k, la, v, rs, norm_g.reshape(1, GLA_DV), tt)


def _merge_kernel(ya_ref, yb_ref, ga_ref, gb_ref, x_ref, g1_ref, sc2_ref, sh2_ref, g2_ref, n2_ref,
                  wglu_ref, bglu_ref, wpa_ref, wpb_ref, wout_ref, wr_ref, wsgu_ref, wsd_ref,
                  base_ref, hp_ref, sc_ref):
    rows = MERGE_TM // MERGE_SPLIT
    subs = [slice(s * rows, (s + 1) * rows) for s in range(MERGE_SPLIT)]
    ya = [ya_ref[sl, :] for sl in subs]
    z = [_dot(a.astype(BF16), wglu_ref[...]) for a in ya]
    pb = [_dot(yb_ref[0, sl, :].astype(BF16), wpb_ref[...]) for sl in subs]
    ya2 = [(a * _sigmoid(zz + bglu_ref[...])).astype(BF16) for a, zz in zip(ya, z)]
    pa = [_dot(a, wpa_ref[...]) for a in ya2]
    mixed = [(ga_ref[0, sl, :] * p + gb_ref[0, sl, :] * q).astype(BF16)
             for sl, p, q in zip(subs, pa, pb)]
    mo = [_dot(m, wout_ref[...]) for m in mixed]
    x1 = [x_ref[0, sl, :] + g1_ref[0] * m for sl, m in zip(subs, mo)]
    h = [(x * lax.rsqrt(jnp.mean(x * x, axis=-1, keepdims=True) + EPS)) * n2_ref[...]
         * (1.0 + sc2_ref[0]) + sh2_ref[0] for x in x1]
    hb = [v.astype(BF16) for v in h]
    logits = [_dot(v, wr_ref[...]) for v in hb]
    gu = [_dot(v, wsgu_ref[...]) for v in hb]
    mid = [(g[:, :EXPERT_FF] * _sigmoid(g[:, :EXPERT_FF]) * g[:, EXPERT_FF:]).astype(BF16) for g in gu]
    shared = [_dot(m, wsd_ref[...]) for m in mid]
    for s, sl in enumerate(subs):
        sc_ref[sl, :] = _sigmoid(logits[s])
        base_ref[sl] = (x1[s] + g2_ref[0] * shared[s]).reshape(rows, *TOKEN_TILE)
        hp_ref[sl] = h[s].reshape(rows, *TOKEN_TILE).astype(BF16)


def _merge(ya_tb, yb, ga, gb, x, g1, sc2, sh2, g2, n2, wglu, bglu, wpa, wpb, wout, wr, wsgu, wsd):
    tm = MERGE_TM
    row3 = lambda w: pl.BlockSpec((1, tm, w), lambda b, i: (b, i, 0))
    const = lambda a: pl.BlockSpec(a.shape, lambda b, i: tuple(0 for _ in a.shape))
    mod = pl.BlockSpec((1, 1, D_MODEL), lambda b, i: (b, 0, 0))
    nt = SEQ // tm
    flat = lambda w: pl.BlockSpec((tm, w), lambda b, i: (b * nt + i, 0))
    tiles = pl.BlockSpec((tm,) + TOKEN_TILE, lambda b, i: (b * nt + i, 0, 0))
    return pl.pallas_call(
        _merge_kernel,
        grid=(BATCH, nt),
        in_specs=[pl.BlockSpec((tm, S5_WIDTH), lambda b, i: (i, b)),
                  row3(GLA_VAL), row3(D_MODEL), row3(D_MODEL), row3(D_MODEL),
                  mod, mod, mod, mod, const(n2),
                  const(wglu), const(bglu), const(wpa), const(wpb), const(wout), const(wr),
                  const(wsgu), const(wsd)],
        out_specs=[tiles, tiles, flat(N_EXPERTS)],
        out_shape=[jax.ShapeDtypeStruct((TOKENS,) + TOKEN_TILE, F32),
                   jax.ShapeDtypeStruct((TOKENS,) + TOKEN_TILE, BF16),
                   jax.ShapeDtypeStruct((TOKENS, N_EXPERTS), F32)],
        compiler_params=_params(("parallel", "parallel")),
        name="merge",
    )(ya_tb, yb, ga, gb, x, g1, sc2, sh2, g2, n2, wglu, bglu, wpa, wpb, wout, wr, wsgu, wsd)


def _first_argmax(x, iota, size):
    m = jnp.max(x, axis=0, keepdims=True)
    idx = jnp.min(jnp.where(x == m, iota, size), axis=0, keepdims=True)
    return m, idx


def _route_kernel(s_ref, bias_ref, tri_ref, e_ref, r_ref, w_ref, cnt_ref, carry_ref):
    @pl.when(pl.program_id(0) == 0)
    def _():
        carry_ref[...] = jnp.zeros_like(carry_ref)

    tr = ROUTE_TR
    neg = -jnp.inf
    s_t = s_ref[...].T
    biased = s_t + bias_ref[...]
    io_g = lax.broadcasted_iota(I32, (GROUP_SIZE, tr), 0)
    rows = []
    for g in range(N_GROUPS):
        xg = biased[g * GROUP_SIZE:(g + 1) * GROUP_SIZE, :]
        m1, i1 = _first_argmax(xg, io_g, GROUP_SIZE)
        m2 = jnp.max(jnp.where(io_g == i1, neg, xg), axis=0, keepdims=True)
        rows.append(m1 + m2)
    gs = jnp.concatenate(rows, axis=0)
    io_n = lax.broadcasted_iota(I32, (N_GROUPS, tr), 0)
    gsel = jnp.zeros((N_GROUPS, tr), F32)
    for _ in range(TOPK_GROUPS):
        _, gi = _first_argmax(gs, io_n, N_GROUPS)
        hit = io_n == gi
        gsel = jnp.where(hit, 1.0, gsel)
        gs = jnp.where(hit, neg, gs)
    masked = jnp.concatenate(
        [jnp.where(gsel[g:g + 1, :] > 0.0, biased[g * GROUP_SIZE:(g + 1) * GROUP_SIZE, :], neg)
         for g in range(N_GROUPS)], axis=0)
    io_e = lax.broadcasted_iota(I32, (N_EXPERTS, tr), 0)
    sel = jnp.zeros((N_EXPERTS, tr), F32)
    idxs = []
    for _ in range(TOP_K):
        _, ei = _first_argmax(masked, io_e, N_EXPERTS)
        hit = io_e == ei
        sel = jnp.where(hit, 1.0, sel)
        masked = jnp.where(hit, neg, masked)
        idxs.append(ei)
    den = jnp.sum(sel * s_t, axis=0, keepdims=True)
    rank = _dot(sel.astype(BF16), tri_ref[...]) + carry_ref[...]
    rks, sks = [], []
    for kk in range(TOP_K):
        hit = io_e == idxs[kk]
        rks.append(jnp.sum(jnp.where(hit, rank, 0.0), axis=0, keepdims=True))
        sks.append(jnp.sum(jnp.where(hit, s_t, 0.0), axis=0, keepdims=True))
    e_ref[...] = jnp.concatenate(idxs, axis=0)
    r_ref[...] = jnp.concatenate(rks, axis=0).astype(I32)
    w_ref[...] = jnp.concatenate(sks, axis=0) / den * ROUTE_SCALE
    carry_ref[...] += jnp.sum(sel, axis=1, keepdims=True)
    cnt_ref[...] = carry_ref[...]


def _route(scores, bias):
    tr = ROUTE_TR
    r = jnp.arange(tr)
    tri = (r[:, None] < r[None, :]).astype(BF16)
    kt = lambda dt: jax.ShapeDtypeStruct((TOP_K, TOKENS), dt)
    blk = pl.BlockSpec((TOP_K, tr), lambda i: (0, i))
    return pl.pallas_call(
        _route_kernel,
        grid=(TOKENS // tr,),
        in_specs=[pl.BlockSpec((tr, N_EXPERTS), lambda i: (i, 0)),
                  pl.BlockSpec((N_EXPERTS, 1), lambda i: (0, 0)),
                  pl.BlockSpec((tr, tr), lambda i: (0, 0))],
        out_specs=[blk, blk, blk, pl.BlockSpec((N_EXPERTS, 1), lambda i: (0, 0))],
        out_shape=[kt(I32), kt(I32), kt(F32), jax.ShapeDtypeStruct((N_EXPERTS, 1), F32)],
        scratch_shapes=[pltpu.VMEM((N_EXPERTS, 1), F32)],
        compiler_params=_params(("arbitrary",)),
        name="route",
    )(scores, bias.reshape(N_EXPERTS, 1), tri)


def _slots_kernel(e_ref, r_ref, ps_ref, d_ref):
    tr = ROUTE_TR
    io_e = lax.broadcasted_iota(I32, (N_EXPERTS, tr), 0)
    ps = ps_ref[...]
    rows = []
    for kk in range(TOP_K):
        hit = io_e == e_ref[kk:kk + 1, :]
        rows.append(jnp.sum(jnp.where(hit, ps, 0.0), axis=0, keepdims=True))
    d_ref[...] = jnp.concatenate(rows, axis=0).astype(I32) + r_ref[...]


def _slots(e_idx, rank, pad_start):
    tr = ROUTE_TR
    blk = pl.BlockSpec((TOP_K, tr), lambda i: (0, i))
    return pl.pallas_call(
        _slots_kernel,
        grid=(TOKENS // tr,),
        in_specs=[blk, blk, pl.BlockSpec((N_EXPERTS, 1), lambda i: (0, 0))],
        out_specs=blk,
        out_shape=jax.ShapeDtypeStruct((TOP_K, TOKENS), I32),
        compiler_params=_params(("parallel",)),
        name="slots",
    )(e_idx, rank, pad_start.astype(F32).reshape(N_EXPERTS, 1))


def _dispatch_kernel(ps_ref, cnt_ref, nb_ref, dest_ref, h_ref, xs_ref, zbuf, sem, zsem):
    tq = DISPATCH_TQ
    step = pl.program_id(0)
    n_used = ps_ref[N_EXPERTS - 1] // MOE_BLOCK + nb_ref[N_EXPERTS - 1]

    def zero_fill(act):
        def block(j, carry):
            r = pl.ds(pl.multiple_of(j * MOE_BLOCK, MOE_BLOCK), MOE_BLOCK)
            act(pltpu.make_async_copy(zbuf, xs_ref.at[r], zsem))
            return carry

        def expert(e, carry):
            @pl.when(cnt_ref[e] < nb_ref[e] * MOE_BLOCK)
            def _():
                block(ps_ref[e] // MOE_BLOCK + nb_ref[e] - 1, 0)
            return carry

        lax.fori_loop(0, N_EXPERTS, expert, 0)
        lax.fori_loop(n_used, N_BLOCKS, block, 0)

    @pl.when(step == 0)
    def _():
        zbuf[...] = jnp.zeros_like(zbuf)
        zero_fill(lambda cp: cp.start())
        zero_fill(lambda cp: cp.wait())

    def start(i, carry):
        for kk in range(TOP_K):
            d = dest_ref[i * TOP_K + kk]
            pltpu.make_async_copy(h_ref.at[i], xs_ref.at[d],
                                  sem).start(priority=kk % 2)
        return carry

    lax.fori_loop(0, tq, start, 0, unroll=2)
    for _ in range(TOP_K):
        pltpu.make_async_copy(h_ref, xs_ref.at[pl.ds(0, tq)], sem).wait()


def _dispatch(pad_start, counts, n_blk, dest, hp):
    tq = DISPATCH_TQ
    grid_spec = pltpu.PrefetchScalarGridSpec(
        num_scalar_prefetch=3,
        grid=(TOKENS // tq,),
        in_specs=[pl.BlockSpec((tq * TOP_K,), lambda i, *_: (i,), memory_space=pltpu.SMEM),
                  pl.BlockSpec((tq,) + TOKEN_TILE, lambda i, *_: (i, 0, 0))],
        out_specs=pl.BlockSpec(memory_space=pl.ANY),
        scratch_shapes=[pltpu.VMEM((MOE_BLOCK,) + TOKEN_TILE, hp.dtype),
                        pltpu.SemaphoreType.DMA, pltpu.SemaphoreType.DMA],
    )
    return pl.pallas_call(
        _dispatch_kernel,
        grid_spec=grid_spec,
        out_shape=jax.ShapeDtypeStruct((N_SLOTS,) + TOKEN_TILE, hp.dtype),
        compiler_params=_params(("arbitrary",)),
        name="dispatch",
    )(pad_start, counts, n_blk, dest, hp)


def _expert_kernel(ps_ref, nb_ref, wg_ref, wu_ref, wd_ref, xs_ref, ys_ref,
                   xbuf, ybuf, wgus, wds, xsem, ysem):
    e = pl.program_id(0)
    nb = nb_ref[e]
    first = ps_ref[e] // MOE_BLOCK
    n_used = ps_ref[N_EXPERTS - 1] // MOE_BLOCK + nb_ref[N_EXPERTS - 1]
    nbuf = EXPERT_NBUF

    def rows(g):
        return pl.ds(pl.multiple_of(g * MOE_BLOCK, MOE_BLOCK), MOE_BLOCK)

    def x_copy(g):
        slot = g % nbuf
        return pltpu.make_async_copy(xs_ref.at[rows(g)], xbuf.at[slot], xsem.at[slot])

    def y_copy(g):
        slot = g % nbuf
        return pltpu.make_async_copy(ybuf.at[slot], ys_ref.at[rows(g)], ysem.at[slot])

    look = nbuf - EXPERT_UNIT

    @pl.when(e == 0)
    def _():
        for g in range(look):
            @pl.when(g < n_used)
            def _():
                x_copy(g).start()

    def run(g, u):
        for d in range(u):
            @pl.when(g + look + d < n_used)
            def _():
                x_copy(g + look + d).start()

        for d in range(u):
            b = g + d

            @pl.when(b >= nbuf)
            def _():
                y_copy(b - nbuf).wait()

        for d in range(u):
            x_copy(g + d).wait()
        gus = [_dot(xbuf[(g + d) % nbuf].astype(F32).reshape(MOE_BLOCK, D_MODEL).astype(BF16), wgus[...])
               for d in range(u)]
        for d in range(u):
            gate = gus[d][:, :EXPERT_FF]
            mid = (gate * _sigmoid(gate) * gus[d][:, EXPERT_FF:]).astype(BF16)
            ybuf[(g + d) % nbuf] = (_dot(mid, wds[...]).reshape(MOE_BLOCK, *TOKEN_TILE)
                                    .astype(EXPERT_OUT_DTYPE))
        for d in range(u):
            y_copy(g + d).start()

    @pl.when(nb > 0)
    def _():
        wgus[:, :EXPERT_FF] = wg_ref[0].astype(BF16)
        wgus[:, EXPERT_FF:] = wu_ref[0].astype(BF16)
        wds[...] = wd_ref[0].astype(BF16)

        whole = nb == EXPERT_UNIT
        triple = (nb % 2 == 1) & (nb >= 3)
        pairs = jnp.where(whole, 0, jnp.where(triple, (nb - 3) // 2, nb // 2))

        def pair(j, carry):
            run(first + j * 2, 2)
            return carry

        lax.fori_loop(0, pairs, pair, 0)

        @pl.when(whole)
        def _():
            run(first, EXPERT_UNIT)

        @pl.when(triple)
        def _():
            run(first + nb - 3, 3)

        @pl.when(nb == 1)
        def _():
            run(first, 1)

    @pl.when(e == N_EXPERTS - 1)
    def _():
        for d in range(nbuf):
            g = n_used - nbuf + d

            @pl.when(g >= 0)
            def _():
                y_copy(g).wait()

        ybuf[0] = jnp.zeros((MOE_BLOCK,) + TOKEN_TILE, EXPERT_OUT_DTYPE)

        def z_copy(j):
            return pltpu.make_async_copy(ybuf.at[0], ys_ref.at[rows(j)], ysem.at[0])

        def z_start(j, carry):
            z_copy(j).start()
            return carry

        def z_wait(j, carry):
            z_copy(j).wait()
            return carry

        lax.fori_loop(n_used, N_BLOCKS, z_start, 0)
        lax.fori_loop(n_used, N_BLOCKS, z_wait, 0)


def _experts(pad_start, n_blk, xs, wg, wu, wd):
    wspec = lambda shape: pl.BlockSpec((1,) + shape, lambda e, ps, nb: (e, 0, 0))
    grid_spec = pltpu.PrefetchScalarGridSpec(
        num_scalar_prefetch=2,
        grid=(N_EXPERTS,),
        in_specs=[wspec((D_MODEL, EXPERT_FF)), wspec((D_MODEL, EXPERT_FF)),
                  wspec((EXPERT_FF, D_MODEL)), pl.BlockSpec(memory_space=pl.ANY)],
        out_specs=pl.BlockSpec(memory_space=pl.ANY),
        scratch_shapes=[pltpu.VMEM((EXPERT_NBUF, MOE_BLOCK) + TOKEN_TILE, xs.dtype),
                        pltpu.VMEM((EXPERT_NBUF, MOE_BLOCK) + TOKEN_TILE, EXPERT_OUT_DTYPE),
                        pltpu.VMEM((D_MODEL, 2 * EXPERT_FF), BF16),
                        pltpu.VMEM((EXPERT_FF, D_MODEL), BF16),
                        pltpu.SemaphoreType.DMA((EXPERT_NBUF,)),
                        pltpu.SemaphoreType.DMA((EXPERT_NBUF,))],
    )
    return pl.pallas_call(
        _expert_kernel,
        grid_spec=grid_spec,
        out_shape=jax.ShapeDtypeStruct((N_SLOTS,) + TOKEN_TILE, EXPERT_OUT_DTYPE),
        compiler_params=_params(("arbitrary",)),
        name="experts",
    )(pad_start, n_blk, wg, wu, wd, xs)


def _combine_kernel(dcur_ref, dnext_ref, w_ref, base_ref, g2_ref, fg_ref, ys_ref, o_ref,
                    buf, acc_ref, sem):
    tq = COMBINE_TQ
    i = pl.program_id(0)
    n = pl.num_programs(0)

    def start(dest_ref, to_slot, t):
        for kk in range(TOP_K):
            d = dest_ref[t * TOP_K + kk]
            pltpu.make_async_copy(ys_ref.at[d], buf.at[to_slot, kk, t],
                                  sem.at[to_slot]).start(priority=kk % 2)

    def reduce(slot, t):
        a = buf[slot, 0, t].astype(F32) * w_ref[t * TOP_K]
        for kk in range(1, TOP_K):
            a = a + buf[slot, kk, t].astype(F32) * w_ref[t * TOP_K + kk]
        acc_ref[t] = a

    @pl.when(i == 0)
    def _():
        lax.fori_loop(0, tq, lambda t, c: start(dcur_ref, 0, t) or c, 0, unroll=4)

    for slot in range(2):
        @pl.when(i % 2 == slot)
        def _(slot=slot):
            @pl.when(i + 1 < n)
            def _():
                lax.fori_loop(0, tq, lambda t, c: start(dnext_ref, 1 - slot, t) or c, 0, unroll=4)

            for kk in range(TOP_K):
                pltpu.make_async_copy(ys_ref.at[pl.ds(0, tq)], buf.at[slot, kk],
                                      sem.at[slot]).wait()
            lax.fori_loop(0, tq, lambda t, c: reduce(slot, t) or c, 0, unroll=8)

    xo = base_ref[...] + g2_ref[0] * acc_ref[...]
    ms = jnp.sum(jnp.sum(xo * xo, axis=2, keepdims=True), axis=1, keepdims=True) * (1.0 / D_MODEL)
    o_ref[0] = ((xo * lax.rsqrt(ms + EPS)) * fg_ref[...]).reshape(tq, D_MODEL)


def _combine(dest, w_flat, base, g2, final_g, ys):
    tq = COMBINE_TQ
    nt = SEQ // tq
    n = TOKENS // tq
    smem = lambda imap: pl.BlockSpec((tq * TOP_K,), imap, memory_space=pltpu.SMEM)
    return pl.pallas_call(
        _combine_kernel,
        grid=(n,),
        in_specs=[smem(lambda i: (i,)), smem(lambda i: (jnp.minimum(i + 1, n - 1),)),
                  smem(lambda i: (i,)),
                  pl.BlockSpec((tq,) + TOKEN_TILE, lambda i: (i, 0, 0)),
                  pl.BlockSpec((1,) + TOKEN_TILE, lambda i: (i // nt, 0, 0)),
                  pl.BlockSpec(TOKEN_TILE, lambda i: (0, 0)),
                  pl.BlockSpec(memory_space=pl.ANY)],
        out_specs=pl.BlockSpec((1, tq, D_MODEL), lambda i: (i // nt, i % nt, 0)),
        out_shape=jax.ShapeDtypeStruct((BATCH, SEQ, D_MODEL), F32),
        scratch_shapes=[pltpu.VMEM((2, TOP_K, tq) + TOKEN_TILE, ys.dtype),
                        pltpu.VMEM((tq,) + TOKEN_TILE, F32),
                        pltpu.SemaphoreType.DMA((2,))],
        compiler_params=_params(("arbitrary",)),
        name="combine",
    )(dest, dest, w_flat, base, g2.reshape((BATCH,) + TOKEN_TILE), final_g.reshape(TOKEN_TILE), ys)


def kernel(x, c, ada_w, ada_b, norm1_g, w_in, s5_lam_re, s5_lam_im, s5_log_dt, s5_b_re, s5_b_im,
           s5_c_re, s5_c_im, s5_d, s5_w_glu, s5_b_glu, w_proj_a, gla_w_gk2, gla_b_gk2, gla_norm_g,
           w_proj_b, w_out, norm2_g, router_w, router_bias, exp_w_gate, exp_w_up, exp_w_down,
           sh_w_gate, sh_w_up, sh_w_down, final_g):
    l = 0
    mod = _ada(c, ada_w[l], ada_b[l])
    sh1, sc1, g1, sh2, sc2, g2 = [m.reshape(BATCH, 1, D_MODEL) for m in jnp.split(mod, 6, axis=-1)]

    w = w_in[l]
    gk0 = _V0 + GLA_VAL
    w_main = jnp.concatenate([w[:, :gk0], w[:, gk0 + GLA_GATE_RANK:]], axis=1).astype(BF16)
    w_gk = jnp.pad(w[:, gk0:gk0 + GLA_GATE_RANK], ((0, 0), (0, LANES - GLA_GATE_RANK))).astype(BF16)
    w_gk2 = jnp.pad(gla_w_gk2[l], ((0, LANES - GLA_GATE_RANK), (0, 0))).astype(BF16)
    u_tb, q, k, v, la, rs, ga, gb = _inproj(
        x, norm1_g[l].reshape(1, D_MODEL), sc1, sh1, w_main, w_gk, w_gk2,
        gla_b_gk2[l].reshape(1, GLA_KEY))
    tables = _s5_tables(s5_lam_re[l], s5_lam_im[l], s5_log_dt[l], s5_b_re[l], s5_b_im[l],
                        s5_c_re[l], s5_c_im[l], s5_d[l])
    ya_tb = _s5(u_tb, *tables)
    yb = _gla(q, k, la, v, rs, gla_norm_g[l])

    wsgu = jnp.concatenate([sh_w_gate[l], sh_w_up[l]], axis=1).astype(BF16)
    base, hp, scores = _merge(
        ya_tb, yb, ga, gb, x, g1, sc2, sh2, g2,
        norm2_g[l].reshape(1, D_MODEL), s5_w_glu[l].astype(BF16), s5_b_glu[l].reshape(1, S5_WIDTH),
        w_proj_a[l].astype(BF16), w_proj_b[l].astype(BF16), w_out[l].astype(BF16),
        router_w[l].astype(BF16), wsgu, sh_w_down[l].astype(BF16))

    e_idx, rank, w_k, cnt = _route(scores, router_bias[l])
    counts = cnt[:, 0].astype(I32)
    padded = (counts + MOE_BLOCK - 1) // MOE_BLOCK * MOE_BLOCK
    pad_end = jnp.cumsum(padded)
    pad_start = pad_end - padded
    dest = _slots(e_idx, rank, pad_start).T.reshape(-1)
    pad_start = pad_start.astype(I32)
    n_blk = (padded // MOE_BLOCK).astype(I32)
    xs = _dispatch(pad_start, counts, n_blk, dest, hp)
    ys = _experts(pad_start, n_blk, xs, exp_w_gate[l], exp_w_up[l], exp_w_down[l])
    return _combine(dest, w_k.T.reshape(-1), base, g2, final_g, ys)
```

```python
import jax
import jax.numpy as jnp
from jax import lax
from jax.experimental import pallas as pl
from jax.experimental.pallas import tpu as pltpu

F32 = jnp.float32
BF16 = jnp.bfloat16
I32 = jnp.int32

D_MODEL = 1024
BATCH = 8
SEQ = 2048
TOKENS = BATCH * SEQ
S5_WIDTH = 512
S5_GROUP = 16
S5_GROUPS = 32
S5_STATE = 64
S5_COLS = S5_GROUPS * S5_STATE
GLA_HEADS = 4
GLA_DK = 64
GLA_DV = 128
GLA_KEY = GLA_HEADS * GLA_DK
GLA_VAL = GLA_HEADS * GLA_DV
GLA_GATE_RANK = 16
GLA_GATE_TAU = 16.0
GLA_CHUNK = 64
GLA_SUB = 16
N_EXPERTS = 256
TOP_K = 8
N_GROUPS = 8
GROUP_SIZE = N_EXPERTS // N_GROUPS
TOPK_GROUPS = 4
EXPERT_FF = 256
ROUTE_SCALE = 2.5
MOE_BLOCK = 128
EPS = 1e-6
N_SLOTS = -(-(TOKENS * TOP_K + N_EXPERTS * (MOE_BLOCK - 1)) // MOE_BLOCK) * MOE_BLOCK
N_BLOCKS = N_SLOTS // MOE_BLOCK

LANES = 128
SUBLANES = 8
TOKEN_TILE = (SUBLANES, LANES)
assert D_MODEL == SUBLANES * LANES
VMEM_LIMIT = 56 * 1024 * 1024

ADA_TN = 1536
INPROJ_TM = 512
S5_TC = 128
S5_RB = 32
S5_COL_CHUNK = 512
GLA_TG = 512
MERGE_TM = 512
MERGE_SPLIT = 2
ROUTE_TR = 512
DISPATCH_TQ = 1024
COMBINE_TQ = 256
EXPERT_UNIT = 4
EXPERT_NBUF = 10
EXPERT_OUT_DTYPE = BF16
EXP_CLAMP = 60.0


def _params(sem, vmem=VMEM_LIMIT):
    return pltpu.CompilerParams(dimension_semantics=sem, vmem_limit_bytes=vmem)


def _dot(a, b):
    return jnp.dot(a, b, preferred_element_type=F32)


def _dot_nt(a, b):
    return lax.dot_general(a, b, (((1,), (1,)), ((), ())), preferred_element_type=F32)


def _sigmoid(x):
    return jax.nn.sigmoid(x)


def _ada_kernel(c_ref, w_ref, b_ref, o_ref):
    c = c_ref[...]
    s = (c * _sigmoid(c)).astype(BF16)
    o_ref[...] = _dot(s, w_ref[...].astype(BF16)) + b_ref[...]


def _ada(c, w, b):
    n = w.shape[1]
    return pl.pallas_call(
        _ada_kernel,
        grid=(n // ADA_TN,),
        in_specs=[
            pl.BlockSpec((BATCH, D_MODEL), lambda j: (0, 0)),
            pl.BlockSpec((D_MODEL, ADA_TN), lambda j: (0, j)),
            pl.BlockSpec((1, ADA_TN), lambda j: (0, j)),
        ],
        out_specs=pl.BlockSpec((BATCH, ADA_TN), lambda j: (0, j)),
        out_shape=jax.ShapeDtypeStruct((BATCH, n), F32),
        compiler_params=_params(("arbitrary",)),
        name="ada",
    )(c, w, b.reshape(1, n))


_U0, _Q0, _K0, _V0, _R0, _GA0, _GB0, _END = 0, 512, 768, 1024, 1536, 2048, 3072, 4096


def _inproj_kernel(x_ref, g_ref, sc_ref, sh_ref, wm_ref, wgk_ref, wgk2_ref, bgk_ref,
                   u_ref, q_ref, k_ref, v_ref, la_ref, rs_ref, ga_ref, gb_ref):
    x = x_ref[0]
    r = lax.rsqrt(jnp.mean(x * x, axis=-1, keepdims=True) + EPS)
    h = (x * r) * g_ref[...] * (1.0 + sc_ref[0]) + sh_ref[0]
    hb = h.astype(BF16)

    def seg(lo, hi):
        return _dot(hb, wm_ref[:, lo:hi])

    u_ref[...] = seg(_U0, _Q0)
    q_ref[0] = seg(_Q0, _K0) * (GLA_DK ** -0.5)
    k_ref[0] = seg(_K0, _V0)
    v_ref[0] = seg(_V0, _R0)
    rr = seg(_R0, _GA0)
    rs_ref[0] = rr * _sigmoid(rr)
    ga_ref[0] = _sigmoid(seg(_GA0, _GB0))
    gb_ref[0] = _sigmoid(seg(_GB0, _END))
    gk = _dot(hb, wgk_ref[...])
    z = _dot(gk.astype(BF16), wgk2_ref[...]) + bgk_ref[...]
    la_ref[0] = -(jnp.maximum(-z, 0.0) + jnp.log1p(jnp.exp(-jnp.abs(z)))) * (1.0 / GLA_GATE_TAU)


def _inproj(x, g, sc, sh, w_main, w_gk, w_gk2, b_gk2):
    tm = INPROJ_TM
    nt = SEQ // tm
    row3 = lambda w: pl.BlockSpec((1, tm, w), lambda b, i: (b, i, 0))
    const = lambda shape: pl.BlockSpec(shape, lambda b, i: tuple(0 for _ in shape))
    mod = pl.BlockSpec((1, 1, D_MODEL), lambda b, i: (b, 0, 0))
    bld = lambda w: jax.ShapeDtypeStruct((BATCH, SEQ, w), F32)
    return pl.pallas_call(
        _inproj_kernel,
        grid=(BATCH, nt),
        in_specs=[row3(D_MODEL), const((1, D_MODEL)), mod, mod,
                  const((D_MODEL, _END)), const((D_MODEL, LANES)), const((LANES, GLA_KEY)),
                  const((1, GLA_KEY))],
        out_specs=[pl.BlockSpec((tm, S5_WIDTH), lambda b, i: (i, b)),
                   row3(GLA_KEY), row3(GLA_KEY), row3(GLA_VAL), row3(GLA_KEY), row3(GLA_VAL),
                   row3(D_MODEL), row3(D_MODEL)],
        out_shape=[jax.ShapeDtypeStruct((SEQ, BATCH * S5_WIDTH), F32),
                   bld(GLA_KEY), bld(GLA_KEY), bld(GLA_VAL), bld(GLA_KEY), bld(GLA_VAL),
                   bld(D_MODEL), bld(D_MODEL)],
        compiler_params=_params(("parallel", "parallel")),
        name="inproj",
    )(x, g, sc, sh, w_main, w_gk, w_gk2, b_gk2)


def _s5_kernel(u_ref, bre_ref, bim_ref, cre_ref, cim_ref, are_ref, aim_ref, d_ref,
               y_ref, s_ref, st_ref):
    half = S5_COLS // 2

    @pl.when(pl.program_id(0) == 0)
    def _():
        st_ref[...] = jnp.zeros_like(st_ref)

    u = u_ref[...].reshape(S5_TC, BATCH, S5_WIDTH).reshape(S5_TC * BATCH, S5_WIDTH)
    ub = u.astype(BF16)
    nblk = S5_TC // S5_RB
    brows = [slice(r * S5_RB * BATCH, (r + 1) * S5_RB * BATCH) for r in range(nblk)]
    for r in range(nblk):
        for j in range(2):
            uj = ub[brows[r], j * 256:(j + 1) * 256]
            s_ref[brows[r], j * half:(j + 1) * half] = _dot(uj, bre_ref[j])
            s_ref[brows[r], S5_COLS + j * half:S5_COLS + (j + 1) * half] = _dot(uj, bim_ref[j])

    ncc = S5_COLS // S5_COL_CHUNK
    re_sl = [slice(cc * S5_COL_CHUNK, (cc + 1) * S5_COL_CHUNK) for cc in range(ncc)]
    im_sl = [slice(S5_COLS + cc * S5_COL_CHUNK, S5_COLS + (cc + 1) * S5_COL_CHUNK) for cc in range(ncc)]
    sr = [st_ref[:, sl] for sl in re_sl]
    si = [st_ref[:, sl] for sl in im_sl]
    for r in range(nblk):
        for t in range(r * S5_RB, (r + 1) * S5_RB):
            rows = slice(t * BATCH, (t + 1) * BATCH)
            for cc in range(ncc):
                ar = are_ref[:, re_sl[cc]]
                ai = aim_ref[:, re_sl[cc]]
                nr = ar * sr[cc] - ai * si[cc] + s_ref[rows, re_sl[cc]]
                ni = ar * si[cc] + ai * sr[cc] + s_ref[rows, im_sl[cc]]
                s_ref[rows, re_sl[cc]] = nr
                s_ref[rows, im_sl[cc]] = ni
                sr[cc], si[cc] = nr, ni
    for cc in range(ncc):
        st_ref[:, re_sl[cc]] = sr[cc]
        st_ref[:, im_sl[cc]] = si[cc]

    for r in range(nblk):
        ys = []
        for j in range(2):
            sre = s_ref[brows[r], j * half:(j + 1) * half].astype(BF16)
            sim = s_ref[brows[r], S5_COLS + j * half:S5_COLS + (j + 1) * half].astype(BF16)
            ys.append(_dot(sre, cre_ref[j]) + _dot(sim, cim_ref[j]))
        y = jnp.concatenate(ys, axis=1) + d_ref[...] * u[brows[r], :]
        y_ref[r * S5_RB:(r + 1) * S5_RB, :] = (
            jax.nn.gelu(y).reshape(S5_RB, BATCH, S5_WIDTH).reshape(S5_RB, BATCH * S5_WIDTH))


def _s5(u_tb, bre, bim, cre, cimn, are, aim, dflat):
    rows = S5_TC * BATCH
    const = lambda shape: pl.BlockSpec(shape, lambda i: tuple(0 for _ in shape))
    return pl.pallas_call(
        _s5_kernel,
        grid=(SEQ // S5_TC,),
        in_specs=[pl.BlockSpec((S5_TC, BATCH * S5_WIDTH), lambda i: (i, 0)),
                  const(bre.shape), const(bim.shape), const(cre.shape), const(cimn.shape),
                  const(are.shape), const(aim.shape), const(dflat.shape)],
        out_specs=pl.BlockSpec((S5_TC, BATCH * S5_WIDTH), lambda i: (i, 0)),
        out_shape=jax.ShapeDtypeStruct((SEQ, BATCH * S5_WIDTH), F32),
        scratch_shapes=[pltpu.VMEM((rows, 2 * S5_COLS), F32),
                        pltpu.VMEM((BATCH, 2 * S5_COLS), F32)],
        compiler_params=_params(("arbitrary",)),
        name="s5",
    )(u_tb, bre, bim, cre, cimn, are, aim, dflat)


def _s5_tables(lam_re, lam_im, log_dt, b_re, b_im, c_re, c_im, d_skip):
    lr = lam_re.astype(F32)
    li = lam_im.astype(F32)
    dt = jnp.exp(log_dt.astype(F32))[:, None]
    mag = jnp.exp(lr * dt)
    abar_re = mag * jnp.cos(li * dt)
    abar_im = mag * jnp.sin(li * dt)
    den = lr * lr + li * li
    num_re = abar_re - 1.0
    coef_re = (num_re * lr + abar_im * li) / den
    coef_im = (abar_im * lr - num_re * li) / den
    cr, ci = coef_re[..., None], coef_im[..., None]
    br, bi = b_re.astype(F32), b_im.astype(F32)
    bbar_re = cr * br - ci * bi
    bbar_im = cr * bi + ci * br
    eye = jnp.eye(S5_GROUPS // 2, dtype=F32)

    def in_map(bb):
        bb = bb.reshape(2, S5_GROUPS // 2, S5_STATE, S5_GROUP)
        return jnp.einsum('jgph,gk->jghkp', bb, eye).reshape(2, 256, S5_COLS // 2).astype(BF16)

    def out_map(cc):
        cc = cc.reshape(2, S5_GROUPS // 2, S5_GROUP, S5_STATE)
        return jnp.einsum('jghp,gk->jgpkh', cc, eye).reshape(2, S5_COLS // 2, 256).astype(BF16)

    are = jnp.broadcast_to(abar_re.reshape(1, S5_COLS), (BATCH, S5_COLS))
    aim = jnp.broadcast_to(abar_im.reshape(1, S5_COLS), (BATCH, S5_COLS))
    return (in_map(bbar_re), in_map(bbar_im), out_map(c_re.astype(F32)), out_map(-c_im.astype(F32)),
            are, aim, d_skip.astype(F32).reshape(1, S5_WIDTH))


def _gla_kernel(q_ref, k_ref, la_ref, v_ref, rs_ref, ng_ref, tt_ref, o_ref, st_ref):
    @pl.when(pl.program_id(1) == 0)
    def _():
        st_ref[...] = jnp.zeros_like(st_ref)

    c = GLA_CHUNK
    tt = tt_ref[...]
    row = lax.broadcasted_iota(I32, (c, GLA_KEY), 0)
    ri = lax.broadcasted_iota(I32, (c, c), 0)
    ci = lax.broadcasted_iota(I32, (c, c), 1)
    causal = ci <= ri
    nsub = c // GLA_SUB

    heads = range(GLA_HEADS)
    hsl = [slice(h * GLA_DK, (h + 1) * GLA_DK) for h in heads]
    vsl = [slice(h * GLA_DV, (h + 1) * GLA_DV) for h in heads]
    chunks = range(GLA_TG // c)
    intra, kv, qdec, dec = [], [], [], []

    sls = [slice(ch * c, (ch + 1) * c) for ch in chunks]
    csum = []
    for ch in chunks:
        g = la_ref[0, sls[ch], :]
        g1 = g.astype(BF16)
        r1 = g - g1.astype(F32)
        g2 = r1.astype(BF16)
        g3 = (r1 - g2.astype(F32)).astype(BF16)
        csum.append(_dot(tt, g1) + _dot(tt, g2) + _dot(tt, g3))
    qbigs, kbigs, kdec = [], [], []
    for ch in chunks:
        sl = sls[ch]
        cs = csum[ch]
        b = cs[0:c]
        cl = cs[c:2 * c]
        ref_pt = b - cl
        q = q_ref[0, sl, :]
        k = k_ref[0, sl, :]
        qt = q * jnp.exp(cl)
        qe = (q * jnp.exp(b)).astype(BF16)
        blast = b[c - 1:c, :]
        ks = (k * jnp.exp(blast - b)).astype(BF16)
        k_sub = []
        q_sub = []
        for s in range(nsub):
            rs_ = ref_pt[s * GLA_SUB:s * GLA_SUB + 1, :]
            k_sub.append(k * jnp.exp(jnp.minimum(rs_ - b, EXP_CLAMP)))
            q_sub.append(jnp.where((row >= s * GLA_SUB) & (row < (s + 1) * GLA_SUB), qt, 0.0))
        qbigs.append([jnp.concatenate([x[:, hsl[h]] for x in q_sub], axis=1).astype(BF16)
                      for h in heads])
        kbigs.append([jnp.concatenate([x[:, hsl[h]] for x in k_sub], axis=1).astype(BF16)
                      for h in heads])
        kdec.append(ks)
        qdec.append(qe)
        dec.append(jnp.exp(blast))
    scores = [[_dot_nt(qbigs[ch][h], kbigs[ch][h]) for h in heads] for ch in chunks]
    for ch in chunks:
        kv.append([_dot(v_ref[0, sls[ch], vsl[h]].T.astype(BF16), kdec[ch][:, hsl[h]])
                   for h in heads])
    for ch in chunks:
        intra.append([_dot(jnp.where(causal, scores[ch][h], 0.0).astype(BF16),
                           v_ref[0, sls[ch], vsl[h]].astype(BF16)) for h in heads])

    st = [st_ref[h] for h in heads]
    for ch in chunks:
        sl = slice(ch * c, (ch + 1) * c)
        inter = [_dot_nt(qdec[ch][:, hsl[h]], st[h].astype(BF16)) for h in heads]
        for h in heads:
            o = inter[h] + intra[ch][h]
            st[h] = st[h] * dec[ch][:, hsl[h]] + kv[ch][h]
            on = o * lax.rsqrt(jnp.mean(o * o, axis=-1, keepdims=True) + EPS) * ng_ref[...]
            o_ref[0, sl, vsl[h]] = on * rs_ref[0, sl, vsl[h]]
    for h in heads:
        st_ref[h] = st[h]


def _gla(q, k, la, v, rs, norm_g):
    tg = GLA_TG
    c = GLA_CHUNK
    r = jnp.arange(c)
    tri = (r[None, :] <= r[:, None])
    blk = tri & ((r[None, :] // GLA_SUB) == (r[:, None] // GLA_SUB))
    tt = jnp.concatenate([tri, blk], axis=0).astype(BF16)
    row3 = lambda w: pl.BlockSpec((1, tg, w), lambda b, i: (b, i, 0))
    const = lambda shape: pl.BlockSpec(shape, lambda b, i: tuple(0 for _ in shape))
    return pl.pallas_call(
        _gla_kernel,
        grid=(BATCH, SEQ // tg),
        in_specs=[row3(GLA_KEY), row3(GLA_KEY), row3(GLA_KEY), row3(GLA_VAL), row3(GLA_VAL),
                  const((1, GLA_DV)), const((2 * c, c))],
        out_specs=row3(GLA_VAL),
        out_shape=jax.ShapeDtypeStruct((BATCH, SEQ, GLA_VAL), F32),
        scratch_shapes=[pltpu.VMEM((GLA_HEADS, GLA_DV, GLA_DK), F32)],
        compiler_params=_params(("parallel", "arbitrary")),
        name="gla",
    )(q, k, la, v, rs, norm_g.reshape(1, GLA_DV), tt)


def _merge_kernel(ya_ref, yb_ref, ga_ref, gb_ref, x_ref, g1_ref, sc2_ref, sh2_ref, g2_ref, n2_ref,
                  wglu_ref, bglu_ref, wpa_ref, wpb_ref, wout_ref, wr_ref, wsgu_ref, wsd_ref,
                  base_ref, hp_ref, sc_ref):
    rows = MERGE_TM // MERGE_SPLIT
    subs = [slice(s * rows, (s + 1) * rows) for s in range(MERGE_SPLIT)]
    ya = [ya_ref[sl, :] for sl in subs]
    z = [_dot(a.astype(BF16), wglu_ref[...]) for a in ya]
    pb = [_dot(yb_ref[0, sl, :].astype(BF16), wpb_ref[...]) for sl in subs]
    ya2 = [(a * _sigmoid(zz + bglu_ref[...])).astype(BF16) for a, zz in zip(ya, z)]
    pa = [_dot(a, wpa_ref[...]) for a in ya2]
    mixed = [(ga_ref[0, sl, :] * p + gb_ref[0, sl, :] * q).astype(BF16)
             for sl, p, q in zip(subs, pa, pb)]
    mo = [_dot(m, wout_ref[...]) for m in mixed]
    x1 = [x_ref[0, sl, :] + g1_ref[0] * m for sl, m in zip(subs, mo)]
    h = [(x * lax.rsqrt(jnp.mean(x * x, axis=-1, keepdims=True) + EPS)) * n2_ref[...]
         * (1.0 + sc2_ref[0]) + sh2_ref[0] for x in x1]
    hb = [v.astype(BF16) for v in h]
    logits = [_dot(v, wr_ref[...]) for v in hb]
    gu = [_dot(v, wsgu_ref[...]) for v in hb]
    mid = [(g[:, :EXPERT_FF] * _sigmoid(g[:, :EXPERT_FF]) * g[:, EXPERT_FF:]).astype(BF16) for g in gu]
    shared = [_dot(m, wsd_ref[...]) for m in mid]
    for s, sl in enumerate(subs):
        sc_ref[sl, :] = _sigmoid(logits[s])
        base_ref[sl] = (x1[s] + g2_ref[0] * shared[s]).reshape(rows, *TOKEN_TILE)
        hp_ref[sl] = hb[s].reshape(rows, *TOKEN_TILE)


def _merge(ya_tb, yb, ga, gb, x, g1, sc2, sh2, g2, n2, wglu, bglu, wpa, wpb, wout, wr, wsgu, wsd):
    tm = MERGE_TM
    row3 = lambda w: pl.BlockSpec((1, tm, w), lambda b, i: (b, i, 0))
    const = lambda a: pl.BlockSpec(a.shape, lambda b, i: tuple(0 for _ in a.shape))
    mod = pl.BlockSpec((1, 1, D_MODEL), lambda b, i: (b, 0, 0))
    nt = SEQ // tm
    flat = lambda w: pl.BlockSpec((tm, w), lambda b, i: (b * nt + i, 0))
    tiles = pl.BlockSpec((tm,) + TOKEN_TILE, lambda b, i: (b * nt + i, 0, 0))
    return pl.pallas_call(
        _merge_kernel,
        grid=(BATCH, nt),
        in_specs=[pl.BlockSpec((tm, S5_WIDTH), lambda b, i: (i, b)),
                  row3(GLA_VAL), row3(D_MODEL), row3(D_MODEL), row3(D_MODEL),
                  mod, mod, mod, mod, const(n2),
                  const(wglu), const(bglu), const(wpa), const(wpb), const(wout), const(wr),
                  const(wsgu), const(wsd)],
        out_specs=[tiles, tiles, flat(N_EXPERTS)],
        out_shape=[jax.ShapeDtypeStruct((TOKENS,) + TOKEN_TILE, F32),
                   jax.ShapeDtypeStruct((TOKENS,) + TOKEN_TILE, BF16),
                   jax.ShapeDtypeStruct((TOKENS, N_EXPERTS), F32)],
        compiler_params=_params(("parallel", "parallel")),
        name="merge",
    )(ya_tb, yb, ga, gb, x, g1, sc2, sh2, g2, n2, wglu, bglu, wpa, wpb, wout, wr, wsgu, wsd)


def _first_argmax(x, iota, size):
    m = jnp.max(x, axis=0, keepdims=True)
    idx = jnp.min(jnp.where(x == m, iota, size), axis=0, keepdims=True)
    return m, idx


def _route_kernel(s_ref, bias_ref, tri_ref, e_ref, r_ref, w_ref, cnt_ref, carry_ref):
    @pl.when(pl.program_id(0) == 0)
    def _():
        carry_ref[...] = jnp.zeros_like(carry_ref)

    tr = ROUTE_TR
    neg = -jnp.inf
    s_t = s_ref[...].T
    biased = s_t + bias_ref[...]
    io_g = lax.broadcasted_iota(I32, (GROUP_SIZE, tr), 0)
    rows = []
    for g in range(N_GROUPS):
        xg = biased[g * GROUP_SIZE:(g + 1) * GROUP_SIZE, :]
        m1, i1 = _first_argmax(xg, io_g, GROUP_SIZE)
        m2 = jnp.max(jnp.where(io_g == i1, neg, xg), axis=0, keepdims=True)
        rows.append(m1 + m2)
    gs = jnp.concatenate(rows, axis=0)
    io_n = lax.broadcasted_iota(I32, (N_GROUPS, tr), 0)
    gsel = jnp.zeros((N_GROUPS, tr), F32)
    for _ in range(TOPK_GROUPS):
        _, gi = _first_argmax(gs, io_n, N_GROUPS)
        hit = io_n == gi
        gsel = jnp.where(hit, 1.0, gsel)
        gs = jnp.where(hit, neg, gs)
    masked = jnp.concatenate(
        [jnp.where(gsel[g:g + 1, :] > 0.0, biased[g * GROUP_SIZE:(g + 1) * GROUP_SIZE, :], neg)
         for g in range(N_GROUPS)], axis=0)
    io_e = lax.broadcasted_iota(I32, (N_EXPERTS, tr), 0)
    sel = jnp.zeros((N_EXPERTS, tr), F32)
    idxs = []
    for _ in range(TOP_K):
        _, ei = _first_argmax(masked, io_e, N_EXPERTS)
        hit = io_e == ei
        sel = jnp.where(hit, 1.0, sel)
        masked = jnp.where(hit, neg, masked)
        idxs.append(ei)
    den = jnp.sum(sel * s_t, axis=0, keepdims=True)
    rank = _dot(sel.astype(BF16), tri_ref[...]) + carry_ref[...]
    rks, sks = [], []
    for kk in range(TOP_K):
        hit = io_e == idxs[kk]
        rks.append(jnp.sum(jnp.where(hit, rank, 0.0), axis=0, keepdims=True))
        sks.append(jnp.sum(jnp.where(hit, s_t, 0.0), axis=0, keepdims=True))
    e_ref[...] = jnp.concatenate(idxs, axis=0)
    r_ref[...] = jnp.concatenate(rks, axis=0).astype(I32)
    w_ref[...] = jnp.concatenate(sks, axis=0) / den * ROUTE_SCALE
    carry_ref[...] += jnp.sum(sel, axis=1, keepdims=True)
    cnt_ref[...] = carry_ref[...]


def _route(scores, bias):
    tr = ROUTE_TR
    r = jnp.arange(tr)
    tri = (r[:, None] < r[None, :]).astype(BF16)
    kt = lambda dt: jax.ShapeDtypeStruct((TOP_K, TOKENS), dt)
    blk = pl.BlockSpec((TOP_K, tr), lambda i: (0, i))
    return pl.pallas_call(
        _route_kernel,
        grid=(TOKENS // tr,),
        in_specs=[pl.BlockSpec((tr, N_EXPERTS), lambda i: (i, 0)),
                  pl.BlockSpec((N_EXPERTS, 1), lambda i: (0, 0)),
                  pl.BlockSpec((tr, tr), lambda i: (0, 0))],
        out_specs=[blk, blk, blk, pl.BlockSpec((N_EXPERTS, 1), lambda i: (0, 0))],
        out_shape=[kt(I32), kt(I32), kt(F32), jax.ShapeDtypeStruct((N_EXPERTS, 1), F32)],
        scratch_shapes=[pltpu.VMEM((N_EXPERTS, 1), F32)],
        compiler_params=_params(("arbitrary",)),
        name="route",
    )(scores, bias.reshape(N_EXPERTS, 1), tri)


def _slots_kernel(e_ref, r_ref, ps_ref, d_ref):
    tr = ROUTE_TR
    io_e = lax.broadcasted_iota(I32, (N_EXPERTS, tr), 0)
    ps = ps_ref[...]
    rows = []
    for kk in range(TOP_K):
        hit = io_e == e_ref[kk:kk + 1, :]
        rows.append(jnp.sum(jnp.where(hit, ps, 0.0), axis=0, keepdims=True))
    d_ref[...] = jnp.concatenate(rows, axis=0).astype(I32) + r_ref[...]


def _slots(e_idx, rank, pad_start):
    tr = ROUTE_TR
    blk = pl.BlockSpec((TOP_K, tr), lambda i: (0, i))
    return pl.pallas_call(
        _slots_kernel,
        grid=(TOKENS // tr,),
        in_specs=[blk, blk, pl.BlockSpec((N_EXPERTS, 1), lambda i: (0, 0))],
        out_specs=blk,
        out_shape=jax.ShapeDtypeStruct((TOP_K, TOKENS), I32),
        compiler_params=_params(("parallel",)),
        name="slots",
    )(e_idx, rank, pad_start.astype(F32).reshape(N_EXPERTS, 1))


def _dispatch_kernel(ps_ref, cnt_ref, nb_ref, dest_ref, h_ref, xs_ref, zbuf, sem, zsem):
    tq = DISPATCH_TQ
    step = pl.program_id(0)
    n_used = ps_ref[N_EXPERTS - 1] // MOE_BLOCK + nb_ref[N_EXPERTS - 1]

    def zero_fill(act):
        def block(j, carry):
            r = pl.ds(pl.multiple_of(j * MOE_BLOCK, MOE_BLOCK), MOE_BLOCK)
            act(pltpu.make_async_copy(zbuf, xs_ref.at[r], zsem))
            return carry

        def expert(e, carry):
            @pl.when(cnt_ref[e] < nb_ref[e] * MOE_BLOCK)
            def _():
                block(ps_ref[e] // MOE_BLOCK + nb_ref[e] - 1, 0)
            return carry

        lax.fori_loop(0, N_EXPERTS, expert, 0)
        lax.fori_loop(n_used, N_BLOCKS, block, 0)

    @pl.when(step == 0)
    def _():
        zbuf[...] = jnp.zeros_like(zbuf)
        zero_fill(lambda cp: cp.start())
        zero_fill(lambda cp: cp.wait())

    def start(i, carry):
        for kk in range(TOP_K):
            d = dest_ref[i * TOP_K + kk]
            pltpu.make_async_copy(h_ref.at[i], xs_ref.at[d],
                                  sem).start(priority=kk % 2)
        return carry

    lax.fori_loop(0, tq, start, 0, unroll=2)
    for _ in range(TOP_K):
        pltpu.make_async_copy(h_ref, xs_ref.at[pl.ds(0, tq)], sem).wait()


def _dispatch(pad_start, counts, n_blk, dest, hp):
    tq = DISPATCH_TQ
    grid_spec = pltpu.PrefetchScalarGridSpec(
        num_scalar_prefetch=3,
        grid=(TOKENS // tq,),
        in_specs=[pl.BlockSpec((tq * TOP_K,), lambda i, *_: (i,), memory_space=pltpu.SMEM),
                  pl.BlockSpec((tq,) + TOKEN_TILE, lambda i, *_: (i, 0, 0))],
        out_specs=pl.BlockSpec(memory_space=pl.ANY),
        scratch_shapes=[pltpu.VMEM((MOE_BLOCK,) + TOKEN_TILE, hp.dtype),
                        pltpu.SemaphoreType.DMA, pltpu.SemaphoreType.DMA],
    )
    return pl.pallas_call(
        _dispatch_kernel,
        grid_spec=grid_spec,
        out_shape=jax.ShapeDtypeStruct((N_SLOTS,) + TOKEN_TILE, hp.dtype),
        compiler_params=_params(("arbitrary",)),
        name="dispatch",
    )(pad_start, counts, n_blk, dest, hp)


def _expert_kernel(ps_ref, nb_ref, wg_ref, wu_ref, wd_ref, xs_ref, ys_ref,
                   xbuf, ybuf, wgus, wds, xsem, ysem):
    e = pl.program_id(0)
    nb = nb_ref[e]
    first = ps_ref[e] // MOE_BLOCK
    n_used = ps_ref[N_EXPERTS - 1] // MOE_BLOCK + nb_ref[N_EXPERTS - 1]
    nbuf = EXPERT_NBUF

    def rows(g):
        return pl.ds(pl.multiple_of(g * MOE_BLOCK, MOE_BLOCK), MOE_BLOCK)

    def x_copy(g):
        slot = g % nbuf
        return pltpu.make_async_copy(xs_ref.at[rows(g)], xbuf.at[slot], xsem.at[slot])

    def y_copy(g):
        slot = g % nbuf
        return pltpu.make_async_copy(ybuf.at[slot], ys_ref.at[rows(g)], ysem.at[slot])

    look = nbuf - EXPERT_UNIT

    @pl.when(e == 0)
    def _():
        for g in range(look):
            @pl.when(g < n_used)
            def _():
                x_copy(g).start()

    def run(g, u):
        for d in range(u):
            @pl.when(g + look + d < n_used)
            def _():
                x_copy(g + look + d).start()

        for d in range(u):
            b = g + d

            @pl.when(b >= nbuf)
            def _():
                y_copy(b - nbuf).wait()

        for d in range(u):
            x_copy(g + d).wait()
        gus = [_dot(xbuf[(g + d) % nbuf].astype(F32).reshape(MOE_BLOCK, D_MODEL).astype(BF16), wgus[...])
               for d in range(u)]
        for d in range(u):
            gate = gus[d][:, :EXPERT_FF]
            mid = (gate * _sigmoid(gate) * gus[d][:, EXPERT_FF:]).astype(BF16)
            ybuf[(g + d) % nbuf] = (_dot(mid, wds[...]).reshape(MOE_BLOCK, *TOKEN_TILE)
                                    .astype(EXPERT_OUT_DTYPE))
        for d in range(u):
            y_copy(g + d).start()

    @pl.when(nb > 0)
    def _():
        wgus[:, :EXPERT_FF] = wg_ref[0].astype(BF16)
        wgus[:, EXPERT_FF:] = wu_ref[0].astype(BF16)
        wds[...] = wd_ref[0].astype(BF16)

        whole = nb == EXPERT_UNIT
        triple = (nb % 2 == 1) & (nb >= 3)
        pairs = jnp.where(whole, 0, jnp.where(triple, (nb - 3) // 2, nb // 2))

        def pair(j, carry):
            run(first + j * 2, 2)
            return carry

        lax.fori_loop(0, pairs, pair, 0)

        @pl.when(whole)
        def _():
            run(first, EXPERT_UNIT)

        @pl.when(triple)
        def _():
            run(first + nb - 3, 3)

        @pl.when(nb == 1)
        def _():
            run(first, 1)

    @pl.when(e == N_EXPERTS - 1)
    def _():
        for d in range(nbuf):
            g = n_used - nbuf + d

            @pl.when(g >= 0)
            def _():
                y_copy(g).wait()

        ybuf[0] = jnp.zeros((MOE_BLOCK,) + TOKEN_TILE, EXPERT_OUT_DTYPE)

        def z_copy(j):
            return pltpu.make_async_copy(ybuf.at[0], ys_ref.at[rows(j)], ysem.at[0])

        def z_start(j, carry):
            z_copy(j).start()
            return carry

        def z_wait(j, carry):
            z_copy(j).wait()
            return carry

        lax.fori_loop(n_used, N_BLOCKS, z_start, 0)
        lax.fori_loop(n_used, N_BLOCKS, z_wait, 0)


def _experts(pad_start, n_blk, xs, wg, wu, wd):
    wspec = lambda shape: pl.BlockSpec((1,) + shape, lambda e, ps, nb: (e, 0, 0))
    grid_spec = pltpu.PrefetchScalarGridSpec(
        num_scalar_prefetch=2,
        grid=(N_EXPERTS,),
        in_specs=[wspec((D_MODEL, EXPERT_FF)), wspec((D_MODEL, EXPERT_FF)),
                  wspec((EXPERT_FF, D_MODEL)), pl.BlockSpec(memory_space=pl.ANY)],
        out_specs=pl.BlockSpec(memory_space=pl.ANY),
        scratch_shapes=[pltpu.VMEM((EXPERT_NBUF, MOE_BLOCK) + TOKEN_TILE, xs.dtype),
                        pltpu.VMEM((EXPERT_NBUF, MOE_BLOCK) + TOKEN_TILE, EXPERT_OUT_DTYPE),
                        pltpu.VMEM((D_MODEL, 2 * EXPERT_FF), BF16),
                        pltpu.VMEM((EXPERT_FF, D_MODEL), BF16),
                        pltpu.SemaphoreType.DMA((EXPERT_NBUF,)),
                        pltpu.SemaphoreType.DMA((EXPERT_NBUF,))],
    )
    return pl.pallas_call(
        _expert_kernel,
        grid_spec=grid_spec,
        out_shape=jax.ShapeDtypeStruct((N_SLOTS,) + TOKEN_TILE, EXPERT_OUT_DTYPE),
        compiler_params=_params(("arbitrary",)),
        name="experts",
    )(pad_start, n_blk, wg, wu, wd, xs)


def _combine_kernel(dcur_ref, dnext_ref, w_ref, base_ref, g2_ref, fg_ref, ys_ref, o_ref,
                    buf, acc_ref, sem):
    tq = COMBINE_TQ
    i = pl.program_id(0)
    n = pl.num_programs(0)

    def start(dest_ref, to_slot, t):
        for kk in range(TOP_K):
            d = dest_ref[t * TOP_K + kk]
            pltpu.make_async_copy(ys_ref.at[d], buf.at[to_slot, kk, t],
                                  sem.at[to_slot]).start(priority=kk % 2)

    def reduce(slot, t):
        a = buf[slot, 0, t].astype(F32) * w_ref[t * TOP_K]
        for kk in range(1, TOP_K):
            a = a + buf[slot, kk, t].astype(F32) * w_ref[t * TOP_K + kk]
        acc_ref[t] = a

    @pl.when(i == 0)
    def _():
        lax.fori_loop(0, tq, lambda t, c: start(dcur_ref, 0, t) or c, 0, unroll=4)

    for slot in range(2):
        @pl.when(i % 2 == slot)
        def _(slot=slot):
            @pl.when(i + 1 < n)
            def _():
                lax.fori_loop(0, tq, lambda t, c: start(dnext_ref, 1 - slot, t) or c, 0, unroll=4)

            for kk in range(TOP_K):
                pltpu.make_async_copy(ys_ref.at[pl.ds(0, tq)], buf.at[slot, kk],
                                      sem.at[slot]).wait()
            lax.fori_loop(0, tq, lambda t, c: reduce(slot, t) or c, 0, unroll=8)

    xo = base_ref[...] + g2_ref[0] * acc_ref[...]
    ms = jnp.sum(jnp.sum(xo * xo, axis=2, keepdims=True), axis=1, keepdims=True) * (1.0 / D_MODEL)
    o_ref[0] = ((xo * lax.rsqrt(ms + EPS)) * fg_ref[...]).reshape(tq, D_MODEL)


def _combine(dest, w_flat, base, g2, final_g, ys):
    tq = COMBINE_TQ
    nt = SEQ // tq
    n = TOKENS // tq
    smem = lambda imap: pl.BlockSpec((tq * TOP_K,), imap, memory_space=pltpu.SMEM)
    return pl.pallas_call(
        _combine_kernel,
        grid=(n,),
        in_specs=[smem(lambda i: (i,)), smem(lambda i: (jnp.minimum(i + 1, n - 1),)),
                  smem(lambda i: (i,)),
                  pl.BlockSpec((tq,) + TOKEN_TILE, lambda i: (i, 0, 0)),
                  pl.BlockSpec((1,) + TOKEN_TILE, lambda i: (i // nt, 0, 0)),
                  pl.BlockSpec(TOKEN_TILE, lambda i: (0, 0)),
                  pl.BlockSpec(memory_space=pl.ANY)],
        out_specs=pl.BlockSpec((1, tq, D_MODEL), lambda i: (i // nt, i % nt, 0)),
        out_shape=jax.ShapeDtypeStruct((BATCH, SEQ, D_MODEL), F32),
        scratch_shapes=[pltpu.VMEM((2, TOP_K, tq) + TOKEN_TILE, ys.dtype),
                        pltpu.VMEM((tq,) + TOKEN_TILE, F32),
                        pltpu.SemaphoreType.DMA((2,))],
        compiler_params=_params(("arbitrary",)),
        name="combine",
    )(dest, dest, w_flat, base, g2.reshape((BATCH,) + TOKEN_TILE), final_g.reshape(TOKEN_TILE), ys)


def kernel(x, c, ada_w, ada_b, norm1_g, w_in, s5_lam_re, s5_lam_im, s5_log_dt, s5_b_re, s5_b_im,
           s5_c_re, s5_c_im, s5_d, s5_w_glu, s5_b_glu, w_proj_a, gla_w_gk2, gla_b_gk2, gla_norm_g,
           w_proj_b, w_out, norm2_g, router_w, router_bias, exp_w_gate, exp_w_up, exp_w_down,
           sh_w_gate, sh_w_up, sh_w_down, final_g):
    l = 0
    mod = _ada(c, ada_w[l], ada_b[l])
    sh1, sc1, g1, sh2, sc2, g2 = [m.reshape(BATCH, 1, D_MODEL) for m in jnp.split(mod, 6, axis=-1)]

    w = w_in[l]
    gk0 = _V0 + GLA_VAL
    w_main = jnp.concatenate([w[:, :gk0], w[:, gk0 + GLA_GATE_RANK:]], axis=1).astype(BF16)
    w_gk = jnp.pad(w[:, gk0:gk0 + GLA_GATE_RANK], ((0, 0), (0, LANES - GLA_GATE_RANK))).astype(BF16)
    w_gk2 = jnp.pad(gla_w_gk2[l], ((0, LANES - GLA_GATE_RANK), (0, 0))).astype(BF16)
    u_tb, q, k, v, la, rs, ga, gb = _inproj(
        x, norm1_g[l].reshape(1, D_MODEL), sc1, sh1, w_main, w_gk, w_gk2,
        gla_b_gk2[l].reshape(1, GLA_KEY))
    tables = _s5_tables(s5_lam_re[l], s5_lam_im[l], s5_log_dt[l], s5_b_re[l], s5_b_im[l],
                        s5_c_re[l], s5_c_im[l], s5_d[l])
    ya_tb = _s5(u_tb, *tables)
    yb = _gla(q, k, la, v, rs, gla_norm_g[l])

    wsgu = jnp.concatenate([sh_w_gate[l], sh_w_up[l]], axis=1).astype(BF16)
    base, hp, scores = _merge(
        ya_tb, yb, ga, gb, x, g1, sc2, sh2, g2,
        norm2_g[l].reshape(1, D_MODEL), s5_w_glu[l].astype(BF16), s5_b_glu[l].reshape(1, S5_WIDTH),
        w_proj_a[l].astype(BF16), w_proj_b[l].astype(BF16), w_out[l].astype(BF16),
        router_w[l].astype(BF16), wsgu, sh_w_down[l].astype(BF16))

    e_idx, rank, w_k, cnt = _route(scores, router_bias[l])
    counts = cnt[:, 0].astype(I32)
    padded = (counts + MOE_BLOCK - 1) // MOE_BLOCK * MOE_BLOCK
    pad_end = jnp.cumsum(padded)
    pad_start = pad_end - padded
    dest = _slots(e_idx, rank, pad_start).T.reshape(-1)
    pad_start = pad_start.astype(I32)
    n_blk = (padded // MOE_BLOCK).astype(I32)
    xs = _dispatch(pad_start, counts, n_blk, dest, hp)
    ys = _experts(pad_start, n_blk, xs, exp_w_gate[l], exp_w_up[l], exp_w_down[l])
    return _combine(dest, w_k.T.reshape(-1), base, g2, final_g, ys)
```

```python
import jax
import jax.numpy as jnp
from jax import lax
from jax.experimental import pallas as pl
from jax.experimental.pallas import tpu as pltpu

F32 = jnp.float32
BF16 = jnp.bfloat16
I32 = jnp.int32

D_MODEL = 1024
BATCH = 8
SEQ = 2048
TOKENS = BATCH * SEQ
S5_WIDTH = 512
S5_GROUP = 16
S5_GROUPS = 32
S5_STATE = 64
S5_COLS = S5_GROUPS * S5_STATE
GLA_HEADS = 4
GLA_DK = 64
GLA_DV = 128
GLA_KEY = GLA_HEADS * GLA_DK
GLA_VAL = GLA_HEADS * GLA_DV
GLA_GATE_RANK = 16
GLA_GATE_TAU = 16.0
GLA_CHUNK = 64
GLA_SUB = 16
N_EXPERTS = 256
TOP_K = 8
N_GROUPS = 8
GROUP_SIZE = N_EXPERTS // N_GROUPS
TOPK_GROUPS = 4
EXPERT_FF = 256
ROUTE_SCALE = 2.5
MOE_BLOCK = 128
EPS = 1e-6
N_SLOTS = -(-(TOKENS * TOP_K + N_EXPERTS * (MOE_BLOCK - 1)) // MOE_BLOCK) * MOE_BLOCK
N_BLOCKS = N_SLOTS // MOE_BLOCK

LANES = 128
SUBLANES = 8
TOKEN_TILE = (SUBLANES, LANES)
assert D_MODEL == SUBLANES * LANES
VMEM_LIMIT = 56 * 1024 * 1024

ADA_TN = 1536
INPROJ_TM = 512
S5_TC = 128
S5_RB = 32
S5_COL_CHUNK = 512
GLA_TG = 512
MERGE_TM = 512
MERGE_SPLIT = 2
ROUTE_TR = 512
DISPATCH_TQ = 1024
COMBINE_TQ = 512
EXPERT_UNIT = 4
EXPERT_NBUF = 10
EXPERT_OUT_DTYPE = BF16
EXP_CLAMP = 60.0


def _params(sem, vmem=VMEM_LIMIT):
    return pltpu.CompilerParams(dimension_semantics=sem, vmem_limit_bytes=vmem)


def _dot(a, b):
    return jnp.dot(a, b, preferred_element_type=F32)


def _dot_nt(a, b):
    return lax.dot_general(a, b, (((1,), (1,)), ((), ())), preferred_element_type=F32)


def _sigmoid(x):
    return jax.nn.sigmoid(x)


def _ada_kernel(c_ref, w_ref, b_ref, o_ref):
    c = c_ref[...]
    s = (c * _sigmoid(c)).astype(BF16)
    o_ref[...] = _dot(s, w_ref[...].astype(BF16)) + b_ref[...]


def _ada(c, w, b):
    n = w.shape[1]
    return pl.pallas_call(
        _ada_kernel,
        grid=(n // ADA_TN,),
        in_specs=[
            pl.BlockSpec((BATCH, D_MODEL), lambda j: (0, 0)),
            pl.BlockSpec((D_MODEL, ADA_TN), lambda j: (0, j)),
            pl.BlockSpec((1, ADA_TN), lambda j: (0, j)),
        ],
        out_specs=pl.BlockSpec((BATCH, ADA_TN), lambda j: (0, j)),
        out_shape=jax.ShapeDtypeStruct((BATCH, n), F32),
        compiler_params=_params(("arbitrary",)),
        name="ada",
    )(c, w, b.reshape(1, n))


_U0, _Q0, _K0, _V0, _R0, _GA0, _GB0, _END = 0, 512, 768, 1024, 1536, 2048, 3072, 4096


def _inproj_kernel(x_ref, g_ref, sc_ref, sh_ref, wm_ref, wgk_ref, wgk2_ref, bgk_ref,
                   u_ref, q_ref, k_ref, v_ref, la_ref, rs_ref, ga_ref, gb_ref):
    x = x_ref[0]
    r = lax.rsqrt(jnp.mean(x * x, axis=-1, keepdims=True) + EPS)
    h = (x * r) * g_ref[...] * (1.0 + sc_ref[0]) + sh_ref[0]
    hb = h.astype(BF16)

    def seg(lo, hi):
        return _dot(hb, wm_ref[:, lo:hi])

    u_ref[...] = seg(_U0, _Q0)
    q_ref[0] = seg(_Q0, _K0) * (GLA_DK ** -0.5)
    k_ref[0] = seg(_K0, _V0)
    v_ref[0] = seg(_V0, _R0)
    rr = seg(_R0, _GA0)
    rs_ref[0] = rr * _sigmoid(rr)
    ga_ref[0] = _sigmoid(seg(_GA0, _GB0))
    gb_ref[0] = _sigmoid(seg(_GB0, _END))
    gk = _dot(hb, wgk_ref[...])
    z = _dot(gk.astype(BF16), wgk2_ref[...]) + bgk_ref[...]
    la_ref[0] = -(jnp.maximum(-z, 0.0) + jnp.log1p(jnp.exp(-jnp.abs(z)))) * (1.0 / GLA_GATE_TAU)


def _inproj(x, g, sc, sh, w_main, w_gk, w_gk2, b_gk2):
    tm = INPROJ_TM
    nt = SEQ // tm
    row3 = lambda w: pl.BlockSpec((1, tm, w), lambda b, i: (b, i, 0))
    const = lambda shape: pl.BlockSpec(shape, lambda b, i: tuple(0 for _ in shape))
    mod = pl.BlockSpec((1, 1, D_MODEL), lambda b, i: (b, 0, 0))
    bld = lambda w: jax.ShapeDtypeStruct((BATCH, SEQ, w), F32)
    return pl.pallas_call(
        _inproj_kernel,
        grid=(BATCH, nt),
        in_specs=[row3(D_MODEL), const((1, D_MODEL)), mod, mod,
                  const((D_MODEL, _END)), const((D_MODEL, LANES)), const((LANES, GLA_KEY)),
                  const((1, GLA_KEY))],
        out_specs=[pl.BlockSpec((tm, S5_WIDTH), lambda b, i: (i, b)),
                   row3(GLA_KEY), row3(GLA_KEY), row3(GLA_VAL), row3(GLA_KEY), row3(GLA_VAL),
                   row3(D_MODEL), row3(D_MODEL)],
        out_shape=[jax.ShapeDtypeStruct((SEQ, BATCH * S5_WIDTH), F32),
                   bld(GLA_KEY), bld(GLA_KEY), bld(GLA_VAL), bld(GLA_KEY), bld(GLA_VAL),
                   bld(D_MODEL), bld(D_MODEL)],
        compiler_params=_params(("parallel", "parallel")),
        name="inproj",
    )(x, g, sc, sh, w_main, w_gk, w_gk2, b_gk2)


def _s5_kernel(u_ref, bre_ref, bim_ref, cre_ref, cim_ref, are_ref, aim_ref, d_ref,
               y_ref, s_ref, st_ref):
    half = S5_COLS // 2

    @pl.when(pl.program_id(0) == 0)
    def _():
        st_ref[...] = jnp.zeros_like(st_ref)

    u = u_ref[...].reshape(S5_TC, BATCH, S5_WIDTH).reshape(S5_TC * BATCH, S5_WIDTH)
    ub = u.astype(BF16)
    nblk = S5_TC // S5_RB
    brows = [slice(r * S5_RB * BATCH, (r + 1) * S5_RB * BATCH) for r in range(nblk)]
    for r in range(nblk):
        for j in range(2):
            uj = ub[brows[r], j * 256:(j + 1) * 256]
            s_ref[brows[r], j * half:(j + 1) * half] = _dot(uj, bre_ref[j])
            s_ref[brows[r], S5_COLS + j * half:S5_COLS + (j + 1) * half] = _dot(uj, bim_ref[j])

    ncc = S5_COLS // S5_COL_CHUNK
    re_sl = [slice(cc * S5_COL_CHUNK, (cc + 1) * S5_COL_CHUNK) for cc in range(ncc)]
    im_sl = [slice(S5_COLS + cc * S5_COL_CHUNK, S5_COLS + (cc + 1) * S5_COL_CHUNK) for cc in range(ncc)]
    sr = [st_ref[:, sl] for sl in re_sl]
    si = [st_ref[:, sl] for sl in im_sl]
    for r in range(nblk):
        for t in range(r * S5_RB, (r + 1) * S5_RB):
            rows = slice(t * BATCH, (t + 1) * BATCH)
            for cc in range(ncc):
                ar = are_ref[:, re_sl[cc]]
                ai = aim_ref[:, re_sl[cc]]
                nr = ar * sr[cc] - ai * si[cc] + s_ref[rows, re_sl[cc]]
                ni = ar * si[cc] + ai * sr[cc] + s_ref[rows, im_sl[cc]]
                s_ref[rows, re_sl[cc]] = nr
                s_ref[rows, im_sl[cc]] = ni
                sr[cc], si[cc] = nr, ni
    for cc in range(ncc):
        st_ref[:, re_sl[cc]] = sr[cc]
        st_ref[:, im_sl[cc]] = si[cc]

    for r in range(nblk):
        ys = []
        for j in range(2):
            sre = s_ref[brows[r], j * half:(j + 1) * half].astype(BF16)
            sim = s_ref[brows[r], S5_COLS + j * half:S5_COLS + (j + 1) * half].astype(BF16)
            ys.append(_dot(sre, cre_ref[j]) + _dot(sim, cim_ref[j]))
        y = jnp.concatenate(ys, axis=1) + d_ref[...] * u[brows[r], :]
        y_ref[r * S5_RB:(r + 1) * S5_RB, :] = (
            jax.nn.gelu(y).reshape(S5_RB, BATCH, S5_WIDTH).reshape(S5_RB, BATCH * S5_WIDTH))


def _s5(u_tb, bre, bim, cre, cimn, are, aim, dflat):
    rows = S5_TC * BATCH
    const = lambda shape: pl.BlockSpec(shape, lambda i: tuple(0 for _ in shape))
    return pl.pallas_call(
        _s5_kernel,
        grid=(SEQ // S5_TC,),
        in_specs=[pl.BlockSpec((S5_TC, BATCH * S5_WIDTH), lambda i: (i, 0)),
                  const(bre.shape), const(bim.shape), const(cre.shape), const(cimn.shape),
                  const(are.shape), const(aim.shape), const(dflat.shape)],
        out_specs=pl.BlockSpec((S5_TC, BATCH * S5_WIDTH), lambda i: (i, 0)),
        out_shape=jax.ShapeDtypeStruct((SEQ, BATCH * S5_WIDTH), F32),
        scratch_shapes=[pltpu.VMEM((rows, 2 * S5_COLS), F32),
                        pltpu.VMEM((BATCH, 2 * S5_COLS), F32)],
        compiler_params=_params(("arbitrary",)),
        name="s5",
    )(u_tb, bre, bim, cre, cimn, are, aim, dflat)


def _s5_tables(lam_re, lam_im, log_dt, b_re, b_im, c_re, c_im, d_skip):
    lr = lam_re.astype(F32)
    li = lam_im.astype(F32)
    dt = jnp.exp(log_dt.astype(F32))[:, None]
    mag = jnp.exp(lr * dt)
    abar_re = mag * jnp.cos(li * dt)
    abar_im = mag * jnp.sin(li * dt)
    den = lr * lr + li * li
    num_re = abar_re - 1.0
    coef_re = (num_re * lr + abar_im * li) / den
    coef_im = (abar_im * lr - num_re * li) / den
    cr, ci = coef_re[..., None], coef_im[..., None]
    br, bi = b_re.astype(F32), b_im.astype(F32)
    bbar_re = cr * br - ci * bi
    bbar_im = cr * bi + ci * br
    eye = jnp.eye(S5_GROUPS // 2, dtype=F32)

    def in_map(bb):
        bb = bb.reshape(2, S5_GROUPS // 2, S5_STATE, S5_GROUP)
        return jnp.einsum('jgph,gk->jghkp', bb, eye).reshape(2, 256, S5_COLS // 2).astype(BF16)

    def out_map(cc):
        cc = cc.reshape(2, S5_GROUPS // 2, S5_GROUP, S5_STATE)
        return jnp.einsum('jghp,gk->jgpkh', cc, eye).reshape(2, S5_COLS // 2, 256).astype(BF16)

    are = jnp.broadcast_to(abar_re.reshape(1, S5_COLS), (BATCH, S5_COLS))
    aim = jnp.broadcast_to(abar_im.reshape(1, S5_COLS), (BATCH, S5_COLS))
    return (in_map(bbar_re), in_map(bbar_im), out_map(c_re.astype(F32)), out_map(-c_im.astype(F32)),
            are, aim, d_skip.astype(F32).reshape(1, S5_WIDTH))


def _gla_kernel(q_ref, k_ref, la_ref, v_ref, rs_ref, ng_ref, tt_ref, o_ref, st_ref):
    @pl.when(pl.program_id(1) == 0)
    def _():
        st_ref[...] = jnp.zeros_like(st_ref)

    c = GLA_CHUNK
    tt = tt_ref[...]
    row = lax.broadcasted_iota(I32, (c, GLA_KEY), 0)
    ri = lax.broadcasted_iota(I32, (c, c), 0)
    ci = lax.broadcasted_iota(I32, (c, c), 1)
    causal = ci <= ri
    nsub = c // GLA_SUB

    heads = range(GLA_HEADS)
    hsl = [slice(h * GLA_DK, (h + 1) * GLA_DK) for h in heads]
    vsl = [slice(h * GLA_DV, (h + 1) * GLA_DV) for h in heads]
    chunks = range(GLA_TG // c)
    intra, kv, qdec, dec = [], [], [], []

    sls = [slice(ch * c, (ch + 1) * c) for ch in chunks]
    csum = []
    for ch in chunks:
        g = la_ref[0, sls[ch], :]
        g1 = g.astype(BF16)
        r1 = g - g1.astype(F32)
        g2 = r1.astype(BF16)
        g3 = (r1 - g2.astype(F32)).astype(BF16)
        csum.append(_dot(tt, g1) + _dot(tt, g2) + _dot(tt, g3))
    qbigs, kbigs, kdec = [], [], []
    for ch in chunks:
        sl = sls[ch]
        cs = csum[ch]
        b = cs[0:c]
        cl = cs[c:2 * c]
        ref_pt = b - cl
        q = q_ref[0, sl, :]
        k = k_ref[0, sl, :]
        qt = q * jnp.exp(cl)
        qe = (q * jnp.exp(b)).astype(BF16)
        blast = b[c - 1:c, :]
        ks = (k * jnp.exp(blast - b)).astype(BF16)
        k_sub = []
        q_sub = []
        for s in range(nsub):
            rs_ = ref_pt[s * GLA_SUB:s * GLA_SUB + 1, :]
            k_sub.append(k * jnp.exp(jnp.minimum(rs_ - b, EXP_CLAMP)))
            q_sub.append(jnp.where((row >= s * GLA_SUB) & (row < (s + 1) * GLA_SUB), qt, 0.0))
        qbigs.append([jnp.concatenate([x[:, hsl[h]] for x in q_sub], axis=1).astype(BF16)
                      for h in heads])
        kbigs.append([jnp.concatenate([x[:, hsl[h]] for x in k_sub], axis=1).astype(BF16)
                      for h in heads])
        kdec.append(ks)
        qdec.append(qe)
        dec.append(jnp.exp(blast))
    scores = [[_dot_nt(qbigs[ch][h], kbigs[ch][h]) for h in heads] for ch in chunks]
    for ch in chunks:
        kv.append([_dot(v_ref[0, sls[ch], vsl[h]].T.astype(BF16), kdec[ch][:, hsl[h]])
                   for h in heads])
    for ch in chunks:
        intra.append([_dot(jnp.where(causal, scores[ch][h], 0.0).astype(BF16),
                           v_ref[0, sls[ch], vsl[h]].astype(BF16)) for h in heads])

    st = [st_ref[h] for h in heads]
    for ch in chunks:
        sl = slice(ch * c, (ch + 1) * c)
        inter = [_dot_nt(qdec[ch][:, hsl[h]], st[h].astype(BF16)) for h in heads]
        for h in heads:
            o = inter[h] + intra[ch][h]
            st[h] = st[h] * dec[ch][:, hsl[h]] + kv[ch][h]
            on = o * lax.rsqrt(jnp.mean(o * o, axis=-1, keepdims=True) + EPS) * ng_ref[...]
            o_ref[0, sl, vsl[h]] = on * rs_ref[0, sl, vsl[h]]
    for h in heads:
        st_ref[h] = st[h]


def _gla(q, k, la, v, rs, norm_g):
    tg = GLA_TG
    c = GLA_CHUNK
    r = jnp.arange(c)
    tri = (r[None, :] <= r[:, None])
    blk = tri & ((r[None, :] // GLA_SUB) == (r[:, None] // GLA_SUB))
    tt = jnp.concatenate([tri, blk], axis=0).astype(BF16)
    row3 = lambda w: pl.BlockSpec((1, tg, w), lambda b, i: (b, i, 0))
    const = lambda shape: pl.BlockSpec(shape, lambda b, i: tuple(0 for _ in shape))
    return pl.pallas_call(
        _gla_kernel,
        grid=(BATCH, SEQ // tg),
        in_specs=[row3(GLA_KEY), row3(GLA_KEY), row3(GLA_KEY), row3(GLA_VAL), row3(GLA_VAL),
                  const((1, GLA_DV)), const((2 * c, c))],
        out_specs=row3(GLA_VAL),
        out_shape=jax.ShapeDtypeStruct((BATCH, SEQ, GLA_VAL), F32),
        scratch_shapes=[pltpu.VMEM((GLA_HEADS, GLA_DV, GLA_DK), F32)],
        compiler_params=_params(("parallel", "arbitrary")),
        name="gla",
    )(q, k, la, v, rs, norm_g.reshape(1, GLA_DV), tt)


def _merge_kernel(ya_ref, yb_ref, ga_ref, gb_ref, x_ref, g1_ref, sc2_ref, sh2_ref, g2_ref, n2_ref,
                  wglu_ref, bglu_ref, wpa_ref, wpb_ref, wout_ref, wr_ref, wsgu_ref, wsd_ref,
                  base_ref, hp_ref, sc_ref):
    rows = MERGE_TM // MERGE_SPLIT
    subs = [slice(s * rows, (s + 1) * rows) for s in range(MERGE_SPLIT)]
    ya = [ya_ref[sl, :] for sl in subs]
    z = [_dot(a.astype(BF16), wglu_ref[...]) for a in ya]
    pb = [_dot(yb_ref[0, sl, :].astype(BF16), wpb_ref[...]) for sl in subs]
    ya2 = [(a * _sigmoid(zz + bglu_ref[...])).astype(BF16) for a, zz in zip(ya, z)]
    pa = [_dot(a, wpa_ref[...]) for a in ya2]
    mixed = [(ga_ref[0, sl, :] * p + gb_ref[0, sl, :] * q).astype(BF16)
             for sl, p, q in zip(subs, pa, pb)]
    mo = [_dot(m, wout_ref[...]) for m in mixed]
    x1 = [x_ref[0, sl, :] + g1_ref[0] * m for sl, m in zip(subs, mo)]
    h = [(x * lax.rsqrt(jnp.mean(x * x, axis=-1, keepdims=True) + EPS)) * n2_ref[...]
         * (1.0 + sc2_ref[0]) + sh2_ref[0] for x in x1]
    hb = [v.astype(BF16) for v in h]
    logits = [_dot(v, wr_ref[...]) for v in hb]
    gu = [_dot(v, wsgu_ref[...]) for v in hb]
    mid = [(g[:, :EXPERT_FF] * _sigmoid(g[:, :EXPERT_FF]) * g[:, EXPERT_FF:]).astype(BF16) for g in gu]
    shared = [_dot(m, wsd_ref[...]) for m in mid]
    for s, sl in enumerate(subs):
        sc_ref[sl, :] = _sigmoid(logits[s])
        base_ref[sl] = (x1[s] + g2_ref[0] * shared[s]).reshape(rows, *TOKEN_TILE)
        hp_ref[sl] = hb[s].reshape(rows, *TOKEN_TILE)


def _merge(ya_tb, yb, ga, gb, x, g1, sc2, sh2, g2, n2, wglu, bglu, wpa, wpb, wout, wr, wsgu, wsd):
    tm = MERGE_TM
    row3 = lambda w: pl.BlockSpec((1, tm, w), lambda b, i: (b, i, 0))
    const = lambda a: pl.BlockSpec(a.shape, lambda b, i: tuple(0 for _ in a.shape))
    mod = pl.BlockSpec((1, 1, D_MODEL), lambda b, i: (b, 0, 0))
    nt = SEQ // tm
    flat = lambda w: pl.BlockSpec((tm, w), lambda b, i: (b * nt + i, 0))
    tiles = pl.BlockSpec((tm,) + TOKEN_TILE, lambda b, i: (b * nt + i, 0, 0))
    return pl.pallas_call(
        _merge_kernel,
        grid=(BATCH, nt),
        in_specs=[pl.BlockSpec((tm, S5_WIDTH), lambda b, i: (i, b)),
                  row3(GLA_VAL), row3(D_MODEL), row3(D_MODEL), row3(D_MODEL),
                  mod, mod, mod, mod, const(n2),
                  const(wglu), const(bglu), const(wpa), const(wpb), const(wout), const(wr),
                  const(wsgu), const(wsd)],
        out_specs=[tiles, tiles, flat(N_EXPERTS)],
        out_shape=[jax.ShapeDtypeStruct((TOKENS,) + TOKEN_TILE, F32),
                   jax.ShapeDtypeStruct((TOKENS,) + TOKEN_TILE, BF16),
                   jax.ShapeDtypeStruct((TOKENS, N_EXPERTS), F32)],
        compiler_params=_params(("parallel", "parallel")),
        name="merge",
    )(ya_tb, yb, ga, gb, x, g1, sc2, sh2, g2, n2, wglu, bglu, wpa, wpb, wout, wr, wsgu, wsd)


def _first_argmax(x, iota, size):
    m = jnp.max(x, axis=0, keepdims=True)
    idx = jnp.min(jnp.where(x == m, iota, size), axis=0, keepdims=True)
    return m, idx


def _route_kernel(s_ref, bias_ref, tri_ref, e_ref, r_ref, w_ref, cnt_ref, carry_ref):
    @pl.when(pl.program_id(0) == 0)
    def _():
        carry_ref[...] = jnp.zeros_like(carry_ref)

    tr = ROUTE_TR
    neg = -jnp.inf
    s_t = s_ref[...].T
    biased = s_t + bias_ref[...]
    io_g = lax.broadcasted_iota(I32, (GROUP_SIZE, tr), 0)
    rows = []
    for g in range(N_GROUPS):
        xg = biased[g * GROUP_SIZE:(g + 1) * GROUP_SIZE, :]
        m1, i1 = _first_argmax(xg, io_g, GROUP_SIZE)
        m2 = jnp.max(jnp.where(io_g == i1, neg, xg), axis=0, keepdims=True)
        rows.append(m1 + m2)
    gs = jnp.concatenate(rows, axis=0)
    io_n = lax.broadcasted_iota(I32, (N_GROUPS, tr), 0)
    gsel = jnp.zeros((N_GROUPS, tr), F32)
    for _ in range(TOPK_GROUPS):
        _, gi = _first_argmax(gs, io_n, N_GROUPS)
        hit = io_n == gi
        gsel = jnp.where(hit, 1.0, gsel)
        gs = jnp.where(hit, neg, gs)
    masked = jnp.concatenate(
        [jnp.where(gsel[g:g + 1, :] > 0.0, biased[g * GROUP_SIZE:(g + 1) * GROUP_SIZE, :], neg)
         for g in range(N_GROUPS)], axis=0)
    io_e = lax.broadcasted_iota(I32, (N_EXPERTS, tr), 0)
    sel = jnp.zeros((N_EXPERTS, tr), F32)
    idxs = []
    for _ in range(TOP_K):
        _, ei = _first_argmax(masked, io_e, N_EXPERTS)
        hit = io_e == ei
        sel = jnp.where(hit, 1.0, sel)
        masked = jnp.where(hit, neg, masked)
        idxs.append(ei)
    den = jnp.sum(sel * s_t, axis=0, keepdims=True)
    rank = _dot(sel.astype(BF16), tri_ref[...]) + carry_ref[...]
    rks, sks = [], []
    for kk in range(TOP_K):
        hit = io_e == idxs[kk]
        rks.append(jnp.sum(jnp.where(hit, rank, 0.0), axis=0, keepdims=True))
        sks.append(jnp.sum(jnp.where(hit, s_t, 0.0), axis=0, keepdims=True))
    e_ref[...] = jnp.concatenate(idxs, axis=0)
    r_ref[...] = jnp.concatenate(rks, axis=0).astype(I32)
    w_ref[...] = jnp.concatenate(sks, axis=0) / den * ROUTE_SCALE
    carry_ref[...] += jnp.sum(sel, axis=1, keepdims=True)
    cnt_ref[...] = carry_ref[...]


def _route(scores, bias):
    tr = ROUTE_TR
    r = jnp.arange(tr)
    tri = (r[:, None] < r[None, :]).astype(BF16)
    kt = lambda dt: jax.ShapeDtypeStruct((TOP_K, TOKENS), dt)
    blk = pl.BlockSpec((TOP_K, tr), lambda i: (0, i))
    return pl.pallas_call(
        _route_kernel,
        grid=(TOKENS // tr,),
        in_specs=[pl.BlockSpec((tr, N_EXPERTS), lambda i: (i, 0)),
                  pl.BlockSpec((N_EXPERTS, 1), lambda i: (0, 0)),
                  pl.BlockSpec((tr, tr), lambda i: (0, 0))],
        out_specs=[blk, blk, blk, pl.BlockSpec((N_EXPERTS, 1), lambda i: (0, 0))],
        out_shape=[kt(I32), kt(I32), kt(F32), jax.ShapeDtypeStruct((N_EXPERTS, 1), F32)],
        scratch_shapes=[pltpu.VMEM((N_EXPERTS, 1), F32)],
        compiler_params=_params(("arbitrary",)),
        name="route",
    )(scores, bias.reshape(N_EXPERTS, 1), tri)


def _slots_kernel(e_ref, r_ref, ps_ref, d_ref):
    tr = ROUTE_TR
    io_e = lax.broadcasted_iota(I32, (N_EXPERTS, tr), 0)
    ps = ps_ref[...]
    rows = []
    for kk in range(TOP_K):
        hit = io_e == e_ref[kk:kk + 1, :]
        rows.append(jnp.sum(jnp.where(hit, ps, 0.0), axis=0, keepdims=True))
    d_ref[...] = jnp.concatenate(rows, axis=0).astype(I32) + r_ref[...]


def _slots(e_idx, rank, pad_start):
    tr = ROUTE_TR
    blk = pl.BlockSpec((TOP_K, tr), lambda i: (0, i))
    return pl.pallas_call(
        _slots_kernel,
        grid=(TOKENS // tr,),
        in_specs=[blk, blk, pl.BlockSpec((N_EXPERTS, 1), lambda i: (0, 0))],
        out_specs=blk,
        out_shape=jax.ShapeDtypeStruct((TOP_K, TOKENS), I32),
        compiler_params=_params(("parallel",)),
        name="slots",
    )(e_idx, rank, pad_start.astype(F32).reshape(N_EXPERTS, 1))


def _dispatch_kernel(ps_ref, cnt_ref, nb_ref, dest_ref, h_ref, xs_ref, zbuf, sem, zsem):
    tq = DISPATCH_TQ
    step = pl.program_id(0)
    n_used = ps_ref[N_EXPERTS - 1] // MOE_BLOCK + nb_ref[N_EXPERTS - 1]

    def zero_fill(act):
        def block(j, carry):
            r = pl.ds(pl.multiple_of(j * MOE_BLOCK, MOE_BLOCK), MOE_BLOCK)
            act(pltpu.make_async_copy(zbuf, xs_ref.at[r], zsem))
            return carry

        def expert(e, carry):
            @pl.when(cnt_ref[e] < nb_ref[e] * MOE_BLOCK)
            def _():
                block(ps_ref[e] // MOE_BLOCK + nb_ref[e] - 1, 0)
            return carry

        lax.fori_loop(0, N_EXPERTS, expert, 0)
        lax.fori_loop(n_used, N_BLOCKS, block, 0)

    @pl.when(step == 0)
    def _():
        zbuf[...] = jnp.zeros_like(zbuf)
        zero_fill(lambda cp: cp.start())
        zero_fill(lambda cp: cp.wait())

    def start(i, carry):
        for kk in range(TOP_K):
            d = dest_ref[i * TOP_K + kk]
            pltpu.make_async_copy(h_ref.at[i], xs_ref.at[d],
                                  sem).start(priority=kk % 2)
        return carry

    lax.fori_loop(0, tq, start, 0, unroll=2)
    for _ in range(TOP_K):
        pltpu.make_async_copy(h_ref, xs_ref.at[pl.ds(0, tq)], sem).wait()


def _dispatch(pad_start, counts, n_blk, dest, hp):
    tq = DISPATCH_TQ
    grid_spec = pltpu.PrefetchScalarGridSpec(
        num_scalar_prefetch=3,
        grid=(TOKENS // tq,),
        in_specs=[pl.BlockSpec((tq * TOP_K,), lambda i, *_: (i,), memory_space=pltpu.SMEM),
                  pl.BlockSpec((tq,) + TOKEN_TILE, lambda i, *_: (i, 0, 0))],
        out_specs=pl.BlockSpec(memory_space=pl.ANY),
        scratch_shapes=[pltpu.VMEM((MOE_BLOCK,) + TOKEN_TILE, hp.dtype),
                        pltpu.SemaphoreType.DMA, pltpu.SemaphoreType.DMA],
    )
    return pl.pallas_call(
        _dispatch_kernel,
        grid_spec=grid_spec,
        out_shape=jax.ShapeDtypeStruct((N_SLOTS,) + TOKEN_TILE, hp.dtype),
        compiler_params=_params(("arbitrary",)),
        name="dispatch",
    )(pad_start, counts, n_blk, dest, hp)


def _expert_kernel(ps_ref, nb_ref, wg_ref, wu_ref, wd_ref, xs_ref, ys_ref,
                   xbuf, ybuf, wgus, wds, xsem, ysem):
    e = pl.program_id(0)
    nb = nb_ref[e]
    first = ps_ref[e] // MOE_BLOCK
    n_used = ps_ref[N_EXPERTS - 1] // MOE_BLOCK + nb_ref[N_EXPERTS - 1]
    nbuf = EXPERT_NBUF

    def rows(g):
        return pl.ds(pl.multiple_of(g * MOE_BLOCK, MOE_BLOCK), MOE_BLOCK)

    def x_copy(g):
        slot = g % nbuf
        return pltpu.make_async_copy(xs_ref.at[rows(g)], xbuf.at[slot], xsem.at[slot])

    def y_copy(g):
        slot = g % nbuf
        return pltpu.make_async_copy(ybuf.at[slot], ys_ref.at[rows(g)], ysem.at[slot])

    look = nbuf - EXPERT_UNIT

    @pl.when(e == 0)
    def _():
        for g in range(look):
            @pl.when(g < n_used)
            def _():
                x_copy(g).start()

    def run(g, u):
        for d in range(u):
            @pl.when(g + look + d < n_used)
            def _():
                x_copy(g + look + d).start()

        for d in range(u):
            b = g + d

            @pl.when(b >= nbuf)
            def _():
                y_copy(b - nbuf).wait()

        for d in range(u):
            x_copy(g + d).wait()
        gus = [_dot(xbuf[(g + d) % nbuf].astype(F32).reshape(MOE_BLOCK, D_MODEL).astype(BF16), wgus[...])
               for d in range(u)]
        for d in range(u):
            gate = gus[d][:, :EXPERT_FF]
            mid = (gate * _sigmoid(gate) * gus[d][:, EXPERT_FF:]).astype(BF16)
            ybuf[(g + d) % nbuf] = (_dot(mid, wds[...]).reshape(MOE_BLOCK, *TOKEN_TILE)
                                    .astype(EXPERT_OUT_DTYPE))
        for d in range(u):
            y_copy(g + d).start()

    @pl.when(nb > 0)
    def _():
        wgus[:, :EXPERT_FF] = wg_ref[0].astype(BF16)
        wgus[:, EXPERT_FF:] = wu_ref[0].astype(BF16)
        wds[...] = wd_ref[0].astype(BF16)

        whole = nb == EXPERT_UNIT
        triple = (nb % 2 == 1) & (nb >= 3)
        pairs = jnp.where(whole, 0, jnp.where(triple, (nb - 3) // 2, nb // 2))

        def pair(j, carry):
            run(first + j * 2, 2)
            return carry

        lax.fori_loop(0, pairs, pair, 0)

        @pl.when(whole)
        def _():
            run(first, EXPERT_UNIT)

        @pl.when(triple)
        def _():
            run(first + nb - 3, 3)

        @pl.when(nb == 1)
        def _():
            run(first, 1)

    @pl.when(e == N_EXPERTS - 1)
    def _():
        for d in range(nbuf):
            g = n_used - nbuf + d

            @pl.when(g >= 0)
            def _():
                y_copy(g).wait()

        ybuf[0] = jnp.zeros((MOE_BLOCK,) + TOKEN_TILE, EXPERT_OUT_DTYPE)

        def z_copy(j):
            return pltpu.make_async_copy(ybuf.at[0], ys_ref.at[rows(j)], ysem.at[0])

        def z_start(j, carry):
            z_copy(j).start()
            return carry

        def z_wait(j, carry):
            z_copy(j).wait()
            return carry

        lax.fori_loop(n_used, N_BLOCKS, z_start, 0)
        lax.fori_loop(n_used, N_BLOCKS, z_wait, 0)


def _experts(pad_start, n_blk, xs, wg, wu, wd):
    wspec = lambda shape: pl.BlockSpec((1,) + shape, lambda e, ps, nb: (e, 0, 0))
    grid_spec = pltpu.PrefetchScalarGridSpec(
        num_scalar_prefetch=2,
        grid=(N_EXPERTS,),
        in_specs=[wspec((D_MODEL, EXPERT_FF)), wspec((D_MODEL, EXPERT_FF)),
                  wspec((EXPERT_FF, D_MODEL)), pl.BlockSpec(memory_space=pl.ANY)],
        out_specs=pl.BlockSpec(memory_space=pl.ANY),
        scratch_shapes=[pltpu.VMEM((EXPERT_NBUF, MOE_BLOCK) + TOKEN_TILE, xs.dtype),
                        pltpu.VMEM((EXPERT_NBUF, MOE_BLOCK) + TOKEN_TILE, EXPERT_OUT_DTYPE),
                        pltpu.VMEM((D_MODEL, 2 * EXPERT_FF), BF16),
                        pltpu.VMEM((EXPERT_FF, D_MODEL), BF16),
                        pltpu.SemaphoreType.DMA((EXPERT_NBUF,)),
                        pltpu.SemaphoreType.DMA((EXPERT_NBUF,))],
    )
    return pl.pallas_call(
        _expert_kernel,
        grid_spec=grid_spec,
        out_shape=jax.ShapeDtypeStruct((N_SLOTS,) + TOKEN_TILE, EXPERT_OUT_DTYPE),
        compiler_params=_params(("arbitrary",)),
        name="experts",
    )(pad_start, n_blk, wg, wu, wd, xs)


def _combine_kernel(dcur_ref, dnext_ref, w_ref, base_ref, g2_ref, fg_ref, ys_ref, o_ref,
                    buf, acc_ref, sem):
    tq = COMBINE_TQ
    i = pl.program_id(0)
    n = pl.num_programs(0)

    def start(dest_ref, to_slot, t):
        for kk in range(TOP_K):
            d = dest_ref[t * TOP_K + kk]
            pltpu.make_async_copy(ys_ref.at[d], buf.at[to_slot, kk, t],
                                  sem.at[to_slot]).start(priority=kk % 2)

    def reduce(slot, t):
        a = buf[slot, 0, t].astype(F32) * w_ref[t * TOP_K]
        for kk in range(1, TOP_K):
            a = a + buf[slot, kk, t].astype(F32) * w_ref[t * TOP_K + kk]
        acc_ref[t] = a

    @pl.when(i == 0)
    def _():
        lax.fori_loop(0, tq, lambda t, c: start(dcur_ref, 0, t) or c, 0, unroll=4)

    for slot in range(2):
        @pl.when(i % 2 == slot)
        def _(slot=slot):
            @pl.when(i + 1 < n)
            def _():
                lax.fori_loop(0, tq, lambda t, c: start(dnext_ref, 1 - slot, t) or c, 0, unroll=4)

            for kk in range(TOP_K):
                pltpu.make_async_copy(ys_ref.at[pl.ds(0, tq)], buf.at[slot, kk],
                                      sem.at[slot]).wait()
            lax.fori_loop(0, tq, lambda t, c: reduce(slot, t) or c, 0, unroll=8)

    xo = base_ref[...] + g2_ref[0] * acc_ref[...]
    ms = jnp.sum(jnp.sum(xo * xo, axis=2, keepdims=True), axis=1, keepdims=True) * (1.0 / D_MODEL)
    o_ref[0] = ((xo * lax.rsqrt(ms + EPS)) * fg_ref[...]).reshape(tq, D_MODEL)


def _combine(dest, w_flat, base, g2, final_g, ys):
    tq = COMBINE_TQ
    nt = SEQ // tq
    n = TOKENS // tq
    smem = lambda imap: pl.BlockSpec((tq * TOP_K,), imap, memory_space=pltpu.SMEM)
    return pl.pallas_call(
        _combine_kernel,
        grid=(n,),
        in_specs=[smem(lambda i: (i,)), smem(lambda i: (jnp.minimum(i + 1, n - 1),)),
                  smem(lambda i: (i,)),
                  pl.BlockSpec((tq,) + TOKEN_TILE, lambda i: (i, 0, 0)),
                  pl.BlockSpec((1,) + TOKEN_TILE, lambda i: (i // nt, 0, 0)),
                  pl.BlockSpec(TOKEN_TILE, lambda i: (0, 0)),
                  pl.BlockSpec(memory_space=pl.ANY)],
        out_specs=pl.BlockSpec((1, tq, D_MODEL), lambda i: (i // nt, i % nt, 0)),
        out_shape=jax.ShapeDtypeStruct((BATCH, SEQ, D_MODEL), F32),
        scratch_shapes=[pltpu.VMEM((2, TOP_K, tq) + TOKEN_TILE, ys.dtype),
                        pltpu.VMEM((tq,) + TOKEN_TILE, F32),
                        pltpu.SemaphoreType.DMA((2,))],
        compiler_params=_params(("arbitrary",)),
        name="combine",
    )(dest, dest, w_flat, base, g2.reshape((BATCH,) + TOKEN_TILE), final_g.reshape(TOKEN_TILE), ys)


def kernel(x, c, ada_w, ada_b, norm1_g, w_in, s5_lam_re, s5_lam_im, s5_log_dt, s5_b_re, s5_b_im,
           s5_c_re, s5_c_im, s5_d, s5_w_glu, s5_b_glu, w_proj_a, gla_w_gk2, gla_b_gk2, gla_norm_g,
           w_proj_b, w_out, norm2_g, router_w, router_bias, exp_w_gate, exp_w_up, exp_w_down,
           sh_w_gate, sh_w_up, sh_w_down, final_g):
    l = 0
    mod = _ada(c, ada_w[l], ada_b[l])
    sh1, sc1, g1, sh2, sc2, g2 = [m.reshape(BATCH, 1, D_MODEL) for m in jnp.split(mod, 6, axis=-1)]

    w = w_in[l]
    gk0 = _V0 + GLA_VAL
    w_main = jnp.concatenate([w[:, :gk0], w[:, gk0 + GLA_GATE_RANK:]], axis=1).astype(BF16)
    w_gk = jnp.pad(w[:, gk0:gk0 + GLA_GATE_RANK], ((0, 0), (0, LANES - GLA_GATE_RANK))).astype(BF16)
    w_gk2 = jnp.pad(gla_w_gk2[l], ((0, LANES - GLA_GATE_RANK), (0, 0))).astype(BF16)
    u_tb, q, k, v, la, rs, ga, gb = _inproj(
        x, norm1_g[l].reshape(1, D_MODEL), sc1, sh1, w_main, w_gk, w_gk2,
        gla_b_gk2[l].reshape(1, GLA_KEY))
    tables = _s5_tables(s5_lam_re[l], s5_lam_im[l], s5_log_dt[l], s5_b_re[l], s5_b_im[l],
                        s5_c_re[l], s5_c_im[l], s5_d[l])
    ya_tb = _s5(u_tb, *tables)
    yb = _gla(q, k, la, v, rs, gla_norm_g[l])

    wsgu = jnp.concatenate([sh_w_gate[l], sh_w_up[l]], axis=1).astype(BF16)
    base, hp, scores = _merge(
        ya_tb, yb, ga, gb, x, g1, sc2, sh2, g2,
        norm2_g[l].reshape(1, D_MODEL), s5_w_glu[l].astype(BF16), s5_b_glu[l].reshape(1, S5_WIDTH),
        w_proj_a[l].astype(BF16), w_proj_b[l].astype(BF16), w_out[l].astype(BF16),
        router_w[l].astype(BF16), wsgu, sh_w_down[l].astype(BF16))

    e_idx, rank, w_k, cnt = _route(scores, router_bias[l])
    counts = cnt[:, 0].astype(I32)
    padded = (counts + MOE_BLOCK - 1) // MOE_BLOCK * MOE_BLOCK
    pad_end = jnp.cumsum(padded)
    pad_start = pad_end - padded
    dest = _slots(e_idx, rank, pad_start).T.reshape(-1)
    pad_start = pad_start.astype(I32)
    n_blk = (padded // MOE_BLOCK).astype(I32)
    xs = _dispatch(pad_start, counts, n_blk, dest, hp)
    ys = _experts(pad_start, n_blk, xs, exp_w_gate[l], exp_w_up[l], exp_w_down[l])
    return _combine(dest, w_k.T.reshape(-1), base, g2, final_g, ys)
```

```python
import jax
import jax.numpy as jnp
from jax import lax
from jax.experimental import pallas as pl
from jax.experimental.pallas import tpu as pltpu

F32 = jnp.float32
BF16 = jnp.bfloat16
I32 = jnp.int32

D_MODEL = 1024
BATCH = 8
SEQ = 2048
TOKENS = BATCH * SEQ
S5_WIDTH = 512
S5_GROUP = 16
S5_GROUPS = 32
S5_STATE = 64
S5_COLS = S5_GROUPS * S5_STATE
GLA_HEADS = 4
GLA_DK = 64
GLA_DV = 128
GLA_KEY = GLA_HEADS * GLA_DK
GLA_VAL = GLA_HEADS * GLA_DV
GLA_GATE_RANK = 16
GLA_GATE_TAU = 16.0
GLA_CHUNK = 64
GLA_SUB = 16
N_EXPERTS = 256
TOP_K = 8
N_GROUPS = 8
GROUP_SIZE = N_EXPERTS // N_GROUPS
TOPK_GROUPS = 4
EXPERT_FF = 256
ROUTE_SCALE = 2.5
MOE_BLOCK = 128
EPS = 1e-6
N_SLOTS = -(-(TOKENS * TOP_K + N_EXPERTS * (MOE_BLOCK - 1)) // MOE_BLOCK) * MOE_BLOCK
N_BLOCKS = N_SLOTS // MOE_BLOCK

LANES = 128
SUBLANES = 8
TOKEN_TILE = (SUBLANES, LANES)
assert D_MODEL == SUBLANES * LANES
VMEM_LIMIT = 56 * 1024 * 1024

ADA_TN = 1536
INPROJ_TM = 512
S5_TC = 128
S5_RB = 32
S5_COL_CHUNK = 512
GLA_TG = 512
MERGE_TM = 512
MERGE_SPLIT = 2
ROUTE_TR = 512
DISPATCH_TQ = 1024
COMBINE_TQ = 256
EXPERT_UNIT = 4
EXPERT_NBUF = 10
EXPERT_OUT_DTYPE = BF16
EXP_CLAMP = 60.0


def _params(sem, vmem=VMEM_LIMIT):
    return pltpu.CompilerParams(dimension_semantics=sem, vmem_limit_bytes=vmem)


def _dot(a, b):
    return jnp.dot(a, b, preferred_element_type=F32)


def _dot_nt(a, b):
    return lax.dot_general(a, b, (((1,), (1,)), ((), ())), preferred_element_type=F32)


def _sigmoid(x):
    return jax.nn.sigmoid(x)


def _ada_kernel(c_ref, w_ref, b_ref, o_ref):
    c = c_ref[...]
    s = (c * _sigmoid(c)).astype(BF16)
    o_ref[...] = _dot(s, w_ref[...].astype(BF16)) + b_ref[...]


def _ada(c, w, b):
    n = w.shape[1]
    return pl.pallas_call(
        _ada_kernel,
        grid=(n // ADA_TN,),
        in_specs=[
            pl.BlockSpec((BATCH, D_MODEL), lambda j: (0, 0)),
            pl.BlockSpec((D_MODEL, ADA_TN), lambda j: (0, j)),
            pl.BlockSpec((1, ADA_TN), lambda j: (0, j)),
        ],
        out_specs=pl.BlockSpec((BATCH, ADA_TN), lambda j: (0, j)),
        out_shape=jax.ShapeDtypeStruct((BATCH, n), F32),
        compiler_params=_params(("arbitrary",)),
        name="ada",
    )(c, w, b.reshape(1, n))


_U0, _Q0, _K0, _V0, _R0, _GA0, _GB0, _END = 0, 512, 768, 1024, 1536, 2048, 3072, 4096


def _inproj_kernel(x_ref, g_ref, sc_ref, sh_ref, wm_ref, wgk_ref, wgk2_ref, bgk_ref,
                   u_ref, q_ref, k_ref, v_ref, la_ref, rs_ref, ga_ref, gb_ref):
    x = x_ref[0]
    r = lax.rsqrt(jnp.mean(x * x, axis=-1, keepdims=True) + EPS)
    h = (x * r) * g_ref[...] * (1.0 + sc_ref[0]) + sh_ref[0]
    hb = h.astype(BF16)

    def seg(lo, hi):
        return _dot(hb, wm_ref[:, lo:hi])

    u_ref[...] = seg(_U0, _Q0)
    q_ref[0] = seg(_Q0, _K0) * (GLA_DK ** -0.5)
    k_ref[0] = seg(_K0, _V0)
    v_ref[0] = seg(_V0, _R0)
    rr = seg(_R0, _GA0)
    rs_ref[0] = rr * _sigmoid(rr)
    ga_ref[0] = _sigmoid(seg(_GA0, _GB0))
    gb_ref[0] = _sigmoid(seg(_GB0, _END))
    gk = _dot(hb, wgk_ref[...])
    z = _dot(gk.astype(BF16), wgk2_ref[...]) + bgk_ref[...]
    la_ref[0] = -(jnp.maximum(-z, 0.0) + jnp.log1p(jnp.exp(-jnp.abs(z)))) * (1.0 / GLA_GATE_TAU)


def _inproj(x, g, sc, sh, w_main, w_gk, w_gk2, b_gk2):
    tm = INPROJ_TM
    nt = SEQ // tm
    row3 = lambda w: pl.BlockSpec((1, tm, w), lambda b, i: (b, i, 0))
    const = lambda shape: pl.BlockSpec(shape, lambda b, i: tuple(0 for _ in shape))
    mod = pl.BlockSpec((1, 1, D_MODEL), lambda b, i: (b, 0, 0))
    bld = lambda w: jax.ShapeDtypeStruct((BATCH, SEQ, w), F32)
    return pl.pallas_call(
        _inproj_kernel,
        grid=(BATCH, nt),
        in_specs=[row3(D_MODEL), const((1, D_MODEL)), mod, mod,
                  const((D_MODEL, _END)), const((D_MODEL, LANES)), const((LANES, GLA_KEY)),
                  const((1, GLA_KEY))],
        out_specs=[pl.BlockSpec((tm, S5_WIDTH), lambda b, i: (i, b)),
                   row3(GLA_KEY), row3(GLA_KEY), row3(GLA_VAL), row3(GLA_KEY), row3(GLA_VAL),
                   row3(D_MODEL), row3(D_MODEL)],
        out_shape=[jax.ShapeDtypeStruct((SEQ, BATCH * S5_WIDTH), F32),
                   bld(GLA_KEY), bld(GLA_KEY), bld(GLA_VAL), bld(GLA_KEY), bld(GLA_VAL),
                   bld(D_MODEL), bld(D_MODEL)],
        compiler_params=_params(("parallel", "parallel")),
        name="inproj",
    )(x, g, sc, sh, w_main, w_gk, w_gk2, b_gk2)


def _s5_kernel(u_ref, bre_ref, bim_ref, cre_ref, cim_ref, are_ref, aim_ref, d_ref,
               y_ref, s_ref, st_ref):
    half = S5_COLS // 2

    @pl.when(pl.program_id(0) == 0)
    def _():
        st_ref[...] = jnp.zeros_like(st_ref)

    u = u_ref[...].reshape(S5_TC, BATCH, S5_WIDTH).reshape(S5_TC * BATCH, S5_WIDTH)
    ub = u.astype(BF16)
    nblk = S5_TC // S5_RB
    brows = [slice(r * S5_RB * BATCH, (r + 1) * S5_RB * BATCH) for r in range(nblk)]
    for r in range(nblk):
        for j in range(2):
            uj = ub[brows[r], j * 256:(j + 1) * 256]
            s_ref[brows[r], j * half:(j + 1) * half] = _dot(uj, bre_ref[j])
            s_ref[brows[r], S5_COLS + j * half:S5_COLS + (j + 1) * half] = _dot(uj, bim_ref[j])

    ncc = S5_COLS // S5_COL_CHUNK
    re_sl = [slice(cc * S5_COL_CHUNK, (cc + 1) * S5_COL_CHUNK) for cc in range(ncc)]
    im_sl = [slice(S5_COLS + cc * S5_COL_CHUNK, S5_COLS + (cc + 1) * S5_COL_CHUNK) for cc in range(ncc)]
    sr = [st_ref[:, sl] for sl in re_sl]
    si = [st_ref[:, sl] for sl in im_sl]
    for r in range(nblk):
        for t in range(r * S5_RB, (r + 1) * S5_RB):
            rows = slice(t * BATCH, (t + 1) * BATCH)
            for cc in range(ncc):
                ar = are_ref[:, re_sl[cc]]
                ai = aim_ref[:, re_sl[cc]]
                nr = ar * sr[cc] - ai * si[cc] + s_ref[rows, re_sl[cc]]
                ni = ar * si[cc] + ai * sr[cc] + s_ref[rows, im_sl[cc]]
                s_ref[rows, re_sl[cc]] = nr
                s_ref[rows, im_sl[cc]] = ni
                sr[cc], si[cc] = nr, ni
    for cc in range(ncc):
        st_ref[:, re_sl[cc]] = sr[cc]
        st_ref[:, im_sl[cc]] = si[cc]

    for r in range(nblk):
        ys = []
        for j in range(2):
            sre = s_ref[brows[r], j * half:(j + 1) * half].astype(BF16)
            sim = s_ref[brows[r], S5_COLS + j * half:S5_COLS + (j + 1) * half].astype(BF16)
            ys.append(_dot(sre, cre_ref[j]) + _dot(sim, cim_ref[j]))
        y = jnp.concatenate(ys, axis=1) + d_ref[...] * u[brows[r], :]
        y_ref[r * S5_RB:(r + 1) * S5_RB, :] = (
            jax.nn.gelu(y).reshape(S5_RB, BATCH, S5_WIDTH).reshape(S5_RB, BATCH * S5_WIDTH))


def _s5(u_tb, bre, bim, cre, cimn, are, aim, dflat):
    rows = S5_TC * BATCH
    const = lambda shape: pl.BlockSpec(shape, lambda i: tuple(0 for _ in shape))
    return pl.pallas_call(
        _s5_kernel,
        grid=(SEQ // S5_TC,),
        in_specs=[pl.BlockSpec((S5_TC, BATCH * S5_WIDTH), lambda i: (i, 0)),
                  const(bre.shape), const(bim.shape), const(cre.shape), const(cimn.shape),
                  const(are.shape), const(aim.shape), const(dflat.shape)],
        out_specs=pl.BlockSpec((S5_TC, BATCH * S5_WIDTH), lambda i: (i, 0)),
        out_shape=jax.ShapeDtypeStruct((SEQ, BATCH * S5_WIDTH), F32),
        scratch_shapes=[pltpu.VMEM((rows, 2 * S5_COLS), F32),
                        pltpu.VMEM((BATCH, 2 * S5_COLS), F32)],
        compiler_params=_params(("arbitrary",)),
        name="s5",
    )(u_tb, bre, bim, cre, cimn, are, aim, dflat)


def _s5_tables(lam_re, lam_im, log_dt, b_re, b_im, c_re, c_im, d_skip):
    lr = lam_re.astype(F32)
    li = lam_im.astype(F32)
    dt = jnp.exp(log_dt.astype(F32))[:, None]
    mag = jnp.exp(lr * dt)
    abar_re = mag * jnp.cos(li * dt)
    abar_im = mag * jnp.sin(li * dt)
    den = lr * lr + li * li
    num_re = abar_re - 1.0
    coef_re = (num_re * lr + abar_im * li) / den
    coef_im = (abar_im * lr - num_re * li) / den
    cr, ci = coef_re[..., None], coef_im[..., None]
    br, bi = b_re.astype(F32), b_im.astype(F32)
    bbar_re = cr * br - ci * bi
    bbar_im = cr * bi + ci * br
    eye = jnp.eye(S5_GROUPS // 2, dtype=F32)

    def in_map(bb):
        bb = bb.reshape(2, S5_GROUPS // 2, S5_STATE, S5_GROUP)
        return jnp.einsum('jgph,gk->jghkp', bb, eye).reshape(2, 256, S5_COLS // 2).astype(BF16)

    def out_map(cc):
        cc = cc.reshape(2, S5_GROUPS // 2, S5_GROUP, S5_STATE)
        return jnp.einsum('jghp,gk->jgpkh', cc, eye).reshape(2, S5_COLS // 2, 256).astype(BF16)

    are = jnp.broadcast_to(abar_re.reshape(1, S5_COLS), (BATCH, S5_COLS))
    aim = jnp.broadcast_to(abar_im.reshape(1, S5_COLS), (BATCH, S5_COLS))
    return (in_map(bbar_re), in_map(bbar_im), out_map(c_re.astype(F32)), out_map(-c_im.astype(F32)),
            are, aim, d_skip.astype(F32).reshape(1, S5_WIDTH))


def _gla_kernel(q_ref, k_ref, la_ref, v_ref, rs_ref, ng_ref, tt_ref, o_ref, st_ref):
    @pl.when(pl.program_id(1) == 0)
    def _():
        st_ref[...] = jnp.zeros_like(st_ref)

    c = GLA_CHUNK
    tt = tt_ref[...]
    row = lax.broadcasted_iota(I32, (c, GLA_KEY), 0)
    ri = lax.broadcasted_iota(I32, (c, c), 0)
    ci = lax.broadcasted_iota(I32, (c, c), 1)
    causal = ci <= ri
    nsub = c // GLA_SUB

    heads = range(GLA_HEADS)
    hsl = [slice(h * GLA_DK, (h + 1) * GLA_DK) for h in heads]
    vsl = [slice(h * GLA_DV, (h + 1) * GLA_DV) for h in heads]
    chunks = range(GLA_TG // c)
    intra, kv, qdec, dec = [], [], [], []

    sls = [slice(ch * c, (ch + 1) * c) for ch in chunks]
    csum = []
    for ch in chunks:
        g = la_ref[0, sls[ch], :]
        g1 = g.astype(BF16)
        r1 = g - g1.astype(F32)
        g2 = r1.astype(BF16)
        g3 = (r1 - g2.astype(F32)).astype(BF16)
        csum.append(_dot(tt, g1) + _dot(tt, g2) + _dot(tt, g3))
    qbigs, kbigs, kdec = [], [], []
    for ch in chunks:
        sl = sls[ch]
        cs = csum[ch]
        b = cs[0:c]
        cl = cs[c:2 * c]
        ref_pt = b - cl
        q = q_ref[0, sl, :]
        k = k_ref[0, sl, :]
        qt = q * jnp.exp(cl)
        qe = (q * jnp.exp(b)).astype(BF16)
        blast = b[c - 1:c, :]
        ks = (k * jnp.exp(blast - b)).astype(BF16)
        k_sub = []
        q_sub = []
        for s in range(nsub):
            rs_ = ref_pt[s * GLA_SUB:s * GLA_SUB + 1, :]
            k_sub.append(k * jnp.exp(jnp.minimum(rs_ - b, EXP_CLAMP)))
            q_sub.append(jnp.where((row >= s * GLA_SUB) & (row < (s + 1) * GLA_SUB), qt, 0.0))
        qbigs.append([jnp.concatenate([x[:, hsl[h]] for x in q_sub], axis=1).astype(BF16)
                      for h in heads])
        kbigs.append([jnp.concatenate([x[:, hsl[h]] for x in k_sub], axis=1).astype(BF16)
                      for h in heads])
        kdec.append(ks)
        qdec.append(qe)
        dec.append(jnp.exp(blast))
    scores = [[_dot_nt(qbigs[ch][h], kbigs[ch][h]) for h in heads] for ch in chunks]
    for ch in chunks:
        kv.append([_dot(v_ref[0, sls[ch], vsl[h]].T.astype(BF16), kdec[ch][:, hsl[h]])
                   for h in heads])
    for ch in chunks:
        intra.append([_dot(jnp.where(causal, scores[ch][h], 0.0).astype(BF16),
                           v_ref[0, sls[ch], vsl[h]].astype(BF16)) for h in heads])

    st = [st_ref[h] for h in heads]
    for ch in chunks:
        sl = slice(ch * c, (ch + 1) * c)
        inter = [_dot_nt(qdec[ch][:, hsl[h]], st[h].astype(BF16)) for h in heads]
        for h in heads:
            o = inter[h] + intra[ch][h]
            st[h] = st[h] * dec[ch][:, hsl[h]] + kv[ch][h]
            on = o * lax.rsqrt(jnp.mean(o * o, axis=-1, keepdims=True) + EPS) * ng_ref[...]
            o_ref[0, sl, vsl[h]] = on * rs_ref[0, sl, vsl[h]]
    for h in heads:
        st_ref[h] = st[h]


def _gla(q, k, la, v, rs, norm_g):
    tg = GLA_TG
    c = GLA_CHUNK
    r = jnp.arange(c)
    tri = (r[None, :] <= r[:, None])
    blk = tri & ((r[None, :] // GLA_SUB) == (r[:, None] // GLA_SUB))
    tt = jnp.concatenate([tri, blk], axis=0).astype(BF16)
    row3 = lambda w: pl.BlockSpec((1, tg, w), lambda b, i: (b, i, 0))
    const = lambda shape: pl.BlockSpec(shape, lambda b, i: tuple(0 for _ in shape))
    return pl.pallas_call(
        _gla_kernel,
        grid=(BATCH, SEQ // tg),
        in_specs=[row3(GLA_KEY), row3(GLA_KEY), row3(GLA_KEY), row3(GLA_VAL), row3(GLA_VAL),
                  const((1, GLA_DV)), const((2 * c, c))],
        out_specs=row3(GLA_VAL),
        out_shape=jax.ShapeDtypeStruct((BATCH, SEQ, GLA_VAL), F32),
        scratch_shapes=[pltpu.VMEM((GLA_HEADS, GLA_DV, GLA_DK), F32)],
        compiler_params=_params(("parallel", "arbitrary")),
        name="gla",
    )(q, k, la, v, rs, norm_g.reshape(1, GLA_DV), tt)


def _merge_kernel(ya_ref, yb_ref, ga_ref, gb_ref, x_ref, g1_ref, sc2_ref, sh2_ref, g2_ref, n2_ref,
                  wglu_ref, bglu_ref, wpa_ref, wpb_ref, wout_ref, wr_ref, wsgu_ref, wsd_ref,
                  base_ref, hp_ref, sc_ref):
    rows = MERGE_TM // MERGE_SPLIT
    subs = [slice(s * rows, (s + 1) * rows) for s in range(MERGE_SPLIT)]
    ya = [ya_ref[sl, :] for sl in subs]
    z = [_dot(a.astype(BF16), wglu_ref[...]) for a in ya]
    pb = [_dot(yb_ref[0, sl, :].astype(BF16), wpb_ref[...]) for sl in subs]
    ya2 = [(a * _sigmoid(zz + bglu_ref[...])).astype(BF16) for a, zz in zip(ya, z)]
    pa = [_dot(a, wpa_ref[...]) for a in ya2]
    mixed = [(ga_ref[0, sl, :] * p + gb_ref[0, sl, :] * q).astype(BF16)
             for sl, p, q in zip(subs, pa, pb)]
    mo = [_dot(m, wout_ref[...]) for m in mixed]
    x1 = [x_ref[0, sl, :] + g1_ref[0] * m for sl, m in zip(subs, mo)]
    h = [(x * lax.rsqrt(jnp.mean(x * x, axis=-1, keepdims=True) + EPS)) * n2_ref[...]
         * (1.0 + sc2_ref[0]) + sh2_ref[0] for x in x1]
    hb = [v.astype(BF16) for v in h]
    logits = [_dot(v, wr_ref[...]) for v in hb]
    gu = [_dot(v, wsgu_ref[...]) for v in hb]
    mid = [(g[:, :EXPERT_FF] * _sigmoid(g[:, :EXPERT_FF]) * g[:, EXPERT_FF:]).astype(BF16) for g in gu]
    shared = [_dot(m, wsd_ref[...]) for m in mid]
    for s, sl in enumerate(subs):
        sc_ref[sl, :] = _sigmoid(logits[s])
        base_ref[sl] = (x1[s] + g2_ref[0] * shared[s]).reshape(rows, *TOKEN_TILE)
        hp_ref[sl] = hb[s].reshape(rows, *TOKEN_TILE)


def _merge(ya_tb, yb, ga, gb, x, g1, sc2, sh2, g2, n2, wglu, bglu, wpa, wpb, wout, wr, wsgu, wsd):
    tm = MERGE_TM
    row3 = lambda w: pl.BlockSpec((1, tm, w), lambda b, i: (b, i, 0))
    const = lambda a: pl.BlockSpec(a.shape, lambda b, i: tuple(0 for _ in a.shape))
    mod = pl.BlockSpec((1, 1, D_MODEL), lambda b, i: (b, 0, 0))
    nt = SEQ // tm
    flat = lambda w: pl.BlockSpec((tm, w), lambda b, i: (b * nt + i, 0))
    tiles = pl.BlockSpec((tm,) + TOKEN_TILE, lambda b, i: (b * nt + i, 0, 0))
    return pl.pallas_call(
        _merge_kernel,
        grid=(BATCH, nt),
        in_specs=[pl.BlockSpec((tm, S5_WIDTH), lambda b, i: (i, b)),
                  row3(GLA_VAL), row3(D_MODEL), row3(D_MODEL), row3(D_MODEL),
                  mod, mod, mod, mod, const(n2),
                  const(wglu), const(bglu), const(wpa), const(wpb), const(wout), const(wr),
                  const(wsgu), const(wsd)],
        out_specs=[tiles, tiles, flat(N_EXPERTS)],
        out_shape=[jax.ShapeDtypeStruct((TOKENS,) + TOKEN_TILE, F32),
                   jax.ShapeDtypeStruct((TOKENS,) + TOKEN_TILE, BF16),
                   jax.ShapeDtypeStruct((TOKENS, N_EXPERTS), F32)],
        compiler_params=_params(("parallel", "parallel")),
        name="merge",
    )(ya_tb, yb, ga, gb, x, g1, sc2, sh2, g2, n2, wglu, bglu, wpa, wpb, wout, wr, wsgu, wsd)


def _first_argmax(x, iota, size):
    m = jnp.max(x, axis=0, keepdims=True)
    idx = jnp.min(jnp.where(x == m, iota, size), axis=0, keepdims=True)
    return m, idx


def _route_kernel(s_ref, bias_ref, tri_ref, e_ref, r_ref, w_ref, cnt_ref, carry_ref):
    @pl.when(pl.program_id(0) == 0)
    def _():
        carry_ref[...] = jnp.zeros_like(carry_ref)

    tr = ROUTE_TR
    neg = -jnp.inf
    s_t = s_ref[...].T
    biased = s_t + bias_ref[...]
    io_g = lax.broadcasted_iota(I32, (GROUP_SIZE, tr), 0)
    rows = []
    for g in range(N_GROUPS):
        xg = biased[g * GROUP_SIZE:(g + 1) * GROUP_SIZE, :]
        m1, i1 = _first_argmax(xg, io_g, GROUP_SIZE)
        m2 = jnp.max(jnp.where(io_g == i1, neg, xg), axis=0, keepdims=True)
        rows.append(m1 + m2)
    gs = jnp.concatenate(rows, axis=0)
    io_n = lax.broadcasted_iota(I32, (N_GROUPS, tr), 0)
    gsel = jnp.zeros((N_GROUPS, tr), F32)
    for _ in range(TOPK_GROUPS):
        _, gi = _first_argmax(gs, io_n, N_GROUPS)
        hit = io_n == gi
        gsel = jnp.where(hit, 1.0, gsel)
        gs = jnp.where(hit, neg, gs)
    masked = jnp.concatenate(
        [jnp.where(gsel[g:g + 1, :] > 0.0, biased[g * GROUP_SIZE:(g + 1) * GROUP_SIZE, :], neg)
         for g in range(N_GROUPS)], axis=0)
    io_e = lax.broadcasted_iota(I32, (N_EXPERTS, tr), 0)
    sel = jnp.zeros((N_EXPERTS, tr), F32)
    idxs = []
    for _ in range(TOP_K):
        _, ei = _first_argmax(masked, io_e, N_EXPERTS)
        hit = io_e == ei
        sel = jnp.where(hit, 1.0, sel)
        masked = jnp.where(hit, neg, masked)
        idxs.append(ei)
    den = jnp.sum(sel * s_t, axis=0, keepdims=True)
    rank = _dot(sel.astype(BF16), tri_ref[...]) + carry_ref[...]
    rks, sks = [], []
    for kk in range(TOP_K):
        hit = io_e == idxs[kk]
        rks.append(jnp.sum(jnp.where(hit, rank, 0.0), axis=0, keepdims=True))
        sks.append(jnp.sum(jnp.where(hit, s_t, 0.0), axis=0, keepdims=True))
    e_ref[...] = jnp.concatenate(idxs, axis=0)
    r_ref[...] = jnp.concatenate(rks, axis=0).astype(I32)
    w_ref[...] = jnp.concatenate(sks, axis=0) / den * ROUTE_SCALE
    carry_ref[...] += jnp.sum(sel, axis=1, keepdims=True)
    cnt_ref[...] = carry_ref[...]


def _route(scores, bias):
    tr = ROUTE_TR
    r = jnp.arange(tr)
    tri = (r[:, None] < r[None, :]).astype(BF16)
    kt = lambda dt: jax.ShapeDtypeStruct((TOP_K, TOKENS), dt)
    blk = pl.BlockSpec((TOP_K, tr), lambda i: (0, i))
    return pl.pallas_call(
        _route_kernel,
        grid=(TOKENS // tr,),
        in_specs=[pl.BlockSpec((tr, N_EXPERTS), lambda i: (i, 0)),
                  pl.BlockSpec((N_EXPERTS, 1), lambda i: (0, 0)),
                  pl.BlockSpec((tr, tr), lambda i: (0, 0))],
        out_specs=[blk, blk, blk, pl.BlockSpec((N_EXPERTS, 1), lambda i: (0, 0))],
        out_shape=[kt(I32), kt(I32), kt(F32), jax.ShapeDtypeStruct((N_EXPERTS, 1), F32)],
        scratch_shapes=[pltpu.VMEM((N_EXPERTS, 1), F32)],
        compiler_params=_params(("arbitrary",)),
        name="route",
    )(scores, bias.reshape(N_EXPERTS, 1), tri)


def _slots_kernel(e_ref, r_ref, ps_ref, d_ref):
    tr = ROUTE_TR
    io_e = lax.broadcasted_iota(I32, (N_EXPERTS, tr), 0)
    ps = ps_ref[...]
    rows = []
    for kk in range(TOP_K):
        hit = io_e == e_ref[kk:kk + 1, :]
        rows.append(jnp.sum(jnp.where(hit, ps, 0.0), axis=0, keepdims=True))
    d_ref[...] = jnp.concatenate(rows, axis=0).astype(I32) + r_ref[...]


def _slots(e_idx, rank, pad_start):
    tr = ROUTE_TR
    blk = pl.BlockSpec((TOP_K, tr), lambda i: (0, i))
    return pl.pallas_call(
        _slots_kernel,
        grid=(TOKENS // tr,),
        in_specs=[blk, blk, pl.BlockSpec((N_EXPERTS, 1), lambda i: (0, 0))],
        out_specs=blk,
        out_shape=jax.ShapeDtypeStruct((TOP_K, TOKENS), I32),
        compiler_params=_params(("parallel",)),
        name="slots",
    )(e_idx, rank, pad_start.astype(F32).reshape(N_EXPERTS, 1))


def _dispatch_kernel(ps_ref, cnt_ref, nb_ref, dest_ref, h_ref, xs_ref, zbuf, sem, zsem):
    tq = DISPATCH_TQ
    step = pl.program_id(0)
    n_used = ps_ref[N_EXPERTS - 1] // MOE_BLOCK + nb_ref[N_EXPERTS - 1]

    def zero_fill(act):
        def block(j, carry):
            r = pl.ds(pl.multiple_of(j * MOE_BLOCK, MOE_BLOCK), MOE_BLOCK)
            act(pltpu.make_async_copy(zbuf, xs_ref.at[r], zsem))
            return carry

        def expert(e, carry):
            @pl.when(cnt_ref[e] < nb_ref[e] * MOE_BLOCK)
            def _():
                block(ps_ref[e] // MOE_BLOCK + nb_ref[e] - 1, 0)
            return carry

        lax.fori_loop(0, N_EXPERTS, expert, 0)
        lax.fori_loop(n_used, N_BLOCKS, block, 0)

    @pl.when(step == 0)
    def _():
        zbuf[...] = jnp.zeros_like(zbuf)
        zero_fill(lambda cp: cp.start())
        zero_fill(lambda cp: cp.wait())

    def start(i, carry):
        for kk in range(TOP_K):
            d = dest_ref[i * TOP_K + kk]
            pltpu.make_async_copy(h_ref.at[i], xs_ref.at[d],
                                  sem).start(priority=kk % 2)
        return carry

    lax.fori_loop(0, tq, start, 0, unroll=2)
    for _ in range(TOP_K):
        pltpu.make_async_copy(h_ref, xs_ref.at[pl.ds(0, tq)], sem).wait()


def _dispatch(pad_start, counts, n_blk, dest, hp):
    tq = DISPATCH_TQ
    grid_spec = pltpu.PrefetchScalarGridSpec(
        num_scalar_prefetch=3,
        grid=(TOKENS // tq,),
        in_specs=[pl.BlockSpec((tq * TOP_K,), lambda i, *_: (i,), memory_space=pltpu.SMEM),
                  pl.BlockSpec((tq,) + TOKEN_TILE, lambda i, *_: (i, 0, 0))],
        out_specs=pl.BlockSpec(memory_space=pl.ANY),
        scratch_shapes=[pltpu.VMEM((MOE_BLOCK,) + TOKEN_TILE, hp.dtype),
                        pltpu.SemaphoreType.DMA, pltpu.SemaphoreType.DMA],
    )
    return pl.pallas_call(
        _dispatch_kernel,
        grid_spec=grid_spec,
        out_shape=jax.ShapeDtypeStruct((N_SLOTS,) + TOKEN_TILE, hp.dtype),
        compiler_params=_params(("arbitrary",)),
        name="dispatch",
    )(pad_start, counts, n_blk, dest, hp)


def _expert_kernel(ps_ref, nb_ref, wg_ref, wu_ref, wd_ref, xs_ref, ys_ref,
                   xbuf, ybuf, wgus, wds, xsem, ysem):
    e = pl.program_id(0)
    nb = nb_ref[e]
    first = ps_ref[e] // MOE_BLOCK
    n_used = ps_ref[N_EXPERTS - 1] // MOE_BLOCK + nb_ref[N_EXPERTS - 1]
    nbuf = EXPERT_NBUF

    def rows(g):
        return pl.ds(pl.multiple_of(g * MOE_BLOCK, MOE_BLOCK), MOE_BLOCK)

    def x_copy(g):
        slot = g % nbuf
        return pltpu.make_async_copy(xs_ref.at[rows(g)], xbuf.at[slot], xsem.at[slot])

    def y_copy(g):
        slot = g % nbuf
        return pltpu.make_async_copy(ybuf.at[slot], ys_ref.at[rows(g)], ysem.at[slot])

    look = nbuf - EXPERT_UNIT

    @pl.when(e == 0)
    def _():
        for g in range(look):
            @pl.when(g < n_used)
            def _():
                x_copy(g).start(priority=1)

    def run(g, u):
        for d in range(u):
            @pl.when(g + look + d < n_used)
            def _():
                x_copy(g + look + d).start(priority=1)

        for d in range(u):
            b = g + d

            @pl.when(b >= nbuf)
            def _():
                y_copy(b - nbuf).wait()

        for d in range(u):
            x_copy(g + d).wait()
        gus = [_dot(xbuf[(g + d) % nbuf].astype(F32).reshape(MOE_BLOCK, D_MODEL).astype(BF16), wgus[...])
               for d in range(u)]
        for d in range(u):
            gate = gus[d][:, :EXPERT_FF]
            mid = (gate * _sigmoid(gate) * gus[d][:, EXPERT_FF:]).astype(BF16)
            ybuf[(g + d) % nbuf] = (_dot(mid, wds[...]).reshape(MOE_BLOCK, *TOKEN_TILE)
                                    .astype(EXPERT_OUT_DTYPE))
        for d in range(u):
            y_copy(g + d).start(priority=1)

    @pl.when(nb > 0)
    def _():
        wgus[:, :EXPERT_FF] = wg_ref[0].astype(BF16)
        wgus[:, EXPERT_FF:] = wu_ref[0].astype(BF16)
        wds[...] = wd_ref[0].astype(BF16)

        whole = nb == EXPERT_UNIT
        triple = (nb % 2 == 1) & (nb >= 3)
        pairs = jnp.where(whole, 0, jnp.where(triple, (nb - 3) // 2, nb // 2))

        def pair(j, carry):
            run(first + j * 2, 2)
            return carry

        lax.fori_loop(0, pairs, pair, 0)

        @pl.when(whole)
        def _():
            run(first, EXPERT_UNIT)

        @pl.when(triple)
        def _():
            run(first + nb - 3, 3)

        @pl.when(nb == 1)
        def _():
            run(first, 1)

    @pl.when(e == N_EXPERTS - 1)
    def _():
        for d in range(nbuf):
            g = n_used - nbuf + d

            @pl.when(g >= 0)
            def _():
                y_copy(g).wait()

        ybuf[0] = jnp.zeros((MOE_BLOCK,) + TOKEN_TILE, EXPERT_OUT_DTYPE)

        def z_copy(j):
            return pltpu.make_async_copy(ybuf.at[0], ys_ref.at[rows(j)], ysem.at[0])

        def z_start(j, carry):
            z_copy(j).start()
            return carry

        def z_wait(j, carry):
            z_copy(j).wait()
            return carry

        lax.fori_loop(n_used, N_BLOCKS, z_start, 0)
        lax.fori_loop(n_used, N_BLOCKS, z_wait, 0)


def _experts(pad_start, n_blk, xs, wg, wu, wd):
    wspec = lambda shape: pl.BlockSpec((1,) + shape, lambda e, ps, nb: (e, 0, 0))
    grid_spec = pltpu.PrefetchScalarGridSpec(
        num_scalar_prefetch=2,
        grid=(N_EXPERTS,),
        in_specs=[wspec((D_MODEL, EXPERT_FF)), wspec((D_MODEL, EXPERT_FF)),
                  wspec((EXPERT_FF, D_MODEL)), pl.BlockSpec(memory_space=pl.ANY)],
        out_specs=pl.BlockSpec(memory_space=pl.ANY),
        scratch_shapes=[pltpu.VMEM((EXPERT_NBUF, MOE_BLOCK) + TOKEN_TILE, xs.dtype),
                        pltpu.VMEM((EXPERT_NBUF, MOE_BLOCK) + TOKEN_TILE, EXPERT_OUT_DTYPE),
                        pltpu.VMEM((D_MODEL, 2 * EXPERT_FF), BF16),
                        pltpu.VMEM((EXPERT_FF, D_MODEL), BF16),
                        pltpu.SemaphoreType.DMA((EXPERT_NBUF,)),
                        pltpu.SemaphoreType.DMA((EXPERT_NBUF,))],
    )
    return pl.pallas_call(
        _expert_kernel,
        grid_spec=grid_spec,
        out_shape=jax.ShapeDtypeStruct((N_SLOTS,) + TOKEN_TILE, EXPERT_OUT_DTYPE),
        compiler_params=_params(("arbitrary",)),
        name="experts",
    )(pad_start, n_blk, wg, wu, wd, xs)


def _combine_kernel(dcur_ref, dnext_ref, w_ref, base_ref, g2_ref, fg_ref, ys_ref, o_ref,
                    buf, acc_ref, sem):
    tq = COMBINE_TQ
    i = pl.program_id(0)
    n = pl.num_programs(0)

    def start(dest_ref, to_slot, t):
        for kk in range(TOP_K):
            d = dest_ref[t * TOP_K + kk]
            pltpu.make_async_copy(ys_ref.at[d], buf.at[to_slot, kk, t],
                                  sem.at[to_slot]).start(priority=kk % 2)

    def reduce(slot, t):
        a = buf[slot, 0, t].astype(F32) * w_ref[t * TOP_K]
        for kk in range(1, TOP_K):
            a = a + buf[slot, kk, t].astype(F32) * w_ref[t * TOP_K + kk]
        acc_ref[t] = a

    @pl.when(i == 0)
    def _():
        lax.fori_loop(0, tq, lambda t, c: start(dcur_ref, 0, t) or c, 0, unroll=4)

    for slot in range(2):
        @pl.when(i % 2 == slot)
        def _(slot=slot):
            @pl.when(i + 1 < n)
            def _():
                lax.fori_loop(0, tq, lambda t, c: start(dnext_ref, 1 - slot, t) or c, 0, unroll=4)

            for kk in range(TOP_K):
                pltpu.make_async_copy(ys_ref.at[pl.ds(0, tq)], buf.at[slot, kk],
                                      sem.at[slot]).wait()
            lax.fori_loop(0, tq, lambda t, c: reduce(slot, t) or c, 0, unroll=8)

    xo = base_ref[...] + g2_ref[0] * acc_ref[...]
    ms = jnp.sum(jnp.sum(xo * xo, axis=2, keepdims=True), axis=1, keepdims=True) * (1.0 / D_MODEL)
    o_ref[0] = ((xo * lax.rsqrt(ms + EPS)) * fg_ref[...]).reshape(tq, D_MODEL)


def _combine(dest, w_flat, base, g2, final_g, ys):
    tq = COMBINE_TQ
    nt = SEQ // tq
    n = TOKENS // tq
    smem = lambda imap: pl.BlockSpec((tq * TOP_K,), imap, memory_space=pltpu.SMEM)
    return pl.pallas_call(
        _combine_kernel,
        grid=(n,),
        in_specs=[smem(lambda i: (i,)), smem(lambda i: (jnp.minimum(i + 1, n - 1),)),
                  smem(lambda i: (i,)),
                  pl.BlockSpec((tq,) + TOKEN_TILE, lambda i: (i, 0, 0)),
                  pl.BlockSpec((1,) + TOKEN_TILE, lambda i: (i // nt, 0, 0)),
                  pl.BlockSpec(TOKEN_TILE, lambda i: (0, 0)),
                  pl.BlockSpec(memory_space=pl.ANY)],
        out_specs=pl.BlockSpec((1, tq, D_MODEL), lambda i: (i // nt, i % nt, 0)),
        out_shape=jax.ShapeDtypeStruct((BATCH, SEQ, D_MODEL), F32),
        scratch_shapes=[pltpu.VMEM((2, TOP_K, tq) + TOKEN_TILE, ys.dtype),
                        pltpu.VMEM((tq,) + TOKEN_TILE, F32),
                        pltpu.SemaphoreType.DMA((2,))],
        compiler_params=_params(("arbitrary",)),
        name="combine",
    )(dest, dest, w_flat, base, g2.reshape((BATCH,) + TOKEN_TILE), final_g.reshape(TOKEN_TILE), ys)


def kernel(x, c, ada_w, ada_b, norm1_g, w_in, s5_lam_re, s5_lam_im, s5_log_dt, s5_b_re, s5_b_im,
           s5_c_re, s5_c_im, s5_d, s5_w_glu, s5_b_glu, w_proj_a, gla_w_gk2, gla_b_gk2, gla_norm_g,
           w_proj_b, w_out, norm2_g, router_w, router_bias, exp_w_gate, exp_w_up, exp_w_down,
           sh_w_gate, sh_w_up, sh_w_down, final_g):
    l = 0
    mod = _ada(c, ada_w[l], ada_b[l])
    sh1, sc1, g1, sh2, sc2, g2 = [m.reshape(BATCH, 1, D_MODEL) for m in jnp.split(mod, 6, axis=-1)]

    w = w_in[l]
    gk0 = _V0 + GLA_VAL
    w_main = jnp.concatenate([w[:, :gk0], w[:, gk0 + GLA_GATE_RANK:]], axis=1).astype(BF16)
    w_gk = jnp.pad(w[:, gk0:gk0 + GLA_GATE_RANK], ((0, 0), (0, LANES - GLA_GATE_RANK))).astype(BF16)
    w_gk2 = jnp.pad(gla_w_gk2[l], ((0, LANES - GLA_GATE_RANK), (0, 0))).astype(BF16)
    u_tb, q, k, v, la, rs, ga, gb = _inproj(
        x, norm1_g[l].reshape(1, D_MODEL), sc1, sh1, w_main, w_gk, w_gk2,
        gla_b_gk2[l].reshape(1, GLA_KEY))
    tables = _s5_tables(s5_lam_re[l], s5_lam_im[l], s5_log_dt[l], s5_b_re[l], s5_b_im[l],
                        s5_c_re[l], s5_c_im[l], s5_d[l])
    ya_tb = _s5(u_tb, *tables)
    yb = _gla(q, k, la, v, rs, gla_norm_g[l])

    wsgu = jnp.concatenate([sh_w_gate[l], sh_w_up[l]], axis=1).astype(BF16)
    base, hp, scores = _merge(
        ya_tb, yb, ga, gb, x, g1, sc2, sh2, g2,
        norm2_g[l].reshape(1, D_MODEL), s5_w_glu[l].astype(BF16), s5_b_glu[l].reshape(1, S5_WIDTH),
        w_proj_a[l].astype(BF16), w_proj_b[l].astype(BF16), w_out[l].astype(BF16),
        router_w[l].astype(BF16), wsgu, sh_w_down[l].astype(BF16))

    e_idx, rank, w_k, cnt = _route(scores, router_bias[l])
    counts = cnt[:, 0].astype(I32)
    padded = (counts + MOE_BLOCK - 1) // MOE_BLOCK * MOE_BLOCK
    pad_end = jnp.cumsum(padded)
    pad_start = pad_end - padded
    dest = _slots(e_idx, rank, pad_start).T.reshape(-1)
    pad_start = pad_start.astype(I32)
    n_blk = (padded // MOE_BLOCK).astype(I32)
    xs = _dispatch(pad_start, counts, n_blk, dest, hp)
    ys = _experts(pad_start, n_blk, xs, exp_w_gate[l], exp_w_up[l], exp_w_down[l])
    return _combine(dest, w_k.T.reshape(-1), base, g2, final_g, ys)
```
